```python
import math
import jax, jax.numpy as jnp
from jax import lax
import numpy as np

D_MODEL = 1024
BATCH = 4
SEQ = 8192
DEPTH = 1

N_META = 16
HEAD_DIM = 64
N_Q_HEADS = 16
N_KV_HEADS = 4
GQA_GROUP = N_Q_HEADS // N_KV_HEADS
WINDOW = 128
SSM_WIDTH = D_MODEL
SSM_GROUP_CH = 16
SSM_GROUPS = SSM_WIDTH // SSM_GROUP_CH
SSM_STATE = 64
D_FF = -(-8 * D_MODEL // (3 * 256)) * 256
Q_W = N_Q_HEADS * HEAD_DIM
KV_W = N_KV_HEADS * HEAD_DIM
IN_COLS = Q_W + 2 * KV_W + SSM_WIDTH + 2 * D_MODEL
EPS = 1e-6
DT_MIN = 1e-3
DT_MAX = 1e-1

kernel_name = "hybrid_swa_sink_s5_gated_block"


def rmsnorm(x, g):
    xf = x.astype(jnp.float32)
    xf = xf * lax.rsqrt(jnp.mean(xf * xf, axis=-1, keepdims=True) + EPS)
    return (xf * g.astype(jnp.float32)).astype(x.dtype)


def softmax_with_sink(s, sink):
    sink = sink.astype(jnp.float32)[:, :, None, None]
    m = jnp.maximum(jnp.max(s, axis=-1, keepdims=True), sink)
    e = jnp.exp(s - m)
    return e / (jnp.sum(e, axis=-1, keepdims=True) + jnp.exp(sink - m))


def sliding_window_attention(q, k, v, sinks):
    b, L = q.shape[0], q.shape[1]
    n_blk = (L - N_META) // WINDOW
    scale = HEAD_DIM ** -0.5
    sink = sinks.reshape(N_KV_HEADS, GQA_GROUP)
    qm = q[:, :N_META].reshape(b, N_META, N_KV_HEADS, GQA_GROUP, HEAD_DIM)
    km, vm = k[:, :N_META], v[:, :N_META]
    s_m = jnp.einsum('bqkrd,bskd->bkrqs', qm, km, preferred_element_type=jnp.float32) * scale
    causal = jnp.tril(jnp.ones((N_META, N_META), dtype=bool))
    s_m = jnp.where(causal, s_m, -jnp.inf)
    p_m = softmax_with_sink(s_m, sink).astype(v.dtype)
    o_m = jnp.einsum('bkrqs,bskd->bqkrd', p_m, vm).reshape(b, N_META, Q_W)
    qb = q[:, N_META:].reshape(b, n_blk, WINDOW, N_KV_HEADS, GQA_GROUP, HEAD_DIM)
    kb = k[:, N_META:].reshape(b, n_blk, WINDOW, N_KV_HEADS, HEAD_DIM)
    vb = v[:, N_META:].reshape(b, n_blk, WINDOW, N_KV_HEADS, HEAD_DIM)
    pad = ((0, 0), (1, 0), (0, 0), (0, 0), (0, 0))
    k_prev = jnp.pad(kb[:, :-1], pad)
    v_prev = jnp.pad(vb[:, :-1], pad)
    meta_shape = (b, n_blk, N_META, N_KV_HEADS, HEAD_DIM)
    k_win = jnp.concatenate([jnp.broadcast_to(km[:, None], meta_shape), k_prev, kb], axis=2)
    v_win = jnp.concatenate([jnp.broadcast_to(vm[:, None], meta_shape), v_prev, vb], axis=2)
    s = jnp.einsum('bnqkrd,bnskd->bnkrqs', qb, k_win, preferred_element_type=jnp.float32) * scale
    qi = jnp.arange(WINDOW)[:, None]
    kj = jnp.arange(WINDOW)[None, :]
    blk = jnp.arange(n_blk)[:, None, None]
    meta_vis = jnp.ones((n_blk, WINDOW, N_META), dtype=bool)
    prev_vis = (kj > qi)[None] & (blk > 0)
    cur_vis = jnp.broadcast_to((kj <= qi)[None], (n_blk, WINDOW, WINDOW))
    mask = jnp.concatenate([meta_vis, prev_vis, cur_vis], axis=-1)
    s = jnp.where(mask[None, :, None, None], s, -jnp.inf)
    p = softmax_with_sink(s, sink).astype(v.dtype)
    o_r = jnp.einsum('bnkrqs,bnskd->bnqkrd', p, v_win).reshape(b, n_blk * WINDOW, Q_W)
    return jnp.concatenate([o_m, o_r], axis=1)


def s5_ssm(u, lam_re, lam_im, log_dt, b_re, b_im, c_re, c_im, d_skip):
    bsz, L = u.shape[0], u.shape[1]
    uf = u.astype(jnp.float32)
    ug = uf.reshape(bsz, L, SSM_GROUPS, SSM_GROUP_CH)
    dt = jnp.exp(log_dt.astype(jnp.float32))[:, None]
    lr, li = lam_re.astype(jnp.float32), lam_im.astype(jnp.float32)
    mag = jnp.exp(lr * dt)
    ar, ai = mag * jnp.cos(li * dt), mag * jnp.sin(li * dt)
    den = lr * lr + li * li
    nr, ni = ar - 1.0, ai
    fr, fi = (nr * lr + ni * li) / den, (ni * lr - nr * li) / den
    br, bi = b_re.astype(jnp.float32), b_im.astype(jnp.float32)
    bbar_re = fr[..., None] * br - fi[..., None] * bi
    bbar_im = fr[..., None] * bi + fi[..., None] * br
    xr = jnp.einsum('gpc,blgc->blgp', bbar_re, ug)
    xi = jnp.einsum('gpc,blgc->blgp', bbar_im, ug)
    a_re = jnp.broadcast_to(ar[None, None], (1, L, SSM_GROUPS, SSM_STATE))
    a_im = jnp.broadcast_to(ai[None, None], (1, L, SSM_GROUPS, SSM_STATE))

    def combine(e1, e2):
        a1r, a1i, b1r, b1i = e1
        a2r, a2i, b2r, b2i = e2
        return (a1r * a2r - a1i * a2i,
                a1r * a2i + a1i * a2r,
                a2r * b1r - a2i * b1i + b2r,
                a2r * b1i + a2i * b1r + b2i)

    _, _, sr, si = lax.associative_scan(combine, (a_re, a_im, xr, xi), axis=1)
    y = (jnp.einsum('gcp,blgp->blgc', c_re.astype(jnp.float32), sr)
         - jnp.einsum('gcp,blgp->blgc', c_im.astype(jnp.float32), si))
    y = y.reshape(bsz, L, SSM_WIDTH) + d_skip.astype(jnp.float32) * uf
    return y


def setup_inputs(seed: int = 0) -> dict:
    key = jax.random.key(seed)
    ks = jax.random.split(key, 24)
    f32 = jnp.float32
    nrm = lambda k, shape, s: jax.random.normal(k, shape, f32) * s
    gain = lambda k, shape: 1.0 + 0.02 * jax.random.normal(k, shape, f32)
    n_idx = jnp.arange(SSM_STATE, dtype=f32)
    return {
        "x": nrm(ks[0], (BATCH, SEQ, D_MODEL), 1.0),
        "meta_tokens": nrm(ks[1], (N_META, D_MODEL), 1.0),
        "norm_mix": gain(ks[2], (DEPTH, D_MODEL)),
        "w_in": nrm(ks[3], (DEPTH, D_MODEL, IN_COLS), D_MODEL ** -0.5),
        "q_norm": gain(ks[4], (DEPTH, HEAD_DIM)),
        "k_norm": gain(ks[5], (DEPTH, HEAD_DIM)),
        "attn_sinks": nrm(ks[6], (DEPTH, N_Q_HEADS), 0.5),
        "lam_re": -0.5 + nrm(ks[7], (DEPTH, SSM_GROUPS, SSM_STATE), 0.01),
        "lam_im": math.pi * n_idx + nrm(ks[8], (DEPTH, SSM_GROUPS, SSM_STATE), 0.01),
        "log_dt": jax.random.uniform(ks[9], (DEPTH, SSM_GROUPS), f32, math.log(DT_MIN), math.log(DT_MAX)),
        "ssm_b_re": nrm(ks[10], (DEPTH, SSM_GROUPS, SSM_STATE, SSM_GROUP_CH), (2 * SSM_GROUP_CH) ** -0.5),
        "ssm_b_im": nrm(ks[11], (DEPTH, SSM_GROUPS, SSM_STATE, SSM_GROUP_CH), (2 * SSM_GROUP_CH) ** -0.5),
        "ssm_c_re": nrm(ks[12], (DEPTH, SSM_GROUPS, SSM_GROUP_CH, SSM_STATE), (2 * SSM_STATE) ** -0.5),
        "ssm_c_im": nrm(ks[13], (DEPTH, SSM_GROUPS, SSM_GROUP_CH, SSM_STATE), (2 * SSM_STATE) ** -0.5),
        "ssm_d": nrm(ks[14], (DEPTH, SSM_WIDTH), 1.0),
        "w_glu": nrm(ks[15], (DEPTH, SSM_WIDTH, 2 * D_MODEL), SSM_WIDTH ** -0.5),
        "attn_branch_norm": gain(ks[16], (DEPTH, D_MODEL)),
        "ssm_branch_norm": gain(ks[17], (DEPTH, D_MODEL)),
        "w_out": nrm(ks[18], (DEPTH, D_MODEL, D_MODEL), D_MODEL ** -0.5),
        "norm_ffn": gain(ks[19], (DEPTH, D_MODEL)),
        "w_ffn_in": nrm(ks[20], (DEPTH, D_MODEL, 2 * D_FF), D_MODEL ** -0.5),
        "w_ffn_out": nrm(ks[21], (DEPTH, D_FF, D_MODEL), D_FF ** -0.5),
    }


def reference(x, meta_tokens, norm_mix, w_in, q_norm, k_norm, attn_sinks, lam_re, lam_im, log_dt,
              ssm_b_re, ssm_b_im, ssm_c_re, ssm_c_im, ssm_d, w_glu, attn_branch_norm, ssm_branch_norm,
              w_out, norm_ffn, w_ffn_in, w_ffn_out):
    b = x.shape[0]
    meta = jnp.broadcast_to(meta_tokens.astype(x.dtype)[None], (b, N_META, D_MODEL))
    h = jnp.concatenate([meta, x], axis=1)
    L = h.shape[1]
    offs = np.cumsum([Q_W, KV_W, KV_W, SSM_WIDTH, D_MODEL]).tolist()
    for i in range(DEPTH):
        xn = rmsnorm(h, norm_mix[i])
        proj = xn @ w_in[i]
        q, k, v, u, g_att, g_ssm = jnp.split(proj, offs, axis=-1)
        q = rmsnorm(q.reshape(b, L, N_Q_HEADS, HEAD_DIM), q_norm[i])
        k = rmsnorm(k.reshape(b, L, N_KV_HEADS, HEAD_DIM), k_norm[i])
        v = v.reshape(b, L, N_KV_HEADS, HEAD_DIM)
        attn = sliding_window_attention(q, k, v, attn_sinks[i])
        y = s5_ssm(u, lam_re[i], lam_im[i], log_dt[i], ssm_b_re[i], ssm_b_im[i],
                   ssm_c_re[i], ssm_c_im[i], ssm_d[i])
        z = jax.nn.gelu(y).astype(h.dtype)
        za, zb = jnp.split(z @ w_glu[i], 2, axis=-1)
        ssm = za * jax.nn.sigmoid(zb)
        merged = (jax.nn.sigmoid(g_att) * rmsnorm(attn, attn_branch_norm[i])
                  + jax.nn.sigmoid(g_ssm) * rmsnorm(ssm, ssm_branch_norm[i]))
        h = h + merged @ w_out[i]
        hn = rmsnorm(h, norm_ffn[i])
        gate, up = jnp.split(hn @ w_ffn_in[i], 2, axis=-1)
        h = h + (jax.nn.silu(gate) * up) @ w_ffn_out[i]
    return h[:, N_META:]
```

```python
import functools
import math

import jax
import jax.numpy as jnp
from jax import lax
from jax.experimental import pallas as pl
from jax.experimental.pallas import tpu as pltpu

D_MODEL = 1024
N_META = 16
HEAD_DIM = 64
N_Q_HEADS = 16
N_KV_HEADS = 4
GQA_GROUP = N_Q_HEADS // N_KV_HEADS
WINDOW = 128
SSM_GROUP_CH = 16
SSM_GROUPS = D_MODEL // SSM_GROUP_CH
SSM_STATE = 64
D_FF = 2816
Q_W = N_Q_HEADS * HEAD_DIM
KV_W = N_KV_HEADS * HEAD_DIM
QKV_W = Q_W + 2 * KV_W
EPS = 1e-6

CHUNK = 16
NC = 256
SUPER = CHUNK * NC
GROUPS_PER_STEP = 8
SCAN_LEVELS = 8
FFN_SPLIT = 2
VMEM_LIMIT = 56 * 1024 * 1024

_F32 = jnp.float32
_BF16 = jnp.bfloat16
_NT = (((1,), (1,)), ((), ()))
_TN = (((0,), (0,)), ((), ()))


def _rms_rows(x, gain):
    return x * lax.rsqrt(jnp.mean(x * x, axis=-1, keepdims=True) + EPS) * gain


def _proj_t_kernel(x_ref, w_ref, nm_ref, gain_ref, o_ref, *, norm_heads):
    xn = _rms_rows(x_ref[...], nm_ref[...]).astype(_BF16)
    p = lax.dot_general(w_ref[...], xn, _NT, preferred_element_type=_F32)
    if norm_heads:
        rows = norm_heads * HEAD_DIM
        tm = p.shape[1]
        hd = p[:rows].reshape(norm_heads, HEAD_DIM, tm)
        ms = jnp.mean(hd * hd, axis=1, keepdims=True)
        hd = hd * lax.rsqrt(ms + EPS) * gain_ref[...].reshape(norm_heads, HEAD_DIM, 1)
        o_ref[:rows, :] = hd.reshape(rows, tm).astype(o_ref.dtype)
        o_ref[rows:, :] = p[rows:].astype(o_ref.dtype)
    else:
        o_ref[...] = p.astype(o_ref.dtype)


def _proj_t(x, w_t, norm_mix, gain, *, norm_heads, tm, name):
    b, s, d = x.shape
    n = w_t.shape[0]
    return pl.pallas_call(
        functools.partial(_proj_t_kernel, norm_heads=norm_heads),
        grid=(b, s // tm),
        in_specs=[
            pl.BlockSpec((None, tm, d), lambda i, j: (i, j, 0)),
            pl.BlockSpec((n, d), lambda i, j: (0, 0)),
            pl.BlockSpec((1, d), lambda i, j: (0, 0)),
            pl.BlockSpec(gain.shape, lambda i, j: (0, 0)),
        ],
        out_specs=pl.BlockSpec((None, n, tm), lambda i, j: (i, 0, j)),
        out_shape=jax.ShapeDtypeStruct((b, n, s), _BF16),
        compiler_params=pltpu.CompilerParams(
            dimension_semantics=("arbitrary", "arbitrary"), vmem_limit_bytes=VMEM_LIMIT),
        name=name,
    )(x, w_t, norm_mix, gain)


def _proj_u_perm(x, w_t, norm_mix, gain):
    b, s, d = x.shape
    n = w_t.shape[0]
    n_super = s // SUPER
    xv = x.reshape(b, n_super, NC, CHUNK * d)
    return pl.pallas_call(
        functools.partial(_proj_t_kernel, norm_heads=0),
        grid=(b, n_super, CHUNK),
        in_specs=[
            pl.BlockSpec((None, None, NC, d), lambda i, st, j: (i, st, 0, j)),
            pl.BlockSpec((n, d), lambda i, st, j: (0, 0)),
            pl.BlockSpec((1, d), lambda i, st, j: (0, 0)),
            pl.BlockSpec(gain.shape, lambda i, st, j: (0, 0)),
        ],
        out_specs=pl.BlockSpec((None, n, NC), lambda i, st, j: (i, 0, st * CHUNK + j)),
        out_shape=jax.ShapeDtypeStruct((b, n, s), _BF16),
        compiler_params=pltpu.CompilerParams(
            dimension_semantics=("arbitrary", "arbitrary", "arbitrary"), vmem_limit_bytes=VMEM_LIMIT),
        name="proj_u_perm",
    )(xv, w_t, norm_mix, gain)


def _ssm_prep_kernel(colp_ref, rowp_ref, bre_ref, bim_ref, cre_ref, cim_ref, am_ref,
                     w1_ref, g_ref, pw_ref, sm_ref):
    hi = lax.Precision.HIGHEST
    lr, li, dt = colp_ref[:, 0:1], colp_ref[:, 1:2], colp_ref[:, 2:3]
    mag = jnp.exp(lr * dt)
    ar, ai = mag * jnp.cos(li * dt), mag * jnp.sin(li * dt)
    den = lr * lr + li * li
    nr, ni = ar - 1.0, ai
    fr, fi = (nr * lr + ni * li) / den, (ni * lr - nr * li) / den
    b_re, b_im = bre_ref[...], bim_ref[...]
    bbr = fr * b_re - fi * b_im
    bbi = fr * b_im + fi * b_re
    width = CHUNK * SSM_GROUP_CH
    lane = lax.broadcasted_iota(jnp.int32, (SSM_GROUP_CH, width), 1)
    row = lax.broadcasted_iota(jnp.int32, (SSM_GROUP_CH, width), 0)
    tile_mat = (lane % SSM_GROUP_CH == row).astype(_F32)
    bbr_t = jnp.dot(bbr, tile_mat, precision=hi, preferred_element_type=_F32)
    bbi_t = jnp.dot(bbi, tile_mat, precision=hi, preferred_element_type=_F32)
    lane_p = lax.broadcasted_iota(jnp.int32, (SSM_STATE, width), 1)
    tau = (CHUNK - 1 - lane_p // SSM_GROUP_CH).astype(_F32)
    pm = jnp.exp(lr * dt * tau)
    ang = li * dt * tau
    pr, pi = pm * jnp.cos(ang), pm * jnp.sin(ang)
    xr = pr * bbr_t - pi * bbi_t
    xi = pr * bbi_t + pi * bbr_t
    krev = (jnp.dot(cre_ref[...], xr, precision=hi, preferred_element_type=_F32)
            - jnp.dot(cim_ref[...], xi, precision=hi, preferred_element_type=_F32))
    rz = jnp.concatenate([krev, jnp.zeros_like(krev)], axis=1)
    rows = []
    for t in range(CHUNK):
        off = (CHUNK - 1 - t) * SSM_GROUP_CH
        rows.append(rz[:, off:off + width])
    toep = jnp.concatenate(rows, axis=0)
    w1_ref[...] = jnp.concatenate([xr, xi, toep], axis=0).astype(w1_ref.dtype)
    lr_r, li_r, dt_r = rowp_ref[0:1, :], rowp_ref[1:2, :], rowp_ref[2:3, :]
    trow = (lax.broadcasted_iota(jnp.int32, (width, SSM_STATE), 0) // SSM_GROUP_CH + 1).astype(_F32)
    pmg = jnp.exp(lr_r * dt_r * trow)
    angg = li_r * dt_r * trow
    gar, gai = pmg * jnp.cos(angg), pmg * jnp.sin(angg)
    crt = jnp.tile(cre_ref[...], (CHUNK, 1))
    cit = jnp.tile(cim_ref[...], (CHUNK, 1))
    g_re = crt * gar - cit * gai
    g_im = -(crt * gai + cit * gar)
    g_ref[...] = jnp.concatenate([g_re, g_im], axis=1).astype(g_ref.dtype)
    lvl = lax.broadcasted_iota(jnp.int32, (SSM_STATE, SCAN_LEVELS), 1)
    e = (CHUNK * jnp.left_shift(1, lvl)).astype(_F32)
    pme = jnp.exp(lr * dt * e)
    ange = li * dt * e
    pw_ref[...] = jnp.concatenate([pme * jnp.cos(ange), pme * jnp.sin(ange)], axis=0)
    am = am_ref[...]
    sm_ref[...] = jnp.concatenate([jnp.sum(xr * am, axis=1, keepdims=True),
                                   jnp.sum(xi * am, axis=1, keepdims=True)], axis=0)


def _ssm_prep(colp, rowp, b_re, b_im, c_re, c_im, a_meta):
    g = SSM_GROUPS
    width = CHUNK * SSM_GROUP_CH
    spec3 = lambda s1, s2: pl.BlockSpec((None, s1, s2), lambda i: (i, 0, 0))
    return pl.pallas_call(
        _ssm_prep_kernel,
        grid=(g,),
        in_specs=[spec3(SSM_STATE, 4), spec3(4, SSM_STATE),
                  spec3(SSM_STATE, SSM_GROUP_CH), spec3(SSM_STATE, SSM_GROUP_CH),
                  spec3(SSM_GROUP_CH, SSM_STATE), spec3(SSM_GROUP_CH, SSM_STATE),
                  spec3(1, width)],
        out_specs=[spec3(2 * SSM_STATE + width, width), spec3(width, 2 * SSM_STATE),
                   spec3(2 * SSM_STATE, SCAN_LEVELS), spec3(2 * SSM_STATE, 1)],
        out_shape=[jax.ShapeDtypeStruct((g, 2 * SSM_STATE + width, width), _BF16),
                   jax.ShapeDtypeStruct((g, width, 2 * SSM_STATE), _BF16),
                   jax.ShapeDtypeStruct((g, 2 * SSM_STATE, SCAN_LEVELS), _F32),
                   jax.ShapeDtypeStruct((g, 2 * SSM_STATE, 1), _F32)],
        compiler_params=pltpu.CompilerParams(dimension_semantics=("arbitrary",)),
        name="ssm_prep",
    )(colp, rowp, b_re, b_im, c_re, c_im, a_meta)


def _ssm_kernel(u_ref, w1_ref, g_ref, pw_ref, sm_ref, d_ref, z_ref, carry_ref):
    st = pl.program_id(2)

    @pl.when(st == 0)
    def _():
        carry_ref[...] = sm_ref[...]

    ns = SSM_STATE
    width = CHUNK * SSM_GROUP_CH
    lane = lax.broadcasted_iota(jnp.int32, (ns, NC), 1)
    lane0 = lane == 0
    for gl in range(GROUPS_PER_STEP):
        r0 = gl * SSM_GROUP_CH
        a_t = jnp.concatenate(
            [u_ref[r0:r0 + SSM_GROUP_CH, j * NC:(j + 1) * NC] for j in range(CHUNK)], axis=0)
        r1 = jnp.dot(w1_ref[gl], a_t, preferred_element_type=_F32)
        pw = pw_ref[gl]
        carry = carry_ref[gl]
        c_re, c_im = carry[:ns], carry[ns:]
        p_re, p_im = pw[:ns, 0:1], pw[ns:, 0:1]
        s_re = r1[:ns] + jnp.where(lane0, p_re * c_re - p_im * c_im, 0.0)
        s_im = r1[ns:2 * ns] + jnp.where(lane0, p_re * c_im + p_im * c_re, 0.0)
        for lvl in range(SCAN_LEVELS):
            sh = 1 << lvl
            p_re, p_im = pw[:ns, lvl:lvl + 1], pw[ns:, lvl:lvl + 1]
            keep = lane >= sh
            t_re = jnp.where(keep, pltpu.roll(s_re, sh, axis=1), 0.0)
            t_im = jnp.where(keep, pltpu.roll(s_im, sh, axis=1), 0.0)
            s_re, s_im = s_re + p_re * t_re - p_im * t_im, s_im + p_re * t_im + p_im * t_re
        prev_re = jnp.where(lane0, c_re, pltpu.roll(s_re, 1, axis=1))
        prev_im = jnp.where(lane0, c_im, pltpu.roll(s_im, 1, axis=1))
        carry_ref[gl] = jnp.concatenate([s_re[:, NC - 1:NC], s_im[:, NC - 1:NC]], axis=0)
        s_prev = jnp.concatenate([prev_re, prev_im], axis=0).astype(_BF16)
        y = r1[2 * ns:] + jnp.dot(g_ref[gl], s_prev, preferred_element_type=_F32)
        d_col = jnp.tile(d_ref[r0:r0 + SSM_GROUP_CH, :], (CHUNK, 1))
        y = y + d_col * a_t.astype(_F32)
        z = jax.nn.gelu(y)
        for t in range(CHUNK):
            z_ref[r0:r0 + SSM_GROUP_CH, t * NC:(t + 1) * NC] = (
                z[t * SSM_GROUP_CH:(t + 1) * SSM_GROUP_CH, :].astype(z_ref.dtype))


def _ssm_core(u_t, w1, g_mat, pw, s_meta, d_col):
    b, n, s = u_t.shape
    gps = GROUPS_PER_STEP
    rows = gps * SSM_GROUP_CH
    width = CHUNK * SSM_GROUP_CH
    return pl.pallas_call(
        _ssm_kernel,
        grid=(n // rows, b, s // SUPER),
        in_specs=[
            pl.BlockSpec((None, rows, SUPER), lambda gb, i, st: (i, gb, st)),
            pl.BlockSpec((gps, 2 * SSM_STATE + width, width), lambda gb, i, st: (gb, 0, 0)),
            pl.BlockSpec((gps, width, 2 * SSM_STATE), lambda gb, i, st: (gb, 0, 0)),
            pl.BlockSpec((gps, 2 * SSM_STATE, SCAN_LEVELS), lambda gb, i, st: (gb, 0, 0)),
            pl.BlockSpec((gps, 2 * SSM_STATE, 1), lambda gb, i, st: (gb, 0, 0)),
            pl.BlockSpec((rows, 1), lambda gb, i, st: (gb, 0)),
        ],
        out_specs=pl.BlockSpec((None, rows, SUPER), lambda gb, i, st: (i, gb, st)),
        out_shape=jax.ShapeDtypeStruct((b, n, s), _BF16),
        scratch_shapes=[pltpu.VMEM((gps, 2 * SSM_STATE, 1), _F32)],
        compiler_params=pltpu.CompilerParams(
            dimension_semantics=("arbitrary", "arbitrary", "arbitrary"), vmem_limit_bytes=VMEM_LIMIT),
        name="ssm_core",
    )(u_t, w1, g_mat, pw, s_meta, d_col)


def _attn_kernel(q_ref, kc_ref, vc_ref, kp_ref, vp_ref, km_ref, vm_ref, sink_ref, o_ref, acc_ref):
    has_prev = pl.program_id(1) > 0
    wq = GQA_GROUP * WINDOW
    kj = lax.broadcasted_iota(jnp.int32, (WINDOW, wq), 0)
    qi = lax.broadcasted_iota(jnp.int32, (WINDOW, wq), 1) % WINDOW
    in_cur = kj <= qi
    for h in range(N_KV_HEADS):
        r0 = h * HEAD_DIM
        q4 = jnp.concatenate(
            [q_ref[(h * GQA_GROUP + r) * HEAD_DIM:(h * GQA_GROUP + r + 1) * HEAD_DIM, :]
             for r in range(GQA_GROUP)], axis=1)
        s_cur = lax.dot_general(kc_ref[r0:r0 + HEAD_DIM, :], q4, _TN, preferred_element_type=_F32)
        s_prev = lax.dot_general(kp_ref[r0:r0 + HEAD_DIM, :], q4, _TN, preferred_element_type=_F32)
        s_meta = lax.dot_general(km_ref[r0:r0 + HEAD_DIM, :], q4, _TN, preferred_element_type=_F32)
        s_sel = jnp.where(in_cur, s_cur, jnp.where(has_prev, s_prev, -jnp.inf))
        sink = sink_ref[h]
        m = jnp.maximum(jnp.maximum(jnp.max(s_sel, axis=0, keepdims=True),
                                    jnp.max(s_meta, axis=0, keepdims=True)), sink)
        e_sel = jnp.exp(s_sel - m)
        e_meta = jnp.exp(s_meta - m)
        den = (jnp.sum(e_sel, axis=0, keepdims=True) + jnp.sum(e_meta, axis=0, keepdims=True)
               + jnp.exp(sink - m))
        p_cur = jnp.where(in_cur, e_sel, 0.0).astype(_BF16)
        p_prev = jnp.where(in_cur, 0.0, e_sel).astype(_BF16)
        o = (jnp.dot(vc_ref[r0:r0 + HEAD_DIM, :], p_cur, preferred_element_type=_F32)
             + jnp.dot(vp_ref[r0:r0 + HEAD_DIM, :], p_prev, preferred_element_type=_F32)
             + jnp.dot(vm_ref[r0:r0 + HEAD_DIM, :], e_meta.astype(_BF16), preferred_element_type=_F32))
        o = o / den
        for r in range(GQA_GROUP):
            hq = h * GQA_GROUP + r
            acc_ref[hq * HEAD_DIM:(hq + 1) * HEAD_DIM, :] = o[:, r * WINDOW:(r + 1) * WINDOW]
    o_ref[...] = acc_ref[...].T.astype(o_ref.dtype)


def _attention(qkv_t, qkv_meta_t, sink_rows):
    b, _, s = qkv_t.shape
    n_blk = s // WINDOW
    kblk = Q_W // KV_W
    return pl.pallas_call(
        _attn_kernel,
        grid=(b, n_blk),
        in_specs=[
            pl.BlockSpec((None, Q_W, WINDOW), lambda i, n: (i, 0, n)),
            pl.BlockSpec((None, KV_W, WINDOW), lambda i, n: (i, kblk, n)),
            pl.BlockSpec((None, KV_W, WINDOW), lambda i, n: (i, kblk + 1, n)),
            pl.BlockSpec((None, KV_W, WINDOW), lambda i, n: (i, kblk, jnp.maximum(n - 1, 0))),
            pl.BlockSpec((None, KV_W, WINDOW), lambda i, n: (i, kblk + 1, jnp.maximum(n - 1, 0))),
            pl.BlockSpec((None, KV_W, N_META), lambda i, n: (0, kblk, 0)),
            pl.BlockSpec((None, KV_W, N_META), lambda i, n: (0, kblk + 1, 0)),
            pl.BlockSpec(sink_rows.shape, lambda i, n: (0, 0, 0)),
        ],
        out_specs=pl.BlockSpec((None, WINDOW, Q_W), lambda i, n: (i, n, 0)),
        out_shape=jax.ShapeDtypeStruct((b, s, Q_W), _BF16),
        scratch_shapes=[pltpu.VMEM((Q_W, WINDOW), _F32)],
        compiler_params=pltpu.CompilerParams(
            dimension_semantics=("arbitrary", "arbitrary"), vmem_limit_bytes=VMEM_LIMIT),
        name="swa_attention",
    )(qkv_t, qkv_t, qkv_t, qkv_t, qkv_t, qkv_meta_t, qkv_meta_t, sink_rows)


def _tail_kernel(x_ref, attn_ref, z_ref, nm_ref, wg_ref, wglu_ref, abn_ref, sbn_ref, wout_ref,
                 nf_ref, wfi_ref, wfo_ref, o_ref, *, pairs):
    d = D_MODEL
    x = jnp.concatenate([x_ref[:, p * d:(p + 1) * d] for p in range(pairs)], axis=0)
    attn = jnp.concatenate([attn_ref[:, p * d:(p + 1) * d] for p in range(pairs)], axis=0)
    xn = _rms_rows(x, nm_ref[...]).astype(_BF16)
    gates = jnp.dot(xn, wg_ref[...], preferred_element_type=_F32)
    zz = lax.dot_general(z_ref[...], wglu_ref[...], _TN, preferred_element_type=_F32)
    ssm = zz[:, :d] * jax.nn.sigmoid(zz[:, d:])
    merged = (jax.nn.sigmoid(gates[:, :d]) * _rms_rows(attn.astype(_F32), abn_ref[...])
              + jax.nn.sigmoid(gates[:, d:]) * _rms_rows(ssm, sbn_ref[...]))
    h = x + jnp.dot(merged.astype(_BF16), wout_ref[...], preferred_element_type=_F32)
    hn = _rms_rows(h, nf_ref[...]).astype(_BF16)
    out = h
    fw = D_FF // FFN_SPLIT
    for c in range(FFN_SPLIT):
        gate = jnp.dot(hn, wfi_ref[:, c * fw:(c + 1) * fw], preferred_element_type=_F32)
        up = jnp.dot(hn, wfi_ref[:, D_FF + c * fw:D_FF + (c + 1) * fw], preferred_element_type=_F32)
        act = (jax.nn.silu(gate) * up).astype(_BF16)
        out = out + jnp.dot(act, wfo_ref[c * fw:(c + 1) * fw, :], preferred_element_type=_F32)
    rows = out.shape[0] // pairs
    for p in range(pairs):
        o_ref[:, p * d:(p + 1) * d] = out[p * rows:(p + 1) * rows, :]


def _tail(x, attn, z_t, norm_mix, w_g, w_glu, abn, sbn, w_out, norm_ffn, w_fi, w_fo, *, pairs=2):
    b, s, d = x.shape
    n_super = s // SUPER
    steps = CHUNK // pairs
    xv = x.reshape(b, n_super, NC, CHUNK * d)
    av = attn.reshape(b, n_super, NC, CHUNK * d)
    view_spec = pl.BlockSpec((None, None, NC, pairs * d), lambda i, st, t: (i, st, 0, t))
    const = lambda a: pl.BlockSpec(a.shape, lambda i, st, t: (0,) * a.ndim,
                                   pipeline_mode=pl.Buffered(1))
    out = pl.pallas_call(
        functools.partial(_tail_kernel, pairs=pairs),
        grid=(b, n_super, steps),
        in_specs=[
            view_spec, view_spec,
            pl.BlockSpec((None, d, pairs * NC), lambda i, st, t: (i, 0, st * steps + t)),
            const(norm_mix), const(w_g), const(w_glu), const(abn), const(sbn), const(w_out),
            const(norm_ffn), const(w_fi), const(w_fo),
        ],
        out_specs=view_spec,
        out_shape=jax.ShapeDtypeStruct(xv.shape, x.dtype),
        compiler_params=pltpu.CompilerParams(
            dimension_semantics=("arbitrary", "arbitrary", "arbitrary"), vmem_limit_bytes=VMEM_LIMIT),
        name="tail",
    )(xv, av, z_t, norm_mix, w_g, w_glu, abn, sbn, w_out, norm_ffn, w_fi, w_fo)
    return out.reshape(b, s, d)


def kernel(x, meta_tokens, norm_mix, w_in, q_norm, k_norm, attn_sinks, lam_re, lam_im, log_dt,
           ssm_b_re, ssm_b_im, ssm_c_re, ssm_c_im, ssm_d, w_glu, attn_branch_norm, ssm_branch_norm,
           w_out, norm_ffn, w_ffn_in, w_ffn_out):
    w = w_in[0]
    w_qkv_t = w[:, :QKV_W].T.astype(_BF16)
    w_u_t = w[:, QKV_W:QKV_W + D_MODEL].T.astype(_BF16)
    w_g = w[:, QKV_W + D_MODEL:].astype(_BF16)
    scale = HEAD_DIM ** -0.5
    qk_gain = jnp.concatenate([jnp.tile(q_norm[0] * scale, N_Q_HEADS),
                               jnp.tile(k_norm[0], N_KV_HEADS)])[:, None].astype(_F32)
    no_gain = jnp.zeros((8, 1), _F32)
    sink_rows = jnp.repeat(attn_sinks[0].reshape(N_KV_HEADS, 1, GQA_GROUP), WINDOW, axis=2).astype(_F32)
    dt = jnp.exp(log_dt[0].astype(_F32))
    lr, li = lam_re[0].astype(_F32), lam_im[0].astype(_F32)
    dtb = jnp.broadcast_to(dt[:, None], lr.shape)
    colp = jnp.stack([lr, li, dtb, jnp.zeros_like(lr)], axis=2)
    rowp = jnp.stack([lr, li, dtb, jnp.zeros_like(lr)], axis=1)
    d_col = ssm_d[0].astype(_F32)[:, None]

    meta = meta_tokens.astype(_F32)[None]
    qkv_meta_t = _proj_t(meta, w_qkv_t, norm_mix, qk_gain, norm_heads=N_Q_HEADS + N_KV_HEADS,
                         tm=N_META, name="proj_qkv_meta")
    u_meta_t = _proj_t(meta, w_u_t, norm_mix, no_gain, norm_heads=0, tm=N_META, name="proj_u_meta")
    a_meta = (u_meta_t[0].astype(_F32).reshape(SSM_GROUPS, SSM_GROUP_CH, CHUNK)
              .transpose(0, 2, 1).reshape(SSM_GROUPS, 1, CHUNK * SSM_GROUP_CH))
    w1, g_mat, pw, s_meta = _ssm_prep(colp, rowp, ssm_b_re[0].astype(_F32), ssm_b_im[0].astype(_F32),
                                      ssm_c_re[0].astype(_F32), ssm_c_im[0].astype(_F32), a_meta)

    qkv_t = _proj_t(x, w_qkv_t, norm_mix, qk_gain, norm_heads=N_Q_HEADS + N_KV_HEADS,
                    tm=512, name="proj_qkv")
    u_t = _proj_u_perm(x, w_u_t, norm_mix, no_gain)
    attn = _attention(qkv_t, qkv_meta_t, sink_rows)
    z_t = _ssm_core(u_t, w1, g_mat, pw, s_meta, d_col)
    return _tail(x, attn, z_t, norm_mix, w_g, w_glu[0].astype(_BF16), attn_branch_norm,
                 ssm_branch_norm, w_out[0].astype(_BF16), norm_ffn, w_ffn_in[0].astype(_BF16),
                 w_ffn_out[0].astype(_BF16))
```

```python
import functools

import jax
import jax.numpy as jnp
from jax import lax
from jax.experimental import pallas as pl
from jax.experimental.pallas import tpu as pltpu

D_MODEL = 1024
N_META = 16
HEAD_DIM = 64
N_Q_HEADS = 16
N_KV_HEADS = 4
GQA_GROUP = N_Q_HEADS // N_KV_HEADS
WINDOW = 128
SSM_GROUP_CH = 16
SSM_GROUPS = D_MODEL // SSM_GROUP_CH
SSM_STATE = 64
D_FF = 2816
Q_W = N_Q_HEADS * HEAD_DIM
KV_W = N_KV_HEADS * HEAD_DIM
QKV_W = Q_W + 2 * KV_W
EPS = 1e-6

LANES = 128
CHUNK = 16
TOEP = CHUNK * SSM_GROUP_CH
GROUPS_PER_STEP = 8
FFN_SPLIT = 2
VMEM_LIMIT = 56 * 1024 * 1024

_F32 = jnp.float32
_BF16 = jnp.bfloat16
_NT = (((1,), (1,)), ((), ()))
_TN = (((0,), (0,)), ((), ()))
_HI = lax.Precision.HIGHEST


def _rms_rows(x, gain):
    return x * lax.rsqrt(jnp.mean(x * x, axis=-1, keepdims=True) + EPS) * gain


def _proj_t_body(x, w_ref, nm_ref, gain_ref, o_ref, norm_heads):
    xn = _rms_rows(x, nm_ref[...]).astype(_BF16)
    p = lax.dot_general(w_ref[...], xn, _NT, preferred_element_type=_F32)
    if norm_heads:
        rows = norm_heads * HEAD_DIM
        tm = p.shape[1]
        hd = p[:rows].reshape(norm_heads, HEAD_DIM, tm)
        ms = jnp.mean(hd * hd, axis=1, keepdims=True)
        hd = hd * lax.rsqrt(ms + EPS) * gain_ref[...].reshape(norm_heads, HEAD_DIM, 1)
        o_ref[:rows, :] = hd.reshape(rows, tm).astype(o_ref.dtype)
        o_ref[rows:, :] = p[rows:].astype(o_ref.dtype)
    else:
        o_ref[...] = p.astype(o_ref.dtype)


def _proj_t_kernel(x_ref, w_ref, nm_ref, gain_ref, o_ref, *, norm_heads):
    _proj_t_body(x_ref[...], w_ref, nm_ref, gain_ref, o_ref, norm_heads)


def _proj_t(x, w_t, norm_mix, gain, *, norm_heads, tm, name):
    b, s, d = x.shape
    n = w_t.shape[0]
    return pl.pallas_call(
        functools.partial(_proj_t_kernel, norm_heads=norm_heads),
        grid=(b, s // tm),
        in_specs=[
            pl.BlockSpec((None, tm, d), lambda i, j: (i, j, 0)),
            pl.BlockSpec((n, d), lambda i, j: (0, 0)),
            pl.BlockSpec((1, d), lambda i, j: (0, 0)),
            pl.BlockSpec(gain.shape, lambda i, j: (0, 0)),
        ],
        out_specs=pl.BlockSpec((None, n, tm), lambda i, j: (i, 0, j)),
        out_shape=jax.ShapeDtypeStruct((b, n, s), _BF16),
        compiler_params=pltpu.CompilerParams(
            dimension_semantics=("arbitrary", "arbitrary"), vmem_limit_bytes=VMEM_LIMIT),
        name=name,
    )(x, w_t, norm_mix, gain)


def _row_set_copy(hbm4, buf, sem, bb, jj, slot):
    return hbm4.at[bb, :, jj, :], buf.at[slot], sem.at[slot]


def _proj_u_kernel(x_hbm, w_ref, nm_ref, o_ref, xbuf, sem):
    nj = pl.num_programs(1)
    last = pl.num_programs(0) * nj - 1
    step = pl.program_id(0) * nj + pl.program_id(1)
    slot = step % 2

    def fetch(s, sl):
        return pltpu.make_async_copy(*_row_set_copy(x_hbm, xbuf, sem, s // nj, s % nj, sl))

    @pl.when(step == 0)
    def _():
        fetch(step, slot).start()

    @pl.when(step < last)
    def _():
        fetch(step + 1, 1 - slot).start()

    fetch(step, slot).wait()
    _proj_t_body(xbuf[slot], w_ref, nm_ref, None, o_ref, 0)


def _proj_u_perm(x, w_t, norm_mix):
    b, s, d = x.shape
    n = w_t.shape[0]
    nchunks = s // CHUNK
    x4 = x.reshape(b, nchunks, CHUNK, d)
    return pl.pallas_call(
        _proj_u_kernel,
        grid=(b, CHUNK),
        in_specs=[
            pl.BlockSpec(memory_space=pl.ANY),
            pl.BlockSpec((n, d), lambda i, j: (0, 0)),
            pl.BlockSpec((1, d), lambda i, j: (0, 0)),
        ],
        out_specs=pl.BlockSpec((None, n, nchunks), lambda i, j: (i, 0, j)),
        out_shape=jax.ShapeDtypeStruct((b, n, s), _BF16),
        scratch_shapes=[pltpu.VMEM((2, nchunks, d), x.dtype), pltpu.SemaphoreType.DMA((2,))],
        compiler_params=pltpu.CompilerParams(
            dimension_semantics=("arbitrary", "arbitrary"), vmem_limit_bytes=VMEM_LIMIT),
        name="proj_u_perm",
    )(x4, w_t, norm_mix)


def _ssm_prep_kernel(colp_ref, rowp_ref, bre_ref, bim_ref, cre_ref, cim_ref, am_ref,
                     w1_ref, g_ref, pwr_ref, pwi_ref, sm_ref, *, levels):
    ns, gc = SSM_STATE, SSM_GROUP_CH
    l16 = lax.broadcasted_iota(jnp.int32, (gc, TOEP), 1)
    r16 = lax.broadcasted_iota(jnp.int32, (gc, TOEP), 0)
    tile_c = (l16 % gc == r16).astype(_F32)
    rep_j = (l16 // gc == r16).astype(_F32)
    rt = lax.broadcasted_iota(jnp.int32, (TOEP, CHUNK), 0)
    ct = lax.broadcasted_iota(jnp.int32, (TOEP, CHUNK), 1)
    rep_t = (rt // gc == ct).astype(_F32)
    expand = lambda a, m: jnp.dot(a, m, precision=_HI, preferred_element_type=_F32)
    for gl in range(GROUPS_PER_STEP):
        lr, li, dt = colp_ref[gl, :, 0:1], colp_ref[gl, :, 1:2], colp_ref[gl, :, 2:3]
        mag = jnp.exp(lr * dt)
        ar, ai = mag * jnp.cos(li * dt), mag * jnp.sin(li * dt)
        den = lr * lr + li * li
        nr, ni = ar - 1.0, ai
        fr, fi = (nr * lr + ni * li) / den, (ni * lr - nr * li) / den
        b_re, b_im = bre_ref[gl], bim_ref[gl]
        bbr_t = expand(fr * b_re - fi * b_im, tile_c)
        bbi_t = expand(fr * b_im + fi * b_re, tile_c)
        tau = (CHUNK - 1 - lax.broadcasted_iota(jnp.int32, (ns, CHUNK), 1)).astype(_F32)
        pm = jnp.exp(lr * dt * tau)
        pr = expand(pm * jnp.cos(li * dt * tau), rep_j)
        pi = expand(pm * jnp.sin(li * dt * tau), rep_j)
        xr = pr * bbr_t - pi * bbi_t
        xi = pr * bbi_t + pi * bbr_t
        krev = (jnp.dot(cre_ref[gl], xr, precision=_HI, preferred_element_type=_F32)
                - jnp.dot(cim_ref[gl], xi, precision=_HI, preferred_element_type=_F32))
        rz = jnp.concatenate([krev, jnp.zeros_like(krev)], axis=1)
        toep = jnp.concatenate(
            [rz[:, (CHUNK - 1 - t) * gc:(CHUNK - 1 - t) * gc + TOEP] for t in range(CHUNK)], axis=0)
        w1_ref[gl] = jnp.concatenate([xr, xi, toep], axis=0).astype(w1_ref.dtype)
        lr_r, li_r, dt_r = rowp_ref[gl, 0:1, :], rowp_ref[gl, 1:2, :], rowp_ref[gl, 2:3, :]
        tp1 = (lax.broadcasted_iota(jnp.int32, (CHUNK, ns), 0) + 1).astype(_F32)
        pmg = jnp.exp(lr_r * dt_r * tp1)
        gar = expand(rep_t, pmg * jnp.cos(li_r * dt_r * tp1))
        gai = expand(rep_t, pmg * jnp.sin(li_r * dt_r * tp1))
        crt = jnp.tile(cre_ref[gl], (CHUNK, 1))
        cit = jnp.tile(cim_ref[gl], (CHUNK, 1))
        g_ref[gl] = jnp.concatenate([crt * gar - cit * gai, -(crt * gai + cit * gar)],
                                    axis=1).astype(g_ref.dtype)
        lvl = lax.broadcasted_iota(jnp.int32, (ns, levels), 1)
        e = (CHUNK * jnp.left_shift(1, lvl)).astype(_F32)
        pme = jnp.exp(lr * dt * e)
        per, pei = pme * jnp.cos(li * dt * e), pme * jnp.sin(li * dt * e)
        for l in range(levels):
            pwr_ref[gl, l] = jnp.broadcast_to(per[:, l:l + 1], (ns, LANES))
            pwi_ref[gl, l] = jnp.broadcast_to(pei[:, l:l + 1], (ns, LANES))
        am = am_ref[gl]
        sr = jnp.sum(xr * am, axis=1, keepdims=True)
        si = jnp.sum(xi * am, axis=1, keepdims=True)
        a16r, a16i = per[:, 0:1], pei[:, 0:1]
        sm_ref[gl, 0] = jnp.broadcast_to(sr, (ns, LANES))
        sm_ref[gl, 1] = jnp.broadcast_to(si, (ns, LANES))
        sm_ref[gl, 2] = jnp.broadcast_to(a16r * sr - a16i * si, (ns, LANES))
        sm_ref[gl, 3] = jnp.broadcast_to(a16r * si + a16i * sr, (ns, LANES))


def _ssm_prep(colp, rowp, b_re, b_im, c_re, c_im, a_meta, levels):
    g, gps = SSM_GROUPS, GROUPS_PER_STEP
    ns, gc = SSM_STATE, SSM_GROUP_CH
    spec = lambda *tail: pl.BlockSpec((gps,) + tail, lambda i: (i,) + (0,) * len(tail))
    return pl.pallas_call(
        functools.partial(_ssm_prep_kernel, levels=levels),
        grid=(g // gps,),
        in_specs=[spec(ns, 4), spec(4, ns), spec(ns, gc), spec(ns, gc), spec(gc, ns), spec(gc, ns),
                  spec(1, TOEP)],
        out_specs=[spec(2 * ns + TOEP, TOEP), spec(TOEP, 2 * ns), spec(levels, ns, LANES),
                   spec(levels, ns, LANES), spec(4, ns, LANES)],
        out_shape=[jax.ShapeDtypeStruct((g, 2 * ns + TOEP, TOEP), _BF16),
                   jax.ShapeDtypeStruct((g, TOEP, 2 * ns), _BF16),
                   jax.ShapeDtypeStruct((g, levels, ns, LANES), _F32),
                   jax.ShapeDtypeStruct((g, levels, ns, LANES), _F32),
                   jax.ShapeDtypeStruct((g, 4, ns, LANES), _F32)],
        compiler_params=pltpu.CompilerParams(dimension_semantics=("arbitrary",),
                                             vmem_limit_bytes=VMEM_LIMIT),
        name="ssm_prep",
    )(colp, rowp, b_re, b_im, c_re, c_im, a_meta)


def _ssm_kernel(u_ref, w1_ref, g_ref, pwr_ref, pwi_ref, sm_ref, d_ref, z_ref,
                bre_ref, bim_ref, yint_ref, *, nchunks, levels):
    ns, gc, gps = SSM_STATE, SSM_GROUP_CH, GROUPS_PER_STEP
    nblk = nchunks // LANES
    rows = gps * ns
    for gl in range(gps):
        r0 = gl * gc
        a_t = jnp.concatenate(
            [u_ref[r0:r0 + gc, j * nchunks:(j + 1) * nchunks] for j in range(CHUNK)], axis=0)
        r1 = jnp.dot(w1_ref[gl], a_t, preferred_element_type=_F32)
        bre_ref[gl * ns:(gl + 1) * ns, :] = r1[:ns]
        bim_ref[gl * ns:(gl + 1) * ns, :] = r1[ns:2 * ns]
        yint_ref[gl] = r1[2 * ns:]
    lane = lax.broadcasted_iota(jnp.int32, (rows, LANES), 1)
    lane0 = lane == 0
    sm_re, sm_im = sm_ref[:, 0].reshape(rows, LANES), sm_ref[:, 1].reshape(rows, LANES)
    in_re, in_im = sm_ref[:, 2].reshape(rows, LANES), sm_ref[:, 3].reshape(rows, LANES)
    s_re = [bre_ref[:, h * LANES:(h + 1) * LANES] for h in range(nblk)]
    s_im = [bim_ref[:, h * LANES:(h + 1) * LANES] for h in range(nblk)]
    s_re[0] = s_re[0] + jnp.where(lane0, in_re, 0.0)
    s_im[0] = s_im[0] + jnp.where(lane0, in_im, 0.0)

    def shifted(blocks, sh, first):
        if sh % LANES == 0:
            k = sh // LANES
            return [None if h < k else blocks[h - k] for h in range(nblk)]
        rot = [pltpu.roll(blk, sh, axis=1) for blk in blocks]
        keep = lane >= sh
        return [jnp.where(keep, rot[h], first if h == 0 else rot[h - 1]) for h in range(nblk)]

    for lvl in range(levels):
        sh = 1 << lvl
        p_re = pwr_ref[:, lvl].reshape(rows, LANES)
        p_im = pwi_ref[:, lvl].reshape(rows, LANES)
        t_re, t_im = shifted(s_re, sh, 0.0), shifted(s_im, sh, 0.0)
        for h in range(nblk):
            if t_re[h] is not None:
                s_re[h], s_im[h] = (s_re[h] + p_re * t_re[h] - p_im * t_im[h],
                                    s_im[h] + p_re * t_im[h] + p_im * t_re[h])
    prev_re, prev_im = shifted(s_re, 1, sm_re), shifted(s_im, 1, sm_im)
    for h in range(nblk):
        bre_ref[:, h * LANES:(h + 1) * LANES] = prev_re[h]
        bim_ref[:, h * LANES:(h + 1) * LANES] = prev_im[h]
    for gl in range(gps):
        r0 = gl * gc
        s_prev = jnp.concatenate([bre_ref[gl * ns:(gl + 1) * ns, :], bim_ref[gl * ns:(gl + 1) * ns, :]],
                                 axis=0).astype(_BF16)
        y = yint_ref[gl] + jnp.dot(g_ref[gl], s_prev, preferred_element_type=_F32)
        d_col = jnp.tile(d_ref[r0:r0 + gc, :], (CHUNK, 1))
        for t in range(CHUNK):
            u_t = u_ref[r0:r0 + gc, t * nchunks:(t + 1) * nchunks].astype(_F32)
            z = jax.nn.gelu(y[t * gc:(t + 1) * gc, :] + d_col[t * gc:(t + 1) * gc, :] * u_t)
            z_ref[r0:r0 + gc, t * nchunks:(t + 1) * nchunks] = z.astype(z_ref.dtype)


def _ssm_core(u_t, w1, g_mat, pwr, pwi, s_meta, d_col, levels):
    b, n, s = u_t.shape
    gps, ns = GROUPS_PER_STEP, SSM_STATE
    rows = gps * SSM_GROUP_CH
    nchunks = s // CHUNK
    wspec = lambda *tail: pl.BlockSpec((gps,) + tail, lambda gb, i: (gb,) + (0,) * len(tail))
    return pl.pallas_call(
        functools.partial(_ssm_kernel, nchunks=nchunks, levels=levels),
        grid=(n // rows, b),
        in_specs=[
            pl.BlockSpec((None, rows, s), lambda gb, i: (i, gb, 0)),
            wspec(2 * ns + TOEP, TOEP), wspec(TOEP, 2 * ns),
            wspec(levels, ns, LANES), wspec(levels, ns, LANES), wspec(4, ns, LANES),
            pl.BlockSpec((rows, 1), lambda gb, i: (gb, 0)),
        ],
        out_specs=pl.BlockSpec((None, rows, s), lambda gb, i: (i, gb, 0)),
        out_shape=jax.ShapeDtypeStruct((b, n, s), _BF16),
        scratch_shapes=[pltpu.VMEM((gps * ns, nchunks), _F32), pltpu.VMEM((gps * ns, nchunks), _F32),
                        pltpu.VMEM((gps, TOEP, nchunks), _F32)],
        compiler_params=pltpu.CompilerParams(
            dimension_semantics=("arbitrary", "arbitrary"), vmem_limit_bytes=VMEM_LIMIT),
        name="ssm_core",
    )(u_t, w1, g_mat, pwr, pwi, s_meta, d_col)


def _attn_kernel(q_ref, kc_ref, vc_ref, kp_ref, vp_ref, km_ref, vm_ref, sink_ref, o_ref, acc_ref):
    has_prev = pl.program_id(1) > 0
    wq = GQA_GROUP * WINDOW
    kj = lax.broadcasted_iota(jnp.int32, (WINDOW, wq), 0)
    qi = lax.broadcasted_iota(jnp.int32, (WINDOW, wq), 1) % WINDOW
    in_cur = kj <= qi
    for h in range(N_KV_HEADS):
        r0 = h * HEAD_DIM
        q4 = jnp.concatenate(
            [q_ref[(h * GQA_GROUP + r) * HEAD_DIM:(h * GQA_GROUP + r + 1) * HEAD_DIM, :]
             for r in range(GQA_GROUP)], axis=1)
        s_cur = lax.dot_general(kc_ref[r0:r0 + HEAD_DIM, :], q4, _TN, preferred_element_type=_F32)
        s_prev = lax.dot_general(kp_ref[r0:r0 + HEAD_DIM, :], q4, _TN, preferred_element_type=_F32)
        s_meta = lax.dot_general(km_ref[r0:r0 + HEAD_DIM, :], q4, _TN, preferred_element_type=_F32)
        s_sel = jnp.where(in_cur, s_cur, jnp.where(has_prev, s_prev, -jnp.inf))
        sink = sink_ref[h]
        m = jnp.maximum(jnp.maximum(jnp.max(s_sel, axis=0, keepdims=True),
                                    jnp.max(s_meta, axis=0, keepdims=True)), sink)
        e_sel = jnp.exp(s_sel - m)
        e_meta = jnp.exp(s_meta - m)
        den = (jnp.sum(e_sel, axis=0, keepdims=True) + jnp.sum(e_meta, axis=0, keepdims=True)
               + jnp.exp(sink - m))
        p_cur = jnp.where(in_cur, e_sel, 0.0).astype(_BF16)
        p_prev = jnp.where(in_cur, 0.0, e_sel).astype(_BF16)
        o = (jnp.dot(vc_ref[r0:r0 + HEAD_DIM, :], p_cur, preferred_element_type=_F32)
             + jnp.dot(vp_ref[r0:r0 + HEAD_DIM, :], p_prev, preferred_element_type=_F32)
             + jnp.dot(vm_ref[r0:r0 + HEAD_DIM, :], e_meta.astype(_BF16), preferred_element_type=_F32))
        o = o / den
        for r in range(GQA_GROUP):
            hq = h * GQA_GROUP + r
            acc_ref[hq * HEAD_DIM:(hq + 1) * HEAD_DIM, :] = o[:, r * WINDOW:(r + 1) * WINDOW]
    o_ref[...] = acc_ref[...].T.astype(o_ref.dtype)


def _attention(qkv_t, qkv_meta_t, sink_rows):
    b, _, s = qkv_t.shape
    n_blk = s // WINDOW
    kblk = Q_W // KV_W
    return pl.pallas_call(
        _attn_kernel,
        grid=(b, n_blk),
        in_specs=[
            pl.BlockSpec((None, Q_W, WINDOW), lambda i, n: (i, 0, n)),
            pl.BlockSpec((None, KV_W, WINDOW), lambda i, n: (i, kblk, n)),
            pl.BlockSpec((None, KV_W, WINDOW), lambda i, n: (i, kblk + 1, n)),
            pl.BlockSpec((None, KV_W, WINDOW), lambda i, n: (i, kblk, jnp.maximum(n - 1, 0))),
            pl.BlockSpec((None, KV_W, WINDOW), lambda i, n: (i, kblk + 1, jnp.maximum(n - 1, 0))),
            pl.BlockSpec((None, KV_W, N_META), lambda i, n: (0, kblk, 0)),
            pl.BlockSpec((None, KV_W, N_META), lambda i, n: (0, kblk + 1, 0)),
            pl.BlockSpec(sink_rows.shape, lambda i, n: (0, 0, 0)),
        ],
        out_specs=pl.BlockSpec((None, WINDOW, Q_W), lambda i, n: (i, n, 0)),
        out_shape=jax.ShapeDtypeStruct((b, s, Q_W), _F32),
        scratch_shapes=[pltpu.VMEM((Q_W, WINDOW), _F32)],
        compiler_params=pltpu.CompilerParams(
            dimension_semantics=("arbitrary", "arbitrary"), vmem_limit_bytes=VMEM_LIMIT),
        name="swa_attention",
    )(qkv_t, qkv_t, qkv_t, qkv_t, qkv_t, qkv_meta_t, qkv_meta_t, sink_rows)


def _tail_kernel(x_hbm, attn_hbm, z_ref, nm_ref, wg_ref, wglu_ref, abn_ref, sbn_ref, wout_ref,
                 nf_ref, wfi_ref, wfo_ref, o_hbm, xbuf, abuf, obuf, sem_x, sem_a, sem_o):
    d = D_MODEL
    nj = pl.num_programs(1)
    last = pl.num_programs(0) * nj - 1
    step = pl.program_id(0) * nj + pl.program_id(1)
    slot = step % 2

    def fetch(s, sl):
        bb, jj = s // nj, s % nj
        return (pltpu.make_async_copy(*_row_set_copy(x_hbm, xbuf, sem_x, bb, jj, sl)),
                pltpu.make_async_copy(*_row_set_copy(attn_hbm, abuf, sem_a, bb, jj, sl)))

    def put(s, sl):
        dst, src, sem = _row_set_copy(o_hbm, obuf, sem_o, s // nj, s % nj, sl)
        return pltpu.make_async_copy(src, dst, sem)

    @pl.when(step == 0)
    def _():
        for c in fetch(step, slot):
            c.start()

    @pl.when(step < last)
    def _():
        for c in fetch(step + 1, 1 - slot):
            c.start()

    for c in fetch(step, slot):
        c.wait()

    @pl.when(step >= 2)
    def _():
        put(step - 2, slot).wait()

    x = xbuf[slot]
    xn = _rms_rows(x, nm_ref[...]).astype(_BF16)
    gates = jnp.dot(xn, wg_ref[...], preferred_element_type=_F32)
    zz = lax.dot_general(z_ref[...], wglu_ref[...], _TN, preferred_element_type=_F32)
    ssm = zz[:, :d] * jax.nn.sigmoid(zz[:, d:])
    merged = (jax.nn.sigmoid(gates[:, :d]) * _rms_rows(abuf[slot], abn_ref[...])
              + jax.nn.sigmoid(gates[:, d:]) * _rms_rows(ssm, sbn_ref[...]))
    h = x + jnp.dot(merged.astype(_BF16), wout_ref[...], preferred_element_type=_F32)
    hn = _rms_rows(h, nf_ref[...]).astype(_BF16)
    out = h
    fw = D_FF // FFN_SPLIT
    for c in range(FFN_SPLIT):
        gate = jnp.dot(hn, wfi_ref[:, c * fw:(c + 1) * fw], preferred_element_type=_F32)
        up = jnp.dot(hn, wfi_ref[:, D_FF + c * fw:D_FF + (c + 1) * fw], preferred_element_type=_F32)
        act = (jax.nn.silu(gate) * up).astype(_BF16)
        out = out + jnp.dot(act, wfo_ref[c * fw:(c + 1) * fw, :], preferred_element_type=_F32)
    obuf[slot] = out
    put(step, slot).start()

    @pl.when(step == last)
    def _():
        put(step, slot).wait()

        @pl.when(step >= 1)
        def _():
            put(step - 1, 1 - slot).wait()


def _tail(x, attn, z_t, norm_mix, w_g, w_glu, abn, sbn, w_out, norm_ffn, w_fi, w_fo):
    b, s, d = x.shape
    nchunks = s // CHUNK
    x4 = x.reshape(b, nchunks, CHUNK, d)
    a4 = attn.reshape(b, nchunks, CHUNK, d)
    const = lambda a: pl.BlockSpec(a.shape, lambda i, j: (0,) * a.ndim, pipeline_mode=pl.Buffered(1))
    hbm = pl.BlockSpec(memory_space=pl.ANY)
    out = pl.pallas_call(
        _tail_kernel,
        grid=(b, CHUNK),
        in_specs=[
            hbm, hbm,
            pl.BlockSpec((None, d, nchunks), lambda i, j: (i, 0, j)),
            const(norm_mix), const(w_g), const(w_glu), const(abn), const(sbn), const(w_out),
            const(norm_ffn), const(w_fi), const(w_fo),
        ],
        out_specs=hbm,
        out_shape=jax.ShapeDtypeStruct(x4.shape, x.dtype),
        scratch_shapes=[pltpu.VMEM((2, nchunks, d), x.dtype), pltpu.VMEM((2, nchunks, d), attn.dtype),
                        pltpu.VMEM((2, nchunks, d), x.dtype),
                        pltpu.SemaphoreType.DMA((2,)), pltpu.SemaphoreType.DMA((2,)),
                        pltpu.SemaphoreType.DMA((2,))],
        compiler_params=pltpu.CompilerParams(
            dimension_semantics=("arbitrary", "arbitrary"), vmem_limit_bytes=VMEM_LIMIT),
        name="tail",
    )(x4, a4, z_t, norm_mix, w_g, w_glu, abn, sbn, w_out, norm_ffn, w_fi, w_fo)
    return out.reshape(b, s, d)


def kernel(x, meta_tokens, norm_mix, w_in, q_norm, k_norm, attn_sinks, lam_re, lam_im, log_dt,
           ssm_b_re, ssm_b_im, ssm_c_re, ssm_c_im, ssm_d, w_glu, attn_branch_norm, ssm_branch_norm,
           w_out, norm_ffn, w_ffn_in, w_ffn_out):
    seq = x.shape[1]
    levels = (seq // CHUNK).bit_length() - 1
    w = w_in[0]
    w_qkv_t = w[:, :QKV_W].T.astype(_BF16)
    w_u_t = w[:, QKV_W:QKV_W + D_MODEL].T.astype(_BF16)
    w_g = w[:, QKV_W + D_MODEL:].astype(_BF16)
    scale = HEAD_DIM ** -0.5
    qk_gain = jnp.concatenate([jnp.tile(q_norm[0] * scale, N_Q_HEADS),
                               jnp.tile(k_norm[0], N_KV_HEADS)])[:, None].astype(_F32)
    no_gain = jnp.zeros((8, 1), _F32)
    sink_rows = jnp.repeat(attn_sinks[0].reshape(N_KV_HEADS, 1, GQA_GROUP), WINDOW, axis=2).astype(_F32)
    dt = jnp.exp(log_dt[0].astype(_F32))
    lr, li = lam_re[0].astype(_F32), lam_im[0].astype(_F32)
    dtb = jnp.broadcast_to(dt[:, None], lr.shape)
    colp = jnp.stack([lr, li, dtb, jnp.zeros_like(lr)], axis=2)
    rowp = jnp.stack([lr, li, dtb, jnp.zeros_like(lr)], axis=1)
    d_col = ssm_d[0].astype(_F32)[:, None]

    meta = meta_tokens.astype(_F32)[None]
    qkv_meta_t = _proj_t(meta, w_qkv_t, norm_mix, qk_gain, norm_heads=N_Q_HEADS + N_KV_HEADS,
                         tm=N_META, name="proj_qkv_meta")
    u_meta_t = _proj_t(meta, w_u_t, norm_mix, no_gain, norm_heads=0, tm=N_META, name="proj_u_meta")
    a_meta = (u_meta_t[0].astype(_F32).reshape(SSM_GROUPS, SSM_GROUP_CH, CHUNK)
              .transpose(0, 2, 1).reshape(SSM_GROUPS, 1, TOEP))
    w1, g_mat, pwr, pwi, s_meta = _ssm_prep(
        colp, rowp, ssm_b_re[0].astype(_F32), ssm_b_im[0].astype(_F32),
        ssm_c_re[0].astype(_F32), ssm_c_im[0].astype(_F32), a_meta, levels)

    qkv_t = _proj_t(x, w_qkv_t, norm_mix, qk_gain, norm_heads=N_Q_HEADS + N_KV_HEADS,
                    tm=512, name="proj_qkv")
    u_t = _proj_u_perm(x, w_u_t, norm_mix)
    attn = _attention(qkv_t, qkv_meta_t, sink_rows)
    z_t = _ssm_core(u_t, w1, g_mat, pwr, pwi, s_meta, d_col, levels)
    return _tail(x, attn, z_t, norm_mix, w_g, w_glu[0].astype(_BF16), attn_branch_norm,
                 ssm_branch_norm, w_out[0].astype(_BF16), norm_ffn, w_ffn_in[0].astype(_BF16),
                 w_ffn_out[0].astype(_BF16))
```

```python
import functools

import jax
import jax.numpy as jnp
from jax import lax
from jax.experimental import pallas as pl
from jax.experimental.pallas import tpu as pltpu

D_MODEL = 1024
N_META = 16
HEAD_DIM = 64
N_Q_HEADS = 16
N_KV_HEADS = 4
GQA_GROUP = N_Q_HEADS // N_KV_HEADS
WINDOW = 128
SSM_GROUP_CH = 16
SSM_GROUPS = D_MODEL // SSM_GROUP_CH
SSM_STATE = 64
D_FF = 2816
Q_W = N_Q_HEADS * HEAD_DIM
KV_W = N_KV_HEADS * HEAD_DIM
QKV_W = Q_W + 2 * KV_W
EPS = 1e-6
LOG2_E = 1.4426950408889634

LANES = 128
CHUNK = 16
TOEP = CHUNK * SSM_GROUP_CH
GROUPS_PER_STEP = 8
FFN_SPLIT = 2
ATTN_BLOCKS = 4
VMEM_LIMIT = 56 * 1024 * 1024

_F32 = jnp.float32
_BF16 = jnp.bfloat16
_NT = (((1,), (1,)), ((), ()))
_TN = (((0,), (0,)), ((), ()))
_HI = lax.Precision.HIGHEST


def _rms_rows(x, gain):
    return x * lax.rsqrt(jnp.mean(x * x, axis=-1, keepdims=True) + EPS) * gain


def _proj_t_body(x, w_ref, nm_ref, gain_ref, o_ref, norm_heads):
    xn = _rms_rows(x, nm_ref[...]).astype(_BF16)
    p = lax.dot_general(w_ref[...], xn, _NT, preferred_element_type=_F32)
    if norm_heads:
        rows = norm_heads * HEAD_DIM
        tm = p.shape[1]
        hd = p[:rows].reshape(norm_heads, HEAD_DIM, tm)
        ms = jnp.mean(hd * hd, axis=1, keepdims=True)
        hd = hd * lax.rsqrt(ms + EPS) * gain_ref[...].reshape(norm_heads, HEAD_DIM, 1)
        o_ref[:rows, :] = hd.reshape(rows, tm).astype(o_ref.dtype)
        o_ref[rows:, :] = p[rows:].astype(o_ref.dtype)
    else:
        o_ref[...] = p.astype(o_ref.dtype)


def _proj_t_kernel(x_ref, w_ref, nm_ref, gain_ref, o_ref, *, norm_heads):
    _proj_t_body(x_ref[...], w_ref, nm_ref, gain_ref, o_ref, norm_heads)


def _proj_t(x, w_t, norm_mix, gain, *, norm_heads, tm, name):
    b, s, d = x.shape
    n = w_t.shape[0]
    return pl.pallas_call(
        functools.partial(_proj_t_kernel, norm_heads=norm_heads),
        grid=(b, s // tm),
        in_specs=[
            pl.BlockSpec((None, tm, d), lambda i, j: (i, j, 0)),
            pl.BlockSpec((n, d), lambda i, j: (0, 0)),
            pl.BlockSpec((1, d), lambda i, j: (0, 0)),
            pl.BlockSpec(gain.shape, lambda i, j: (0, 0)),
        ],
        out_specs=pl.BlockSpec((None, n, tm), lambda i, j: (i, 0, j)),
        out_shape=jax.ShapeDtypeStruct((b, n, s), _BF16),
        compiler_params=pltpu.CompilerParams(
            dimension_semantics=("arbitrary", "arbitrary"), vmem_limit_bytes=VMEM_LIMIT),
        name=name,
    )(x, w_t, norm_mix, gain)


def _row_set_copy(hbm4, buf, sem, bb, jj, slot):
    return hbm4.at[bb, :, jj, :], buf.at[slot], sem.at[slot]


def _proj_u_kernel(x_hbm, w_ref, nm_ref, o_ref, xbuf, sem):
    nj = pl.num_programs(1)
    last = pl.num_programs(0) * nj - 1
    step = pl.program_id(0) * nj + pl.program_id(1)
    slot = step % 2

    def fetch(s, sl):
        return pltpu.make_async_copy(*_row_set_copy(x_hbm, xbuf, sem, s // nj, s % nj, sl))

    @pl.when(step == 0)
    def _():
        fetch(step, slot).start()

    @pl.when(step < last)
    def _():
        fetch(step + 1, 1 - slot).start()

    fetch(step, slot).wait()
    _proj_t_body(xbuf[slot], w_ref, nm_ref, None, o_ref, 0)


def _proj_u_perm(x, w_t, norm_mix):
    b, s, d = x.shape
    n = w_t.shape[0]
    nchunks = s // CHUNK
    x4 = x.reshape(b, nchunks, CHUNK, d)
    return pl.pallas_call(
        _proj_u_kernel,
        grid=(b, CHUNK),
        in_specs=[
            pl.BlockSpec(memory_space=pl.ANY),
            pl.BlockSpec((n, d), lambda i, j: (0, 0)),
            pl.BlockSpec((1, d), lambda i, j: (0, 0)),
        ],
        out_specs=pl.BlockSpec((None, n, nchunks), lambda i, j: (i, 0, j)),
        out_shape=jax.ShapeDtypeStruct((b, n, s), _BF16),
        scratch_shapes=[pltpu.VMEM((2, nchunks, d), x.dtype), pltpu.SemaphoreType.DMA((2,))],
        compiler_params=pltpu.CompilerParams(
            dimension_semantics=("arbitrary", "arbitrary"), vmem_limit_bytes=VMEM_LIMIT),
        name="proj_u_perm",
    )(x4, w_t, norm_mix)


def _ssm_prep_kernel(colp_ref, rowp_ref, bre_ref, bim_ref, cre_ref, cim_ref, am_ref,
                     w1_ref, g_ref, pwr_ref, pwi_ref, sm_ref, *, levels):
    ns, gc = SSM_STATE, SSM_GROUP_CH
    l16 = lax.broadcasted_iota(jnp.int32, (gc, TOEP), 1)
    r16 = lax.broadcasted_iota(jnp.int32, (gc, TOEP), 0)
    tile_c = (l16 % gc == r16).astype(_F32)
    rep_j = (l16 // gc == r16).astype(_F32)
    rt = lax.broadcasted_iota(jnp.int32, (TOEP, CHUNK), 0)
    ct = lax.broadcasted_iota(jnp.int32, (TOEP, CHUNK), 1)
    rep_t = (rt // gc == ct).astype(_F32)
    expand = lambda a, m: jnp.dot(a, m, precision=_HI, preferred_element_type=_F32)
    for gl in range(GROUPS_PER_STEP):
        lr, li, dt = colp_ref[gl, :, 0:1], colp_ref[gl, :, 1:2], colp_ref[gl, :, 2:3]
        mag = jnp.exp(lr * dt)
        ar, ai = mag * jnp.cos(li * dt), mag * jnp.sin(li * dt)
        den = lr * lr + li * li
        nr, ni = ar - 1.0, ai
        fr, fi = (nr * lr + ni * li) / den, (ni * lr - nr * li) / den
        b_re, b_im = bre_ref[gl], bim_ref[gl]
        bbr_t = expand(fr * b_re - fi * b_im, tile_c)
        bbi_t = expand(fr * b_im + fi * b_re, tile_c)
        tau = (CHUNK - 1 - lax.broadcasted_iota(jnp.int32, (ns, CHUNK), 1)).astype(_F32)
        pm = jnp.exp(lr * dt * tau)
        pr = expand(pm * jnp.cos(li * dt * tau), rep_j)
        pi = expand(pm * jnp.sin(li * dt * tau), rep_j)
        xr = pr * bbr_t - pi * bbi_t
        xi = pr * bbi_t + pi * bbr_t
        krev = (jnp.dot(cre_ref[gl], xr, precision=_HI, preferred_element_type=_F32)
                - jnp.dot(cim_ref[gl], xi, precision=_HI, preferred_element_type=_F32))
        rz = jnp.concatenate([krev, jnp.zeros_like(krev)], axis=1)
        toep = jnp.concatenate(
            [rz[:, (CHUNK - 1 - t) * gc:(CHUNK - 1 - t) * gc + TOEP] for t in range(CHUNK)], axis=0)
        w1_ref[gl] = jnp.concatenate([xr, xi, toep], axis=0).astype(w1_ref.dtype)
        lr_r, li_r, dt_r = rowp_ref[gl, 0:1, :], rowp_ref[gl, 1:2, :], rowp_ref[gl, 2:3, :]
        tp1 = (lax.broadcasted_iota(jnp.int32, (CHUNK, ns), 0) + 1).astype(_F32)
        pmg = jnp.exp(lr_r * dt_r * tp1)
        gar = expand(rep_t, pmg * jnp.cos(li_r * dt_r * tp1))
        gai = expand(rep_t, pmg * jnp.sin(li_r * dt_r * tp1))
        crt = jnp.tile(cre_ref[gl], (CHUNK, 1))
        cit = jnp.tile(cim_ref[gl], (CHUNK, 1))
        g_ref[gl] = jnp.concatenate([crt * gar - cit * gai, -(crt * gai + cit * gar)],
                                    axis=1).astype(g_ref.dtype)
        lvl = lax.broadcasted_iota(jnp.int32, (ns, levels), 1)
        e = (CHUNK * jnp.left_shift(1, lvl)).astype(_F32)
        pme = jnp.exp(lr * dt * e)
        per, pei = pme * jnp.cos(li * dt * e), pme * jnp.sin(li * dt * e)
        for l in range(levels):
            pwr_ref[gl, l] = jnp.broadcast_to(per[:, l:l + 1], (ns, LANES))
            pwi_ref[gl, l] = jnp.broadcast_to(pei[:, l:l + 1], (ns, LANES))
        am = am_ref[gl]
        sr = jnp.sum(xr * am, axis=1, keepdims=True)
        si = jnp.sum(xi * am, axis=1, keepdims=True)
        a16r, a16i = per[:, 0:1], pei[:, 0:1]
        sm_ref[gl, 0] = jnp.broadcast_to(sr, (ns, LANES))
        sm_ref[gl, 1] = jnp.broadcast_to(si, (ns, LANES))
        sm_ref[gl, 2] = jnp.broadcast_to(a16r * sr - a16i * si, (ns, LANES))
        sm_ref[gl, 3] = jnp.broadcast_to(a16r * si + a16i * sr, (ns, LANES))


def _ssm_prep(colp, rowp, b_re, b_im, c_re, c_im, a_meta, levels):
    g, gps = SSM_GROUPS, GROUPS_PER_STEP
    ns, gc = SSM_STATE, SSM_GROUP_CH
    spec = lambda *tail: pl.BlockSpec((gps,) + tail, lambda i: (i,) + (0,) * len(tail))
    return pl.pallas_call(
        functools.partial(_ssm_prep_kernel, levels=levels),
        grid=(g // gps,),
        in_specs=[spec(ns, 4), spec(4, ns), spec(ns, gc), spec(ns, gc), spec(gc, ns), spec(gc, ns),
                  spec(1, TOEP)],
        out_specs=[spec(2 * ns + TOEP, TOEP), spec(TOEP, 2 * ns), spec(levels, ns, LANES),
                   spec(levels, ns, LANES), spec(4, ns, LANES)],
        out_shape=[jax.ShapeDtypeStruct((g, 2 * ns + TOEP, TOEP), _BF16),
                   jax.ShapeDtypeStruct((g, TOEP, 2 * ns), _BF16),
                   jax.ShapeDtypeStruct((g, levels, ns, LANES), _F32),
                   jax.ShapeDtypeStruct((g, levels, ns, LANES), _F32),
                   jax.ShapeDtypeStruct((g, 4, ns, LANES), _F32)],
        compiler_params=pltpu.CompilerParams(dimension_semantics=("arbitrary",),
                                             vmem_limit_bytes=VMEM_LIMIT),
        name="ssm_prep",
    )(colp, rowp, b_re, b_im, c_re, c_im, a_meta)


def _ssm_kernel(u_ref, w1_ref, g_ref, pwr_ref, pwi_ref, sm_ref, d_ref, z_ref,
                bre_ref, bim_ref, yint_ref, *, nchunks, levels):
    ns, gc, gps = SSM_STATE, SSM_GROUP_CH, GROUPS_PER_STEP
    nblk = nchunks // LANES
    rows = gps * ns
    for gl in range(gps):
        r0 = gl * gc
        a_t = jnp.concatenate(
            [u_ref[r0:r0 + gc, j * nchunks:(j + 1) * nchunks] for j in range(CHUNK)], axis=0)
        r1 = jnp.dot(w1_ref[gl], a_t, preferred_element_type=_F32)
        bre_ref[gl * ns:(gl + 1) * ns, :] = r1[:ns]
        bim_ref[gl * ns:(gl + 1) * ns, :] = r1[ns:2 * ns]
        yint_ref[gl] = r1[2 * ns:]
    lane = lax.broadcasted_iota(jnp.int32, (rows, LANES), 1)
    lane0 = lane == 0
    sm_re, sm_im = sm_ref[:, 0].reshape(rows, LANES), sm_ref[:, 1].reshape(rows, LANES)
    in_re, in_im = sm_ref[:, 2].reshape(rows, LANES), sm_ref[:, 3].reshape(rows, LANES)
    s_re = [bre_ref[:, h * LANES:(h + 1) * LANES] for h in range(nblk)]
    s_im = [bim_ref[:, h * LANES:(h + 1) * LANES] for h in range(nblk)]
    s_re[0] = s_re[0] + jnp.where(lane0, in_re, 0.0)
    s_im[0] = s_im[0] + jnp.where(lane0, in_im, 0.0)

    def shifted(blocks, sh, first):
        if sh % LANES == 0:
            k = sh // LANES
            return [None if h < k else blocks[h - k] for h in range(nblk)]
        rot = [pltpu.roll(blk, sh, axis=1) for blk in blocks]
        keep = lane >= sh
        return [jnp.where(keep, rot[h], first if h == 0 else rot[h - 1]) for h in range(nblk)]

    for lvl in range(levels):
        sh = 1 << lvl
        p_re = pwr_ref[:, lvl].reshape(rows, LANES)
        p_im = pwi_ref[:, lvl].reshape(rows, LANES)
        t_re, t_im = shifted(s_re, sh, 0.0), shifted(s_im, sh, 0.0)
        for h in range(nblk):
            if t_re[h] is not None:
                s_re[h], s_im[h] = (s_re[h] + p_re * t_re[h] - p_im * t_im[h],
                                    s_im[h] + p_re * t_im[h] + p_im * t_re[h])
    prev_re, prev_im = shifted(s_re, 1, sm_re), shifted(s_im, 1, sm_im)
    for h in range(nblk):
        bre_ref[:, h * LANES:(h + 1) * LANES] = prev_re[h]
        bim_ref[:, h * LANES:(h + 1) * LANES] = prev_im[h]
    for gl in range(gps):
        r0 = gl * gc
        s_prev = jnp.concatenate([bre_ref[gl * ns:(gl + 1) * ns, :], bim_ref[gl * ns:(gl + 1) * ns, :]],
                                 axis=0).astype(_BF16)
        y = yint_ref[gl] + jnp.dot(g_ref[gl], s_prev, preferred_element_type=_F32)
        d_col = jnp.tile(d_ref[r0:r0 + gc, :], (CHUNK, 1))
        for t in range(CHUNK):
            u_t = u_ref[r0:r0 + gc, t * nchunks:(t + 1) * nchunks].astype(_F32)
            z = jax.nn.gelu(y[t * gc:(t + 1) * gc, :] + d_col[t * gc:(t + 1) * gc, :] * u_t)
            z_ref[r0:r0 + gc, t * nchunks:(t + 1) * nchunks] = z.astype(z_ref.dtype)


def _ssm_core(u_t, w1, g_mat, pwr, pwi, s_meta, d_col, levels):
    b, n, s = u_t.shape
    gps, ns = GROUPS_PER_STEP, SSM_STATE
    rows = gps * SSM_GROUP_CH
    nchunks = s // CHUNK
    wspec = lambda *tail: pl.BlockSpec((gps,) + tail, lambda gb, i: (gb,) + (0,) * len(tail))
    return pl.pallas_call(
        functools.partial(_ssm_kernel, nchunks=nchunks, levels=levels),
        grid=(n // rows, b),
        in_specs=[
            pl.BlockSpec((None, rows, s), lambda gb, i: (i, gb, 0)),
            wspec(2 * ns + TOEP, TOEP), wspec(TOEP, 2 * ns),
            wspec(levels, ns, LANES), wspec(levels, ns, LANES), wspec(4, ns, LANES),
            pl.BlockSpec((rows, 1), lambda gb, i: (gb, 0)),
        ],
        out_specs=pl.BlockSpec((None, rows, s), lambda gb, i: (i, gb, 0)),
        out_shape=jax.ShapeDtypeStruct((b, n, s), _BF16),
        scratch_shapes=[pltpu.VMEM((gps * ns, nchunks), _F32), pltpu.VMEM((gps * ns, nchunks), _F32),
                        pltpu.VMEM((gps, TOEP, nchunks), _F32)],
        compiler_params=pltpu.CompilerParams(
            dimension_semantics=("arbitrary", "arbitrary"), vmem_limit_bytes=VMEM_LIMIT),
        name="ssm_core",
    )(u_t, w1, g_mat, pwr, pwi, s_meta, d_col)


def _attn_kernel(q_ref, k_ref, v_ref, kp_ref, vp_ref, km_ref, vm_ref, sink_ref, tri_ref, o_ref, acc_ref):
    first_step = pl.program_id(1) == 0
    wq = GQA_GROUP * WINDOW
    kj = lax.broadcasted_iota(jnp.int32, (WINDOW, wq), 0)
    qi = lax.broadcasted_iota(jnp.int32, (WINDOW, wq), 1) % WINDOW
    in_cur = kj <= qi
    units = [(blk, h) for blk in range(ATTN_BLOCKS) for h in range(N_KV_HEADS)]

    def kv_rows(ref, prev_ref, blk, h):
        rows = slice(h * HEAD_DIM, (h + 1) * HEAD_DIM)
        cur = ref[rows, blk * WINDOW:(blk + 1) * WINDOW]
        prev = prev_ref[rows, :] if blk == 0 else ref[rows, (blk - 1) * WINDOW:blk * WINDOW]
        return cur, prev

    scores = []
    for blk, h in units:
        q4 = jnp.concatenate(
            [q_ref[(h * GQA_GROUP + r) * HEAD_DIM:(h * GQA_GROUP + r + 1) * HEAD_DIM,
                   blk * WINDOW:(blk + 1) * WINDOW] for r in range(GQA_GROUP)], axis=1)
        k_cur, k_prev = kv_rows(k_ref, kp_ref, blk, h)
        k_meta = km_ref[h * HEAD_DIM:(h + 1) * HEAD_DIM, :]
        scores.append(tuple(lax.dot_general(k, q4, _TN, preferred_element_type=_F32)
                            for k in (k_cur, k_prev, k_meta)))
    for (blk, h), (s_cur, s_prev, s_meta) in zip(units, scores):
        if blk == 0:
            s_prev = jnp.where(first_step, -jnp.inf, s_prev)
        s_sel = jnp.where(in_cur, s_cur, s_prev)
        sink = sink_ref[h]
        m = jnp.maximum(jnp.maximum(jnp.max(s_sel, axis=0, keepdims=True),
                                    jnp.max(s_meta, axis=0, keepdims=True)), sink)
        e_sel = jnp.exp2(s_sel - m)
        e_meta = jnp.exp2(s_meta - m)
        den = (jnp.sum(e_sel, axis=0, keepdims=True) + jnp.sum(e_meta, axis=0, keepdims=True)
               + jnp.exp2(sink - m))
        e_bf = e_sel.astype(_BF16)
        p_cur = e_bf * tri_ref[...]
        p_all = jnp.concatenate([p_cur, e_bf - p_cur, e_meta.astype(_BF16)], axis=0)
        v_cur, v_prev = kv_rows(v_ref, vp_ref, blk, h)
        v_all = jnp.concatenate([v_cur, v_prev, vm_ref[h * HEAD_DIM:(h + 1) * HEAD_DIM, :]], axis=1)
        o = jnp.dot(v_all, p_all, preferred_element_type=_F32) * (1.0 / den)
        for r in range(GQA_GROUP):
            hq = h * GQA_GROUP + r
            acc_ref[hq * HEAD_DIM:(hq + 1) * HEAD_DIM, blk * WINDOW:(blk + 1) * WINDOW] = (
                o[:, r * WINDOW:(r + 1) * WINDOW])
    o_ref[...] = acc_ref[...].T.astype(o_ref.dtype)


def _attention(qkv_t, qkv_meta_t, sink_rows):
    b, _, s = qkv_t.shape
    kj = lax.broadcasted_iota(jnp.int32, (WINDOW, GQA_GROUP * WINDOW), 0)
    qi = lax.broadcasted_iota(jnp.int32, (WINDOW, GQA_GROUP * WINDOW), 1) % WINDOW
    tri = (kj <= qi).astype(_BF16)
    tq = ATTN_BLOCKS * WINDOW
    kblk = Q_W // KV_W
    prev_blk = lambda n: jnp.maximum(n * ATTN_BLOCKS - 1, 0)
    return pl.pallas_call(
        _attn_kernel,
        grid=(b, s // tq),
        in_specs=[
            pl.BlockSpec((None, Q_W, tq), lambda i, n: (i, 0, n)),
            pl.BlockSpec((None, KV_W, tq), lambda i, n: (i, kblk, n)),
            pl.BlockSpec((None, KV_W, tq), lambda i, n: (i, kblk + 1, n)),
            pl.BlockSpec((None, KV_W, WINDOW), lambda i, n: (i, kblk, prev_blk(n))),
            pl.BlockSpec((None, KV_W, WINDOW), lambda i, n: (i, kblk + 1, prev_blk(n))),
            pl.BlockSpec((None, KV_W, N_META), lambda i, n: (0, kblk, 0)),
            pl.BlockSpec((None, KV_W, N_META), lambda i, n: (0, kblk + 1, 0)),
            pl.BlockSpec(sink_rows.shape, lambda i, n: (0, 0, 0)),
            pl.BlockSpec(tri.shape, lambda i, n: (0, 0)),
        ],
        out_specs=pl.BlockSpec((None, tq, Q_W), lambda i, n: (i, n, 0)),
        out_shape=jax.ShapeDtypeStruct((b, s, Q_W), _F32),
        scratch_shapes=[pltpu.VMEM((Q_W, tq), _F32)],
        compiler_params=pltpu.CompilerParams(
            dimension_semantics=("arbitrary", "arbitrary"), vmem_limit_bytes=VMEM_LIMIT),
        name="swa_attention",
    )(qkv_t, qkv_t, qkv_t, qkv_t, qkv_t, qkv_meta_t, qkv_meta_t, sink_rows, tri)


def _tail_kernel(x_hbm, attn_hbm, z_ref, nm_ref, wg_ref, wglu_ref, abn_ref, sbn_ref, wout_ref,
                 nf_ref, wfi_ref, wfo_ref, o_hbm, xbuf, abuf, obuf, sem_x, sem_a, sem_o):
    d = D_MODEL
    nj = pl.num_programs(1)
    last = pl.num_programs(0) * nj - 1
    step = pl.program_id(0) * nj + pl.program_id(1)
    slot = step % 2

    def fetch(s, sl):
        bb, jj = s // nj, s % nj
        return (pltpu.make_async_copy(*_row_set_copy(x_hbm, xbuf, sem_x, bb, jj, sl)),
                pltpu.make_async_copy(*_row_set_copy(attn_hbm, abuf, sem_a, bb, jj, sl)))

    def put(s, sl):
        dst, src, sem = _row_set_copy(o_hbm, obuf, sem_o, s // nj, s % nj, sl)
        return pltpu.make_async_copy(src, dst, sem)

    @pl.when(step == 0)
    def _():
        for c in fetch(step, slot):
            c.start()

    @pl.when(step < last)
    def _():
        for c in fetch(step + 1, 1 - slot):
            c.start()

    for c in fetch(step, slot):
        c.wait()

    @pl.when(step >= 2)
    def _():
        put(step - 2, slot).wait()

    x = xbuf[slot]
    xn = _rms_rows(x, nm_ref[...]).astype(_BF16)
    gates = jnp.dot(xn, wg_ref[...], preferred_element_type=_F32)
    zz = lax.dot_general(z_ref[...], wglu_ref[...], _TN, preferred_element_type=_F32)
    ssm = zz[:, :d] * jax.nn.sigmoid(zz[:, d:])
    merged = (jax.nn.sigmoid(gates[:, :d]) * _rms_rows(abuf[slot], abn_ref[...])
              + jax.nn.sigmoid(gates[:, d:]) * _rms_rows(ssm, sbn_ref[...]))
    h = x + jnp.dot(merged.astype(_BF16), wout_ref[...], preferred_element_type=_F32)
    hn = _rms_rows(h, nf_ref[...]).astype(_BF16)
    out = h
    fw = D_FF // FFN_SPLIT
    for c in range(FFN_SPLIT):
        gate = jnp.dot(hn, wfi_ref[:, c * fw:(c + 1) * fw], preferred_element_type=_F32)
        up = jnp.dot(hn, wfi_ref[:, D_FF + c * fw:D_FF + (c + 1) * fw], preferred_element_type=_F32)
        act = (jax.nn.silu(gate) * up).astype(_BF16)
        out = out + jnp.dot(act, wfo_ref[c * fw:(c + 1) * fw, :], preferred_element_type=_F32)
    obuf[slot] = out
    put(step, slot).start()

    @pl.when(step == last)
    def _():
        put(step, slot).wait()

        @pl.when(step >= 1)
        def _():
            put(step - 1, 1 - slot).wait()


def _tail(x, attn, z_t, norm_mix, w_g, w_glu, abn, sbn, w_out, norm_ffn, w_fi, w_fo):
    b, s, d = x.shape
    nchunks = s // CHUNK
    x4 = x.reshape(b, nchunks, CHUNK, d)
    a4 = attn.reshape(b, nchunks, CHUNK, d)
    const = lambda a: pl.BlockSpec(a.shape, lambda i, j: (0,) * a.ndim, pipeline_mode=pl.Buffered(1))
    hbm = pl.BlockSpec(memory_space=pl.ANY)
    out = pl.pallas_call(
        _tail_kernel,
        grid=(b, CHUNK),
        in_specs=[
            hbm, hbm,
            pl.BlockSpec((None, d, nchunks), lambda i, j: (i, 0, j)),
            const(norm_mix), const(w_g), const(w_glu), const(abn), const(sbn), const(w_out),
            const(norm_ffn), const(w_fi), const(w_fo),
        ],
        out_specs=hbm,
        out_shape=jax.ShapeDtypeStruct(x4.shape, x.dtype),
        scratch_shapes=[pltpu.VMEM((2, nchunks, d), x.dtype), pltpu.VMEM((2, nchunks, d), attn.dtype),
                        pltpu.VMEM((2, nchunks, d), x.dtype),
                        pltpu.SemaphoreType.DMA((2,)), pltpu.SemaphoreType.DMA((2,)),
                        pltpu.SemaphoreType.DMA((2,))],
        compiler_params=pltpu.CompilerParams(
            dimension_semantics=("arbitrary", "arbitrary"), vmem_limit_bytes=VMEM_LIMIT),
        name="tail",
    )(x4, a4, z_t, norm_mix, w_g, w_glu, abn, sbn, w_out, norm_ffn, w_fi, w_fo)
    return out.reshape(b, s, d)


def kernel(x, meta_tokens, norm_mix, w_in, q_norm, k_norm, attn_sinks, lam_re, lam_im, log_dt,
           ssm_b_re, ssm_b_im, ssm_c_re, ssm_c_im, ssm_d, w_glu, attn_branch_norm, ssm_branch_norm,
           w_out, norm_ffn, w_ffn_in, w_ffn_out):
    seq = x.shape[1]
    levels = (seq // CHUNK).bit_length() - 1
    w = w_in[0]
    w_qkv_t = w[:, :QKV_W].T.astype(_BF16)
    w_u_t = w[:, QKV_W:QKV_W + D_MODEL].T.astype(_BF16)
    w_g = w[:, QKV_W + D_MODEL:].astype(_BF16)
    scale = HEAD_DIM ** -0.5 * LOG2_E
    qk_gain = jnp.concatenate([jnp.tile(q_norm[0] * scale, N_Q_HEADS),
                               jnp.tile(k_norm[0], N_KV_HEADS)])[:, None].astype(_F32)
    no_gain = jnp.zeros((8, 1), _F32)
    sink_rows = jnp.repeat((attn_sinks[0] * LOG2_E).reshape(N_KV_HEADS, 1, GQA_GROUP), WINDOW,
                           axis=2).astype(_F32)
    dt = jnp.exp(log_dt[0].astype(_F32))
    lr, li = lam_re[0].astype(_F32), lam_im[0].astype(_F32)
    dtb = jnp.broadcast_to(dt[:, None], lr.shape)
    colp = jnp.stack([lr, li, dtb, jnp.zeros_like(lr)], axis=2)
    rowp = jnp.stack([lr, li, dtb, jnp.zeros_like(lr)], axis=1)
    d_col = ssm_d[0].astype(_F32)[:, None]

    meta = meta_tokens.astype(_F32)[None]
    qkv_meta_t = _proj_t(meta, w_qkv_t, norm_mix, qk_gain, norm_heads=N_Q_HEADS + N_KV_HEADS,
                         tm=N_META, name="proj_qkv_meta")
    u_meta_t = _proj_t(meta, w_u_t, norm_mix, no_gain, norm_heads=0, tm=N_META, name="proj_u_meta")
    a_meta = (u_meta_t[0].astype(_F32).reshape(SSM_GROUPS, SSM_GROUP_CH, CHUNK)
              .transpose(0, 2, 1).reshape(SSM_GROUPS, 1, TOEP))
    w1, g_mat, pwr, pwi, s_meta = _ssm_prep(
        colp, rowp, ssm_b_re[0].astype(_F32), ssm_b_im[0].astype(_F32),
        ssm_c_re[0].astype(_F32), ssm_c_im[0].astype(_F32), a_meta, levels)

    qkv_t = _proj_t(x, w_qkv_t, norm_mix, qk_gain, norm_heads=N_Q_HEADS + N_KV_HEADS,
                    tm=512, name="proj_qkv")
    u_t = _proj_u_perm(x, w_u_t, norm_mix)
    attn = _attention(qkv_t, qkv_meta_t, sink_rows)
    z_t = _ssm_core(u_t, w1, g_mat, pwr, pwi, s_meta, d_col, levels)
    return _tail(x, attn, z_t, norm_mix, w_g, w_glu[0].astype(_BF16), attn_branch_norm,
                 ssm_branch_norm, w_out[0].astype(_BF16), norm_ffn, w_ffn_in[0].astype(_BF16),
                 w_ffn_out[0].astype(_BF16))
```

```python
import functools

import jax
import jax.numpy as jnp
from jax import lax
from jax.experimental import pallas as pl
from jax.experimental.pallas import tpu as pltpu

D_MODEL = 1024
N_META = 16
HEAD_DIM = 64
N_Q_HEADS = 16
N_KV_HEADS = 4
GQA_GROUP = N_Q_HEADS // N_KV_HEADS
WINDOW = 128
SSM_GROUP_CH = 16
SSM_GROUPS = D_MODEL // SSM_GROUP_CH
SSM_STATE = 64
D_FF = 2816
Q_W = N_Q_HEADS * HEAD_DIM
KV_W = N_KV_HEADS * HEAD_DIM
QKV_W = Q_W + 2 * KV_W
EPS = 1e-6
LOG2_E = 1.4426950408889634

LANES = 128
CHUNK = 16
TOEP = CHUNK * SSM_GROUP_CH
GROUPS_PER_STEP = 8
FFN_SPLIT = 2
ATTN_BLOCKS = 4
PROJ_PIECE = 256
TAIL_PIECE = 256
PROJ_U_SETS = 2
VMEM_LIMIT = 56 * 1024 * 1024

_F32 = jnp.float32
_BF16 = jnp.bfloat16
_NT = (((1,), (1,)), ((), ()))
_TN = (((0,), (0,)), ((), ()))
_HI = lax.Precision.HIGHEST


def _rms_rows(x, gain):
    return x * lax.rsqrt(jnp.mean(x * x, axis=-1, keepdims=True) + EPS) * gain


def _proj_t_body(x_ref, w_ref, nm_ref, gain_ref, o_ref, norm_heads):
    tm_all = x_ref.shape[0]
    tm = min(tm_all, PROJ_PIECE)
    for c in range(tm_all // tm):
        cols = slice(c * tm, (c + 1) * tm)
        xn = _rms_rows(x_ref[cols, :], nm_ref[...]).astype(_BF16)
        p = lax.dot_general(w_ref[...], xn, _NT, preferred_element_type=_F32)
        if norm_heads:
            rows = norm_heads * HEAD_DIM
            hd = p[:rows].reshape(norm_heads, HEAD_DIM, tm)
            ms = jnp.mean(hd * hd, axis=1, keepdims=True)
            hd = hd * lax.rsqrt(ms + EPS) * gain_ref[...].reshape(norm_heads, HEAD_DIM, 1)
            o_ref[:rows, cols] = hd.reshape(rows, tm).astype(o_ref.dtype)
            o_ref[rows:, cols] = p[rows:].astype(o_ref.dtype)
        else:
            o_ref[:, cols] = p.astype(o_ref.dtype)


def _proj_t_kernel(x_ref, w_ref, nm_ref, gain_ref, o_ref, *, norm_heads):
    _proj_t_body(x_ref, w_ref, nm_ref, gain_ref, o_ref, norm_heads)


def _proj_t(x, w_t, norm_mix, gain, *, norm_heads, tm, name):
    b, s, d = x.shape
    n = w_t.shape[0]
    return pl.pallas_call(
        functools.partial(_proj_t_kernel, norm_heads=norm_heads),
        grid=(b, s // tm),
        in_specs=[
            pl.BlockSpec((None, tm, d), lambda i, j: (i, j, 0)),
            pl.BlockSpec((n, d), lambda i, j: (0, 0)),
            pl.BlockSpec((1, d), lambda i, j: (0, 0)),
            pl.BlockSpec(gain.shape, lambda i, j: (0, 0)),
        ],
        out_specs=pl.BlockSpec((None, n, tm), lambda i, j: (i, 0, j)),
        out_shape=jax.ShapeDtypeStruct((b, n, s), _BF16),
        compiler_params=pltpu.CompilerParams(
            dimension_semantics=("arbitrary", "arbitrary"), vmem_limit_bytes=VMEM_LIMIT),
        name=name,
    )(x, w_t, norm_mix, gain)


def _row_set_copy(hbm4, buf, sem, bb, jj, slot):
    return hbm4.at[bb, :, jj, :], buf.at[slot], sem.at[slot]


def _proj_u_kernel(x_hbm, w_ref, nm_ref, o_ref, xbuf, sem):
    nj = pl.num_programs(1)
    last = pl.num_programs(0) * nj - 1
    step = pl.program_id(0) * nj + pl.program_id(1)
    slot = step % 2
    nchunks = x_hbm.shape[1]

    def fetch(s, sl):
        return [pltpu.make_async_copy(x_hbm.at[s // nj, :, (s % nj) * PROJ_U_SETS + k, :],
                                      xbuf.at[sl, k * nchunks:(k + 1) * nchunks, :], sem.at[sl, k])
                for k in range(PROJ_U_SETS)]

    @pl.when(step == 0)
    def _():
        for c in fetch(step, slot):
            c.start()

    @pl.when(step < last)
    def _():
        for c in fetch(step + 1, 1 - slot):
            c.start()

    for c in fetch(step, slot):
        c.wait()
    _proj_t_body(xbuf.at[slot], w_ref, nm_ref, None, o_ref, 0)


def _proj_u_perm(x, w_t, norm_mix):
    b, s, d = x.shape
    n = w_t.shape[0]
    nchunks = s // CHUNK
    x4 = x.reshape(b, nchunks, CHUNK, d)
    return pl.pallas_call(
        _proj_u_kernel,
        grid=(b, CHUNK // PROJ_U_SETS),
        in_specs=[
            pl.BlockSpec(memory_space=pl.ANY),
            pl.BlockSpec((n, d), lambda i, j: (0, 0)),
            pl.BlockSpec((1, d), lambda i, j: (0, 0)),
        ],
        out_specs=pl.BlockSpec((None, n, PROJ_U_SETS * nchunks), lambda i, j: (i, 0, j)),
        out_shape=jax.ShapeDtypeStruct((b, n, s), _BF16),
        scratch_shapes=[pltpu.VMEM((2, PROJ_U_SETS * nchunks, d), x.dtype),
                        pltpu.SemaphoreType.DMA((2, PROJ_U_SETS))],
        compiler_params=pltpu.CompilerParams(
            dimension_semantics=("arbitrary", "arbitrary"), vmem_limit_bytes=VMEM_LIMIT),
        name="proj_u_perm",
    )(x4, w_t, norm_mix)


def _ssm_prep_kernel(colp_ref, bre_ref, bim_ref, cre_ref, cim_ref, am_ref,
                     w1_ref, g_ref, pwr_ref, pwi_ref, sm_ref, *, levels):
    ns, gc = SSM_STATE, SSM_GROUP_CH
    expand = lambda a, m: jnp.dot(a, m, precision=_HI, preferred_element_type=_F32)
    l16 = lax.broadcasted_iota(jnp.int32, (gc, TOEP), 1)
    r16 = lax.broadcasted_iota(jnp.int32, (gc, TOEP), 0)
    tile_c = (l16 % gc == r16).astype(_F32)
    tau_r = lax.broadcasted_iota(jnp.int32, (LANES, TOEP), 0)
    jc_l = lax.broadcasted_iota(jnp.int32, (LANES, TOEP), 1)
    pick_rev = (tau_r == CHUNK - 1 - jc_l // gc).astype(_F32)
    tc_r = lax.broadcasted_iota(jnp.int32, (TOEP, LANES), 0)
    tau_l = lax.broadcasted_iota(jnp.int32, (TOEP, LANES), 1)
    pick_next = (tau_l == tc_r // gc + 1).astype(_F32)
    lane = lax.broadcasted_iota(jnp.int32, (ns, LANES), 1)
    lvl = jnp.clip(lane - CHUNK, 0, levels - 1)
    expo = jnp.where(lane <= CHUNK, lane,
                     jnp.where(lane < CHUNK + levels, CHUNK * jnp.left_shift(1, lvl), 0)).astype(_F32)
    is_tau = lane <= CHUNK
    for gl in range(GROUPS_PER_STEP):
        lr, li, dt = colp_ref[gl, :, 0:1], colp_ref[gl, :, 1:2], colp_ref[gl, :, 2:3]
        mag = jnp.exp(lr * dt * expo)
        ang = li * dt * expo
        pw_r, pw_i = mag * jnp.cos(ang), mag * jnp.sin(ang)
        ar, ai = pw_r[:, 1:2], pw_i[:, 1:2]
        den = lr * lr + li * li
        nr, ni = ar - 1.0, ai
        fr, fi = (nr * lr + ni * li) / den, (ni * lr - nr * li) / den
        b_re, b_im = bre_ref[gl], bim_ref[gl]
        bbr_t = expand(fr * b_re - fi * b_im, tile_c)
        bbi_t = expand(fr * b_im + fi * b_re, tile_c)
        tau_re, tau_im = jnp.where(is_tau, pw_r, 0.0), jnp.where(is_tau, pw_i, 0.0)
        pr, pi = expand(tau_re, pick_rev), expand(tau_im, pick_rev)
        xr = pr * bbr_t - pi * bbi_t
        xi = pr * bbi_t + pi * bbr_t
        krev = (jnp.dot(cre_ref[gl], xr, precision=_HI, preferred_element_type=_F32)
                - jnp.dot(cim_ref[gl], xi, precision=_HI, preferred_element_type=_F32))
        rz = jnp.concatenate([krev, jnp.zeros_like(krev)], axis=1)
        toep = jnp.concatenate(
            [rz[:, (CHUNK - 1 - t) * gc:(CHUNK - 1 - t) * gc + TOEP] for t in range(CHUNK)], axis=0)
        w1_ref[gl] = jnp.concatenate([xr, xi, toep], axis=0).astype(w1_ref.dtype)
        gar = expand(pick_next, tau_re.T)
        gai = expand(pick_next, tau_im.T)
        crt = jnp.tile(cre_ref[gl], (CHUNK, 1))
        cit = jnp.tile(cim_ref[gl], (CHUNK, 1))
        g_ref[gl] = jnp.concatenate([crt * gar - cit * gai, -(crt * gai + cit * gar)],
                                    axis=1).astype(g_ref.dtype)
        for l in range(levels):
            pwr_ref[gl, l] = jnp.broadcast_to(pw_r[:, CHUNK + l:CHUNK + l + 1], (ns, LANES))
            pwi_ref[gl, l] = jnp.broadcast_to(pw_i[:, CHUNK + l:CHUNK + l + 1], (ns, LANES))
        am = am_ref[gl]
        sr = jnp.sum(xr * am, axis=1, keepdims=True)
        si = jnp.sum(xi * am, axis=1, keepdims=True)
        a16r, a16i = pw_r[:, CHUNK:CHUNK + 1], pw_i[:, CHUNK:CHUNK + 1]
        sm_ref[gl, 0] = jnp.broadcast_to(sr, (ns, LANES))
        sm_ref[gl, 1] = jnp.broadcast_to(si, (ns, LANES))
        sm_ref[gl, 2] = jnp.broadcast_to(a16r * sr - a16i * si, (ns, LANES))
        sm_ref[gl, 3] = jnp.broadcast_to(a16r * si + a16i * sr, (ns, LANES))


def _ssm_prep(colp, b_re, b_im, c_re, c_im, a_meta, levels):
    g, gps = SSM_GROUPS, GROUPS_PER_STEP
    ns, gc = SSM_STATE, SSM_GROUP_CH
    spec = lambda *tail: pl.BlockSpec((gps,) + tail, lambda i: (i,) + (0,) * len(tail))
    return pl.pallas_call(
        functools.partial(_ssm_prep_kernel, levels=levels),
        grid=(g // gps,),
        in_specs=[spec(ns, 4), spec(ns, gc), spec(ns, gc), spec(gc, ns), spec(gc, ns), spec(1, TOEP)],
        out_specs=[spec(2 * ns + TOEP, TOEP), spec(TOEP, 2 * ns), spec(levels, ns, LANES),
                   spec(levels, ns, LANES), spec(4, ns, LANES)],
        out_shape=[jax.ShapeDtypeStruct((g, 2 * ns + TOEP, TOEP), _BF16),
                   jax.ShapeDtypeStruct((g, TOEP, 2 * ns), _BF16),
                   jax.ShapeDtypeStruct((g, levels, ns, LANES), _F32),
                   jax.ShapeDtypeStruct((g, levels, ns, LANES), _F32),
                   jax.ShapeDtypeStruct((g, 4, ns, LANES), _F32)],
        compiler_params=pltpu.CompilerParams(dimension_semantics=("arbitrary",),
                                             vmem_limit_bytes=VMEM_LIMIT),
        name="ssm_prep",
    )(colp, b_re, b_im, c_re, c_im, a_meta)


def _ssm_kernel(u_ref, w1_ref, g_ref, pwr_ref, pwi_ref, sm_ref, d_ref, z_ref,
                bre_ref, bim_ref, yint_ref, *, nchunks, levels):
    ns, gc, gps = SSM_STATE, SSM_GROUP_CH, GROUPS_PER_STEP
    nblk = nchunks // LANES
    rows = gps * ns
    for gl in range(gps):
        r0 = gl * gc
        a_t = jnp.concatenate(
            [u_ref[r0:r0 + gc, j * nchunks:(j + 1) * nchunks] for j in range(CHUNK)], axis=0)
        r1 = jnp.dot(w1_ref[gl], a_t, preferred_element_type=_F32)
        bre_ref[gl * ns:(gl + 1) * ns, :] = r1[:ns]
        bim_ref[gl * ns:(gl + 1) * ns, :] = r1[ns:2 * ns]
        yint_ref[gl] = r1[2 * ns:]
    lane = lax.broadcasted_iota(jnp.int32, (rows, LANES), 1)
    lane0 = lane == 0
    sm_re, sm_im = sm_ref[:, 0].reshape(rows, LANES), sm_ref[:, 1].reshape(rows, LANES)
    in_re, in_im = sm_ref[:, 2].reshape(rows, LANES), sm_ref[:, 3].reshape(rows, LANES)
    s_re = [bre_ref[:, h * LANES:(h + 1) * LANES] for h in range(nblk)]
    s_im = [bim_ref[:, h * LANES:(h + 1) * LANES] for h in range(nblk)]
    s_re[0] = s_re[0] + jnp.where(lane0, in_re, 0.0)
    s_im[0] = s_im[0] + jnp.where(lane0, in_im, 0.0)

    def shifted(blocks, sh, first):
        if sh % LANES == 0:
            k = sh // LANES
            return [None if h < k else blocks[h - k] for h in range(nblk)]
        rot = [pltpu.roll(blk, sh, axis=1) for blk in blocks]
        keep = lane >= sh
        return [jnp.where(keep, rot[h], first if h == 0 else rot[h - 1]) for h in range(nblk)]

    for lvl in range(levels):
        sh = 1 << lvl
        p_re = pwr_ref[:, lvl].reshape(rows, LANES)
        p_im = pwi_ref[:, lvl].reshape(rows, LANES)
        t_re, t_im = shifted(s_re, sh, 0.0), shifted(s_im, sh, 0.0)
        for h in range(nblk):
            if t_re[h] is not None:
                s_re[h], s_im[h] = (s_re[h] + p_re * t_re[h] - p_im * t_im[h],
                                    s_im[h] + p_re * t_im[h] + p_im * t_re[h])
    prev_re, prev_im = shifted(s_re, 1, sm_re), shifted(s_im, 1, sm_im)
    for h in range(nblk):
        bre_ref[:, h * LANES:(h + 1) * LANES] = prev_re[h]
        bim_ref[:, h * LANES:(h + 1) * LANES] = prev_im[h]
    for gl in range(gps):
        r0 = gl * gc
        s_prev = jnp.concatenate([bre_ref[gl * ns:(gl + 1) * ns, :], bim_ref[gl * ns:(gl + 1) * ns, :]],
                                 axis=0).astype(_BF16)
        y = yint_ref[gl] + jnp.dot(g_ref[gl], s_prev, preferred_element_type=_F32)
        d_col = jnp.tile(d_ref[r0:r0 + gc, :], (CHUNK, 1))
        for t in range(CHUNK):
            u_t = u_ref[r0:r0 + gc, t * nchunks:(t + 1) * nchunks].astype(_F32)
            z = jax.nn.gelu(y[t * gc:(t + 1) * gc, :] + d_col[t * gc:(t + 1) * gc, :] * u_t)
            z_ref[r0:r0 + gc, t * nchunks:(t + 1) * nchunks] = z.astype(z_ref.dtype)


def _ssm_core(u_t, w1, g_mat, pwr, pwi, s_meta, d_col, levels):
    b, n, s = u_t.shape
    gps, ns = GROUPS_PER_STEP, SSM_STATE
    rows = gps * SSM_GROUP_CH
    nchunks = s // CHUNK
    wspec = lambda *tail: pl.BlockSpec((gps,) + tail, lambda gb, i: (gb,) + (0,) * len(tail))
    return pl.pallas_call(
        functools.partial(_ssm_kernel, nchunks=nchunks, levels=levels),
        grid=(n // rows, b),
        in_specs=[
            pl.BlockSpec((None, rows, s), lambda gb, i: (i, gb, 0)),
            wspec(2 * ns + TOEP, TOEP), wspec(TOEP, 2 * ns),
            wspec(levels, ns, LANES), wspec(levels, ns, LANES), wspec(4, ns, LANES),
            pl.BlockSpec((rows, 1), lambda gb, i: (gb, 0)),
        ],
        out_specs=pl.BlockSpec((None, rows, s), lambda gb, i: (i, gb, 0)),
        out_shape=jax.ShapeDtypeStruct((b, n, s), _BF16),
        scratch_shapes=[pltpu.VMEM((gps * ns, nchunks), _F32), pltpu.VMEM((gps * ns, nchunks), _F32),
                        pltpu.VMEM((gps, TOEP, nchunks), _F32)],
        compiler_params=pltpu.CompilerParams(
            dimension_semantics=("arbitrary", "arbitrary"), vmem_limit_bytes=VMEM_LIMIT),
        name="ssm_core",
    )(u_t, w1, g_mat, pwr, pwi, s_meta, d_col)


def _attn_kernel(q_ref, k_ref, v_ref, kp_ref, vp_ref, km_ref, vm_ref, sink_ref, tri_ref, o_ref, acc_ref):
    first_step = pl.program_id(1) == 0
    wq = GQA_GROUP * WINDOW
    kj = lax.broadcasted_iota(jnp.int32, (WINDOW, wq), 0)
    qi = lax.broadcasted_iota(jnp.int32, (WINDOW, wq), 1) % WINDOW
    in_cur = kj <= qi
    units = [(blk, h) for blk in range(ATTN_BLOCKS) for h in range(N_KV_HEADS)]

    def kv_rows(ref, prev_ref, blk, h):
        rows = slice(h * HEAD_DIM, (h + 1) * HEAD_DIM)
        cur = ref[rows, blk * WINDOW:(blk + 1) * WINDOW]
        prev = prev_ref[rows, :] if blk == 0 else ref[rows, (blk - 1) * WINDOW:blk * WINDOW]
        return cur, prev

    scores = []
    for blk, h in units:
        q4 = jnp.concatenate(
            [q_ref[(h * GQA_GROUP + r) * HEAD_DIM:(h * GQA_GROUP + r + 1) * HEAD_DIM,
                   blk * WINDOW:(blk + 1) * WINDOW] for r in range(GQA_GROUP)], axis=1)
        k_cur, k_prev = kv_rows(k_ref, kp_ref, blk, h)
        k_meta = km_ref[h * HEAD_DIM:(h + 1) * HEAD_DIM, :]
        scores.append(tuple(lax.dot_general(k, q4, _TN, preferred_element_type=_F32)
                            for k in (k_cur, k_prev, k_meta)))
    for (blk, h), (s_cur, s_prev, s_meta) in zip(units, scores):
        if blk == 0:
            s_prev = jnp.where(first_step, -jnp.inf, s_prev)
        s_sel = jnp.where(in_cur, s_cur, s_prev)
        sink = sink_ref[h]
        m = jnp.maximum(jnp.maximum(jnp.max(s_sel, axis=0, keepdims=True),
                                    jnp.max(s_meta, axis=0, keepdims=True)), sink)
        e_sel = jnp.exp2(s_sel - m)
        e_meta = jnp.exp2(s_meta - m)
        den = (jnp.sum(e_sel, axis=0, keepdims=True) + jnp.sum(e_meta, axis=0, keepdims=True)
               + jnp.exp2(sink - m))
        e_bf = e_sel.astype(_BF16)
        p_cur = e_bf * tri_ref[...]
        p_all = jnp.concatenate([p_cur, e_bf - p_cur, e_meta.astype(_BF16)], axis=0)
        v_cur, v_prev = kv_rows(v_ref, vp_ref, blk, h)
        v_all = jnp.concatenate([v_cur, v_prev, vm_ref[h * HEAD_DIM:(h + 1) * HEAD_DIM, :]], axis=1)
        o = jnp.dot(v_all, p_all, preferred_element_type=_F32) * (1.0 / den)
        for r in range(GQA_GROUP):
            hq = h * GQA_GROUP + r
            acc_ref[hq * HEAD_DIM:(hq + 1) * HEAD_DIM, blk * WINDOW:(blk + 1) * WINDOW] = (
                o[:, r * WINDOW:(r + 1) * WINDOW])
    o_ref[...] = acc_ref[...].T.astype(o_ref.dtype)


def _attention(qkv_t, qkv_meta_t, sink_rows):
    b, _, s = qkv_t.shape
    kj = lax.broadcasted_iota(jnp.int32, (WINDOW, GQA_GROUP * WINDOW), 0)
    qi = lax.broadcasted_iota(jnp.int32, (WINDOW, GQA_GROUP * WINDOW), 1) % WINDOW
    tri = (kj <= qi).astype(_BF16)
    tq = ATTN_BLOCKS * WINDOW
    kblk = Q_W // KV_W
    prev_blk = lambda n: jnp.maximum(n * ATTN_BLOCKS - 1, 0)
    return pl.pallas_call(
        _attn_kernel,
        grid=(b, s // tq),
        in_specs=[
            pl.BlockSpec((None, Q_W, tq), lambda i, n: (i, 0, n)),
            pl.BlockSpec((None, KV_W, tq), lambda i, n: (i, kblk, n)),
            pl.BlockSpec((None, KV_W, tq), lambda i, n: (i, kblk + 1, n)),
            pl.BlockSpec((None, KV_W, WINDOW), lambda i, n: (i, kblk, prev_blk(n))),
            pl.BlockSpec((None, KV_W, WINDOW), lambda i, n: (i, kblk + 1, prev_blk(n))),
            pl.BlockSpec((None, KV_W, N_META), lambda i, n: (0, kblk, 0)),
            pl.BlockSpec((None, KV_W, N_META), lambda i, n: (0, kblk + 1, 0)),
            pl.BlockSpec(sink_rows.shape, lambda i, n: (0, 0, 0)),
            pl.BlockSpec(tri.shape, lambda i, n: (0, 0)),
        ],
        out_specs=pl.BlockSpec((None, tq, Q_W), lambda i, n: (i, n, 0)),
        out_shape=jax.ShapeDtypeStruct((b, s, Q_W), _F32),
        scratch_shapes=[pltpu.VMEM((Q_W, tq), _F32)],
        compiler_params=pltpu.CompilerParams(
            dimension_semantics=("arbitrary", "arbitrary"), vmem_limit_bytes=VMEM_LIMIT),
        name="swa_attention",
    )(qkv_t, qkv_t, qkv_t, qkv_t, qkv_t, qkv_meta_t, qkv_meta_t, sink_rows, tri)


def _tail_kernel(x_hbm, attn_hbm, z_ref, nm_ref, wg_ref, wglu_ref, abn_ref, sbn_ref, wout_ref,
                 nf_ref, wfi_ref, wfo_ref, o_hbm, xbuf, abuf, obuf, sem_x, sem_a, sem_o):
    d = D_MODEL
    nj = pl.num_programs(1)
    last = pl.num_programs(0) * nj - 1
    step = pl.program_id(0) * nj + pl.program_id(1)
    slot = step % 2

    def fetch(s, sl):
        bb, jj = s // nj, s % nj
        return (pltpu.make_async_copy(*_row_set_copy(x_hbm, xbuf, sem_x, bb, jj, sl)),
                pltpu.make_async_copy(*_row_set_copy(attn_hbm, abuf, sem_a, bb, jj, sl)))

    def put(s, sl):
        dst, src, sem = _row_set_copy(o_hbm, obuf, sem_o, s // nj, s % nj, sl)
        return pltpu.make_async_copy(src, dst, sem)

    @pl.when(step == 0)
    def _():
        for c in fetch(step, slot):
            c.start()

    @pl.when(step < last)
    def _():
        for c in fetch(step + 1, 1 - slot):
            c.start()

    for c in fetch(step, slot):
        c.wait()

    @pl.when(step >= 2)
    def _():
        put(step - 2, slot).wait()

    nrows = xbuf.shape[1]
    pieces = [slice(r, r + TAIL_PIECE) for r in range(0, nrows, TAIL_PIECE)]
    fw = D_FF // FFN_SPLIT
    xs = [xbuf[slot, p, :] for p in pieces]
    front = []
    for p, x in zip(pieces, xs):
        xn = _rms_rows(x, nm_ref[...]).astype(_BF16)
        gates = jnp.dot(xn, wg_ref[...], preferred_element_type=_F32)
        zz = lax.dot_general(z_ref[:, p], wglu_ref[...], _TN, preferred_element_type=_F32)
        front.append((gates, zz))
    hs = []
    for p, x, (gates, zz) in zip(pieces, xs, front):
        ssm = zz[:, :d] * jax.nn.sigmoid(zz[:, d:])
        merged = (jax.nn.sigmoid(gates[:, :d]) * _rms_rows(abuf[slot, p, :], abn_ref[...])
                  + jax.nn.sigmoid(gates[:, d:]) * _rms_rows(ssm, sbn_ref[...]))
        h = x + jnp.dot(merged.astype(_BF16), wout_ref[...], preferred_element_type=_F32)
        hs.append((h, _rms_rows(h, nf_ref[...]).astype(_BF16)))
    outs = [h for h, _ in hs]
    for c in range(FFN_SPLIT):
        for i, (_, hn) in enumerate(hs):
            gate = jnp.dot(hn, wfi_ref[:, c * fw:(c + 1) * fw], preferred_element_type=_F32)
            up = jnp.dot(hn, wfi_ref[:, D_FF + c * fw:D_FF + (c + 1) * fw], preferred_element_type=_F32)
            act = (jax.nn.silu(gate) * up).astype(_BF16)
            outs[i] = outs[i] + jnp.dot(act, wfo_ref[c * fw:(c + 1) * fw, :], preferred_element_type=_F32)
    for p, out in zip(pieces, outs):
        obuf[slot, p, :] = out
    put(step, slot).start()

    @pl.when(step == last)
    def _():
        put(step, slot).wait()

        @pl.when(step >= 1)
        def _():
            put(step - 1, 1 - slot).wait()


def _tail(x, attn, z_t, norm_mix, w_g, w_glu, abn, sbn, w_out, norm_ffn, w_fi, w_fo):
    b, s, d = x.shape
    nchunks = s // CHUNK
    x4 = x.reshape(b, nchunks, CHUNK, d)
    a4 = attn.reshape(b, nchunks, CHUNK, d)
    const = lambda a: pl.BlockSpec(a.shape, lambda i, j: (0,) * a.ndim, pipeline_mode=pl.Buffered(1))
    hbm = pl.BlockSpec(memory_space=pl.ANY)
    out = pl.pallas_call(
        _tail_kernel,
        grid=(b, CHUNK),
        in_specs=[
            hbm, hbm,
            pl.BlockSpec((None, d, nchunks), lambda i, j: (i, 0, j)),
            const(norm_mix), const(w_g), const(w_glu), const(abn), const(sbn), const(w_out),
            const(norm_ffn), const(w_fi), const(w_fo),
        ],
        out_specs=hbm,
        out_shape=jax.ShapeDtypeStruct(x4.shape, x.dtype),
        scratch_shapes=[pltpu.VMEM((2, nchunks, d), x.dtype), pltpu.VMEM((2, nchunks, d), attn.dtype),
                        pltpu.VMEM((2, nchunks, d), x.dtype),
                        pltpu.SemaphoreType.DMA((2,)), pltpu.SemaphoreType.DMA((2,)),
                        pltpu.SemaphoreType.DMA((2,))],
        compiler_params=pltpu.CompilerParams(
            dimension_semantics=("arbitrary", "arbitrary"), vmem_limit_bytes=VMEM_LIMIT),
        name="tail",
    )(x4, a4, z_t, norm_mix, w_g, w_glu, abn, sbn, w_out, norm_ffn, w_fi, w_fo)
    return out.reshape(b, s, d)


def kernel(x, meta_tokens, norm_mix, w_in, q_norm, k_norm, attn_sinks, lam_re, lam_im, log_dt,
           ssm_b_re, ssm_b_im, ssm_c_re, ssm_c_im, ssm_d, w_glu, attn_branch_norm, ssm_branch_norm,
           w_out, norm_ffn, w_ffn_in, w_ffn_out):
    seq = x.shape[1]
    levels = (seq // CHUNK).bit_length() - 1
    w = w_in[0]
    w_qkv_t = w[:, :QKV_W].T.astype(_BF16)
    w_u_t = w[:, QKV_W:QKV_W + D_MODEL].T.astype(_BF16)
    w_g = w[:, QKV_W + D_MODEL:].astype(_BF16)
    scale = HEAD_DIM ** -0.5 * LOG2_E
    qk_gain = jnp.concatenate([jnp.tile(q_norm[0] * scale, N_Q_HEADS),
                               jnp.tile(k_norm[0], N_KV_HEADS)])[:, None].astype(_F32)
    no_gain = jnp.zeros((8, 1), _F32)
    sink_rows = jnp.repeat((attn_sinks[0] * LOG2_E).reshape(N_KV_HEADS, 1, GQA_GROUP), WINDOW,
                           axis=2).astype(_F32)
    dt = jnp.exp(log_dt[0].astype(_F32))
    lr, li = lam_re[0].astype(_F32), lam_im[0].astype(_F32)
    dtb = jnp.broadcast_to(dt[:, None], lr.shape)
    colp = jnp.stack([lr, li, dtb, jnp.zeros_like(lr)], axis=2)
    d_col = ssm_d[0].astype(_F32)[:, None]

    meta = meta_tokens.astype(_F32)[None]
    qkv_meta_t = _proj_t(meta, w_qkv_t, norm_mix, qk_gain, norm_heads=N_Q_HEADS + N_KV_HEADS,
                         tm=N_META, name="proj_qkv_meta")
    u_meta_t = _proj_t(meta, w_u_t, norm_mix, no_gain, norm_heads=0, tm=N_META, name="proj_u_meta")
    a_meta = (u_meta_t[0].astype(_F32).reshape(SSM_GROUPS, SSM_GROUP_CH, CHUNK)
              .transpose(0, 2, 1).reshape(SSM_GROUPS, 1, TOEP))
    w1, g_mat, pwr, pwi, s_meta = _ssm_prep(
        colp, ssm_b_re[0].astype(_F32), ssm_b_im[0].astype(_F32),
        ssm_c_re[0].astype(_F32), ssm_c_im[0].astype(_F32), a_meta, levels)

    qkv_t = _proj_t(x, w_qkv_t, norm_mix, qk_gain, norm_heads=N_Q_HEADS + N_KV_HEADS,
                    tm=4 * PROJ_PIECE, name="proj_qkv")
    u_t = _proj_u_perm(x, w_u_t, norm_mix)
    attn = _attention(qkv_t, qkv_meta_t, sink_rows)
    z_t = _ssm_core(u_t, w1, g_mat, pwr, pwi, s_meta, d_col, levels)
    return _tail(x, attn, z_t, norm_mix, w_g, w_glu[0].astype(_BF16), attn_branch_norm,
                 ssm_branch_norm, w_out[0].astype(_BF16), norm_ffn, w_ffn_in[0].astype(_BF16),
                 w_ffn_out[0].astype(_BF16))
```

```python
import functools

import jax
import jax.numpy as jnp
from jax import lax
from jax.experimental import pallas as pl
from jax.experimental.pallas import tpu as pltpu

D_MODEL = 1024
N_META = 16
HEAD_DIM = 64
N_Q_HEADS = 16
N_KV_HEADS = 4
GQA_GROUP = N_Q_HEADS // N_KV_HEADS
WINDOW = 128
SSM_GROUP_CH = 16
SSM_GROUPS = D_MODEL // SSM_GROUP_CH
SSM_STATE = 64
D_FF = 2816
Q_W = N_Q_HEADS * HEAD_DIM
KV_W = N_KV_HEADS * HEAD_DIM
QKV_W = Q_W + 2 * KV_W
EPS = 1e-6
LOG2_E = 1.4426950408889634

LANES = 128
CHUNK = 16
TOEP = CHUNK * SSM_GROUP_CH
GROUPS_PER_STEP = 8
MXU_DIM = 256
FFN_EDGES = (0, 6 * MXU_DIM, D_FF)
ATTN_BLOCKS = 4
PROJ_PIECE = 256
TAIL_PIECE = 256
PROJ_U_SETS = 2
VMEM_LIMIT = 56 * 1024 * 1024

_F32 = jnp.float32
_BF16 = jnp.bfloat16
_NT = (((1,), (1,)), ((), ()))
_TN = (((0,), (0,)), ((), ()))
_HI = lax.Precision.HIGHEST


def _rms_rows(x, gain):
    return x * lax.rsqrt(jnp.mean(x * x, axis=-1, keepdims=True) + EPS) * gain


def _proj_t_body(x_ref, w_ref, nm_ref, gain_ref, o_ref, norm_heads):
    tm_all = x_ref.shape[0]
    tm = min(tm_all, PROJ_PIECE)
    for c in range(tm_all // tm):
        cols = slice(c * tm, (c + 1) * tm)
        xn = _rms_rows(x_ref[cols, :], nm_ref[...]).astype(_BF16)
        p = lax.dot_general(w_ref[...], xn, _NT, preferred_element_type=_F32)
        if norm_heads:
            rows = norm_heads * HEAD_DIM
            hd = p[:rows].reshape(norm_heads, HEAD_DIM, tm)
            ms = jnp.mean(hd * hd, axis=1, keepdims=True)
            hd = hd * lax.rsqrt(ms + EPS) * gain_ref[...].reshape(norm_heads, HEAD_DIM, 1)
            o_ref[:rows, cols] = hd.reshape(rows, tm).astype(o_ref.dtype)
            o_ref[rows:, cols] = p[rows:].astype(o_ref.dtype)
        else:
            o_ref[:, cols] = p.astype(o_ref.dtype)


def _proj_t_kernel(x_ref, w_ref, nm_ref, gain_ref, o_ref, *, norm_heads):
    _proj_t_body(x_ref, w_ref, nm_ref, gain_ref, o_ref, norm_heads)


def _proj_t(x, w_t, norm_mix, gain, *, norm_heads, tm, name):
    b, s, d = x.shape
    n = w_t.shape[0]
    return pl.pallas_call(
        functools.partial(_proj_t_kernel, norm_heads=norm_heads),
        grid=(b, s // tm),
        in_specs=[
            pl.BlockSpec((None, tm, d), lambda i, j: (i, j, 0)),
            pl.BlockSpec((n, d), lambda i, j: (0, 0)),
            pl.BlockSpec((1, d), lambda i, j: (0, 0)),
            pl.BlockSpec(gain.shape, lambda i, j: (0, 0)),
        ],
        out_specs=pl.BlockSpec((None, n, tm), lambda i, j: (i, 0, j)),
        out_shape=jax.ShapeDtypeStruct((b, n, s), _BF16),
        compiler_params=pltpu.CompilerParams(
            dimension_semantics=("arbitrary", "arbitrary"), vmem_limit_bytes=VMEM_LIMIT),
        name=name,
    )(x, w_t, norm_mix, gain)


def _row_set_copy(hbm4, buf, sem, bb, jj, slot):
    return hbm4.at[bb, :, jj, :], buf.at[slot], sem.at[slot]


def _proj_u_kernel(x_hbm, w_ref, nm_ref, o_ref, xbuf, sem):
    nj = pl.num_programs(1)
    last = pl.num_programs(0) * nj - 1
    step = pl.program_id(0) * nj + pl.program_id(1)
    slot = step % 2
    nchunks = x_hbm.shape[1]

    def fetch(s, sl):
        return [pltpu.make_async_copy(x_hbm.at[s // nj, :, (s % nj) * PROJ_U_SETS + k, :],
                                      xbuf.at[sl, k * nchunks:(k + 1) * nchunks, :], sem.at[sl, k])
                for k in range(PROJ_U_SETS)]

    @pl.when(step == 0)
    def _():
        for c in fetch(step, slot):
            c.start()

    @pl.when(step < last)
    def _():
        for c in fetch(step + 1, 1 - slot):
            c.start()

    for c in fetch(step, slot):
        c.wait()
    _proj_t_body(xbuf.at[slot], w_ref, nm_ref, None, o_ref, 0)


def _proj_u_perm(x, w_t, norm_mix):
    b, s, d = x.shape
    n = w_t.shape[0]
    nchunks = s // CHUNK
    x4 = x.reshape(b, nchunks, CHUNK, d)
    return pl.pallas_call(
        _proj_u_kernel,
        grid=(b, CHUNK // PROJ_U_SETS),
        in_specs=[
            pl.BlockSpec(memory_space=pl.ANY),
            pl.BlockSpec((n, d), lambda i, j: (0, 0)),
            pl.BlockSpec((1, d), lambda i, j: (0, 0)),
        ],
        out_specs=pl.BlockSpec((None, n, PROJ_U_SETS * nchunks), lambda i, j: (i, 0, j)),
        out_shape=jax.ShapeDtypeStruct((b, n, s), _BF16),
        scratch_shapes=[pltpu.VMEM((2, PROJ_U_SETS * nchunks, d), x.dtype),
                        pltpu.SemaphoreType.DMA((2, PROJ_U_SETS))],
        compiler_params=pltpu.CompilerParams(
            dimension_semantics=("arbitrary", "arbitrary"), vmem_limit_bytes=VMEM_LIMIT),
        name="proj_u_perm",
    )(x4, w_t, norm_mix)


def _ssm_prep_kernel(colp_ref, bre_ref, bim_ref, cre_ref, cim_ref, am_ref, d_ref,
                     w1_ref, w3_ref, pwr_ref, pwi_ref, sm_ref, *, levels):
    ns, gc = SSM_STATE, SSM_GROUP_CH
    expand = lambda a, m: jnp.dot(a, m, precision=_HI, preferred_element_type=_F32)
    l16 = lax.broadcasted_iota(jnp.int32, (gc, TOEP), 1)
    r16 = lax.broadcasted_iota(jnp.int32, (gc, TOEP), 0)
    tile_c = (l16 % gc == r16).astype(_F32)
    tau_r = lax.broadcasted_iota(jnp.int32, (LANES, TOEP), 0)
    jc_l = lax.broadcasted_iota(jnp.int32, (LANES, TOEP), 1)
    pick_rev = (tau_r == CHUNK - 1 - jc_l // gc).astype(_F32)
    tc_r = lax.broadcasted_iota(jnp.int32, (TOEP, LANES), 0)
    tau_l = lax.broadcasted_iota(jnp.int32, (TOEP, LANES), 1)
    pick_next = (tau_l == tc_r // gc + 1).astype(_F32)
    lane = lax.broadcasted_iota(jnp.int32, (ns, LANES), 1)
    lvl = jnp.clip(lane - CHUNK, 0, levels - 1)
    expo = jnp.where(lane <= CHUNK, lane,
                     jnp.where(lane < CHUNK + levels, CHUNK * jnp.left_shift(1, lvl), 0)).astype(_F32)
    is_tau = lane <= CHUNK
    for gl in range(GROUPS_PER_STEP):
        lr, li, dt = colp_ref[gl, :, 0:1], colp_ref[gl, :, 1:2], colp_ref[gl, :, 2:3]
        mag = jnp.exp(lr * dt * expo)
        ang = li * dt * expo
        pw_r, pw_i = mag * jnp.cos(ang), mag * jnp.sin(ang)
        ar, ai = pw_r[:, 1:2], pw_i[:, 1:2]
        den = lr * lr + li * li
        nr, ni = ar - 1.0, ai
        fr, fi = (nr * lr + ni * li) / den, (ni * lr - nr * li) / den
        b_re, b_im = bre_ref[gl], bim_ref[gl]
        bbr_t = expand(fr * b_re - fi * b_im, tile_c)
        bbi_t = expand(fr * b_im + fi * b_re, tile_c)
        tau_re, tau_im = jnp.where(is_tau, pw_r, 0.0), jnp.where(is_tau, pw_i, 0.0)
        pr, pi = expand(tau_re, pick_rev), expand(tau_im, pick_rev)
        xr = pr * bbr_t - pi * bbi_t
        xi = pr * bbi_t + pi * bbr_t
        krev = (jnp.dot(cre_ref[gl], xr, precision=_HI, preferred_element_type=_F32)
                - jnp.dot(cim_ref[gl], xi, precision=_HI, preferred_element_type=_F32))
        krev = krev + jnp.where(l16 - (CHUNK - 1) * gc == r16, d_ref[gl], 0.0)
        rz = jnp.concatenate([krev, jnp.zeros_like(krev)], axis=1)
        toep = jnp.concatenate(
            [rz[:, (CHUNK - 1 - t) * gc:(CHUNK - 1 - t) * gc + TOEP] for t in range(CHUNK)], axis=0)
        w1_ref[gl] = jnp.concatenate([xr, xi], axis=0).astype(w1_ref.dtype)
        gar = expand(pick_next, tau_re.T)
        gai = expand(pick_next, tau_im.T)
        crt = jnp.tile(cre_ref[gl], (CHUNK, 1))
        cit = jnp.tile(cim_ref[gl], (CHUNK, 1))
        w3_ref[gl] = jnp.concatenate([toep, crt * gar - cit * gai, -(crt * gai + cit * gar)],
                                     axis=1).astype(w3_ref.dtype)
        for l in range(levels):
            pwr_ref[gl, l] = jnp.broadcast_to(pw_r[:, CHUNK + l:CHUNK + l + 1], (ns, LANES))
            pwi_ref[gl, l] = jnp.broadcast_to(pw_i[:, CHUNK + l:CHUNK + l + 1], (ns, LANES))
        am = am_ref[gl]
        sr = jnp.sum(xr * am, axis=1, keepdims=True)
        si = jnp.sum(xi * am, axis=1, keepdims=True)
        a16r, a16i = pw_r[:, CHUNK:CHUNK + 1], pw_i[:, CHUNK:CHUNK + 1]
        sm_ref[gl, 0] = jnp.broadcast_to(sr, (ns, LANES))
        sm_ref[gl, 1] = jnp.broadcast_to(si, (ns, LANES))
        sm_ref[gl, 2] = jnp.broadcast_to(a16r * sr - a16i * si, (ns, LANES))
        sm_ref[gl, 3] = jnp.broadcast_to(a16r * si + a16i * sr, (ns, LANES))


def _ssm_prep(colp, b_re, b_im, c_re, c_im, a_meta, d_grp, levels):
    g, gps = SSM_GROUPS, GROUPS_PER_STEP
    ns, gc = SSM_STATE, SSM_GROUP_CH
    spec = lambda *tail: pl.BlockSpec((gps,) + tail, lambda i: (i,) + (0,) * len(tail))
    return pl.pallas_call(
        functools.partial(_ssm_prep_kernel, levels=levels),
        grid=(g // gps,),
        in_specs=[spec(ns, 4), spec(ns, gc), spec(ns, gc), spec(gc, ns), spec(gc, ns), spec(1, TOEP),
                  spec(gc, 1)],
        out_specs=[spec(2 * ns, TOEP), spec(TOEP, TOEP + 2 * ns), spec(levels, ns, LANES),
                   spec(levels, ns, LANES), spec(4, ns, LANES)],
        out_shape=[jax.ShapeDtypeStruct((g, 2 * ns, TOEP), _BF16),
                   jax.ShapeDtypeStruct((g, TOEP, TOEP + 2 * ns), _BF16),
                   jax.ShapeDtypeStruct((g, levels, ns, LANES), _F32),
                   jax.ShapeDtypeStruct((g, levels, ns, LANES), _F32),
                   jax.ShapeDtypeStruct((g, 4, ns, LANES), _F32)],
        compiler_params=pltpu.CompilerParams(dimension_semantics=("arbitrary",),
                                             vmem_limit_bytes=VMEM_LIMIT),
        name="ssm_prep",
    )(colp, b_re, b_im, c_re, c_im, a_meta, d_grp)


def _ssm_kernel(u_ref, w1_ref, w3_ref, pwr_ref, pwi_ref, sm_ref, y_ref,
                bre_ref, bim_ref, *, nchunks, levels):
    ns, gc, gps = SSM_STATE, SSM_GROUP_CH, GROUPS_PER_STEP
    nblk = nchunks // LANES
    rows = gps * ns

    def chunk_operand(gl):
        r0 = gl * gc
        return jnp.concatenate(
            [u_ref[r0:r0 + gc, j * nchunks:(j + 1) * nchunks] for j in range(CHUNK)], axis=0)

    for gl in range(gps):
        r1 = jnp.dot(w1_ref[gl], chunk_operand(gl), preferred_element_type=_F32)
        bre_ref[gl * ns:(gl + 1) * ns, :] = r1[:ns]
        bim_ref[gl * ns:(gl + 1) * ns, :] = r1[ns:]
    lane = lax.broadcasted_iota(jnp.int32, (ns, LANES), 1)
    lane0 = lane == 0
    blocks = [slice(h * LANES, (h + 1) * LANES) for h in range(nblk)]

    def shifted(vals, sh, first):
        if sh % LANES == 0:
            k = sh // LANES
            return [None if h < k else vals[h - k] for h in range(nblk)]
        rot = [pltpu.roll(v, sh, axis=1) for v in vals]
        keep = lane >= sh
        return [jnp.where(keep, rot[h], first if h == 0 else rot[h - 1]) for h in range(nblk)]

    for gl in range(gps):
        rs = slice(gl * ns, (gl + 1) * ns)
        bre_ref[rs, blocks[0]] += jnp.where(lane0, sm_ref[gl, 2], 0.0)
        bim_ref[rs, blocks[0]] += jnp.where(lane0, sm_ref[gl, 3], 0.0)
    for lvl in range(levels):
        sh = 1 << lvl
        for gl in range(gps):
            rs = slice(gl * ns, (gl + 1) * ns)
            p_re, p_im = pwr_ref[gl, lvl], pwi_ref[gl, lvl]
            s_re = [bre_ref[rs, blk] for blk in blocks]
            s_im = [bim_ref[rs, blk] for blk in blocks]
            t_re, t_im = shifted(s_re, sh, 0.0), shifted(s_im, sh, 0.0)
            for h in range(nblk):
                if t_re[h] is not None:
                    bre_ref[rs, blocks[h]] = s_re[h] + p_re * t_re[h] - p_im * t_im[h]
                    bim_ref[rs, blocks[h]] = s_im[h] + p_re * t_im[h] + p_im * t_re[h]
    for gl in range(gps):
        rs = slice(gl * ns, (gl + 1) * ns)
        prev_re = shifted([bre_ref[rs, blk] for blk in blocks], 1, sm_ref[gl, 0])
        prev_im = shifted([bim_ref[rs, blk] for blk in blocks], 1, sm_ref[gl, 1])
        for h in range(nblk):
            bre_ref[rs, blocks[h]] = prev_re[h]
            bim_ref[rs, blocks[h]] = prev_im[h]
    for gl in range(gps):
        r0 = gl * gc
        rhs = jnp.concatenate([chunk_operand(gl),
                               bre_ref[gl * ns:(gl + 1) * ns, :].astype(_BF16),
                               bim_ref[gl * ns:(gl + 1) * ns, :].astype(_BF16)], axis=0)
        y = jnp.dot(w3_ref[gl], rhs, preferred_element_type=_F32)
        for t in range(CHUNK):
            y_ref[r0:r0 + gc, t * nchunks:(t + 1) * nchunks] = y[t * gc:(t + 1) * gc, :]


def _ssm_core(u_t, w1, w3, pwr, pwi, s_meta, levels):
    b, n, s = u_t.shape
    gps, ns = GROUPS_PER_STEP, SSM_STATE
    rows = gps * SSM_GROUP_CH
    nchunks = s // CHUNK
    wspec = lambda *tail: pl.BlockSpec((gps,) + tail, lambda gb, i: (gb,) + (0,) * len(tail))
    return pl.pallas_call(
        functools.partial(_ssm_kernel, nchunks=nchunks, levels=levels),
        grid=(n // rows, b),
        in_specs=[
            pl.BlockSpec((None, rows, s), lambda gb, i: (i, gb, 0)),
            wspec(2 * ns, TOEP), wspec(TOEP, TOEP + 2 * ns),
            wspec(levels, ns, LANES), wspec(levels, ns, LANES), wspec(4, ns, LANES),
        ],
        out_specs=pl.BlockSpec((None, rows, s), lambda gb, i: (i, gb, 0)),
        out_shape=jax.ShapeDtypeStruct((b, n, s), _F32),
        scratch_shapes=[pltpu.VMEM((gps * ns, nchunks), _F32), pltpu.VMEM((gps * ns, nchunks), _F32)],
        compiler_params=pltpu.CompilerParams(
            dimension_semantics=("arbitrary", "arbitrary"), vmem_limit_bytes=VMEM_LIMIT),
        name="ssm_core",
    )(u_t, w1, w3, pwr, pwi, s_meta)


def _attn_kernel(q_ref, k_ref, v_ref, kp_ref, vp_ref, km_ref, vm_ref, sink_ref, tri_ref, o_ref, acc_ref):
    first_step = pl.program_id(1) == 0
    wq = GQA_GROUP * WINDOW
    kj = lax.broadcasted_iota(jnp.int32, (WINDOW, wq), 0)
    qi = lax.broadcasted_iota(jnp.int32, (WINDOW, wq), 1) % WINDOW
    in_cur = kj <= qi
    units = [(blk, h) for blk in range(ATTN_BLOCKS) for h in range(N_KV_HEADS)]

    def kv_rows(ref, prev_ref, blk, h):
        rows = slice(h * HEAD_DIM, (h + 1) * HEAD_DIM)
        cur = ref[rows, blk * WINDOW:(blk + 1) * WINDOW]
        prev = prev_ref[rows, :] if blk == 0 else ref[rows, (blk - 1) * WINDOW:blk * WINDOW]
        return cur, prev

    scores = []
    for blk, h in units:
        q4 = jnp.concatenate(
            [q_ref[(h * GQA_GROUP + r) * HEAD_DIM:(h * GQA_GROUP + r + 1) * HEAD_DIM,
                   blk * WINDOW:(blk + 1) * WINDOW] for r in range(GQA_GROUP)], axis=1)
        k_cur, k_prev = kv_rows(k_ref, kp_ref, blk, h)
        k_meta = km_ref[h * HEAD_DIM:(h + 1) * HEAD_DIM, :]
        scores.append(tuple(lax.dot_general(k, q4, _TN, preferred_element_type=_F32)
                            for k in (k_cur, k_prev, k_meta)))
    for (blk, h), (s_cur, s_prev, s_meta) in zip(units, scores):
        if blk == 0:
            s_prev = jnp.where(first_step, -jnp.inf, s_prev)
        s_sel = jnp.where(in_cur, s_cur, s_prev)
        sink = sink_ref[h]
        m = jnp.maximum(jnp.maximum(jnp.max(s_sel, axis=0, keepdims=True),
                                    jnp.max(s_meta, axis=0, keepdims=True)), sink)
        e_sel = jnp.exp2(s_sel - m)
        e_meta = jnp.exp2(s_meta - m)
        den = (jnp.sum(e_sel, axis=0, keepdims=True) + jnp.sum(e_meta, axis=0, keepdims=True)
               + jnp.exp2(sink - m))
        e_bf = e_sel.astype(_BF16)
        p_cur = e_bf * tri_ref[...]
        p_all = jnp.concatenate([p_cur, e_bf - p_cur, e_meta.astype(_BF16)], axis=0)
        v_cur, v_prev = kv_rows(v_ref, vp_ref, blk, h)
        v_all = jnp.concatenate([v_cur, v_prev, vm_ref[h * HEAD_DIM:(h + 1) * HEAD_DIM, :]], axis=1)
        o = jnp.dot(v_all, p_all, preferred_element_type=_F32) * (1.0 / den)
        for r in range(GQA_GROUP):
            hq = h * GQA_GROUP + r
            acc_ref[hq * HEAD_DIM:(hq + 1) * HEAD_DIM, blk * WINDOW:(blk + 1) * WINDOW] = (
                o[:, r * WINDOW:(r + 1) * WINDOW])
    o_ref[...] = acc_ref[...].T.astype(o_ref.dtype)


def _attention(qkv_t, qkv_meta_t, sink_rows):
    b, _, s = qkv_t.shape
    kj = lax.broadcasted_iota(jnp.int32, (WINDOW, GQA_GROUP * WINDOW), 0)
    qi = lax.broadcasted_iota(jnp.int32, (WINDOW, GQA_GROUP * WINDOW), 1) % WINDOW
    tri = (kj <= qi).astype(_BF16)
    tq = ATTN_BLOCKS * WINDOW
    kblk = Q_W // KV_W
    prev_blk = lambda n: jnp.maximum(n * ATTN_BLOCKS - 1, 0)
    return pl.pallas_call(
        _attn_kernel,
        grid=(b, s // tq),
        in_specs=[
            pl.BlockSpec((None, Q_W, tq), lambda i, n: (i, 0, n)),
            pl.BlockSpec((None, KV_W, tq), lambda i, n: (i, kblk, n)),
            pl.BlockSpec((None, KV_W, tq), lambda i, n: (i, kblk + 1, n)),
            pl.BlockSpec((None, KV_W, WINDOW), lambda i, n: (i, kblk, prev_blk(n))),
            pl.BlockSpec((None, KV_W, WINDOW), lambda i, n: (i, kblk + 1, prev_blk(n))),
            pl.BlockSpec((None, KV_W, N_META), lambda i, n: (0, kblk, 0)),
            pl.BlockSpec((None, KV_W, N_META), lambda i, n: (0, kblk + 1, 0)),
            pl.BlockSpec(sink_rows.shape, lambda i, n: (0, 0, 0)),
            pl.BlockSpec(tri.shape, lambda i, n: (0, 0)),
        ],
        out_specs=pl.BlockSpec((None, tq, Q_W), lambda i, n: (i, n, 0)),
        out_shape=jax.ShapeDtypeStruct((b, s, Q_W), _F32),
        scratch_shapes=[pltpu.VMEM((Q_W, tq), _F32)],
        compiler_params=pltpu.CompilerParams(
            dimension_semantics=("arbitrary", "arbitrary"), vmem_limit_bytes=VMEM_LIMIT),
        name="swa_attention",
    )(qkv_t, qkv_t, qkv_t, qkv_t, qkv_t, qkv_meta_t, qkv_meta_t, sink_rows, tri)


def _tail_kernel(x_hbm, attn_hbm, y_ref, nm_ref, wg_ref, wglu_ref, abn_ref, sbn_ref, wout_ref,
                 nf_ref, wfi_ref, wfo_ref, o_hbm, xbuf, abuf, obuf, sem_x, sem_a, sem_o):
    d = D_MODEL
    nj = pl.num_programs(1)
    last = pl.num_programs(0) * nj - 1
    step = pl.program_id(0) * nj + pl.program_id(1)
    slot = step % 2

    def fetch(s, sl):
        bb, jj = s // nj, s % nj
        return (pltpu.make_async_copy(*_row_set_copy(x_hbm, xbuf, sem_x, bb, jj, sl)),
                pltpu.make_async_copy(*_row_set_copy(attn_hbm, abuf, sem_a, bb, jj, sl)))

    def put(s, sl):
        dst, src, sem = _row_set_copy(o_hbm, obuf, sem_o, s // nj, s % nj, sl)
        return pltpu.make_async_copy(src, dst, sem)

    @pl.when(step == 0)
    def _():
        for c in fetch(step, slot):
            c.start()

    @pl.when(step < last)
    def _():
        for c in fetch(step + 1, 1 - slot):
            c.start()

    for c in fetch(step, slot):
        c.wait()

    @pl.when(step >= 2)
    def _():
        put(step - 2, slot).wait()

    nrows = xbuf.shape[1]
    pieces = [slice(r, r + TAIL_PIECE) for r in range(0, nrows, TAIL_PIECE)]
    xs =[xbuf[slot, p, :] for p in pieces]
    front = []
    for p, x in zip(pieces, xs):
        xn = _rms_rows(x, nm_ref[...]).astype(_BF16)
        gates = jnp.dot(xn, wg_ref[...], preferred_element_type=_F32)
        z = jax.nn.gelu(y_ref[:, p]).astype(_BF16)
        zz = lax.dot_general(z, wglu_ref[...], _TN, preferred_element_type=_F32)
        front.append((gates, zz))
    hs = []
    for p, x, (gates, zz) in zip(pieces, xs, front):
        ssm = zz[:, :d] * jax.nn.sigmoid(zz[:, d:])
        merged = (jax.nn.sigmoid(gates[:, :d]) * _rms_rows(abuf[slot, p, :], abn_ref[...])
                  + jax.nn.sigmoid(gates[:, d:]) * _rms_rows(ssm, sbn_ref[...]))
        h = x + jnp.dot(merged.astype(_BF16), wout_ref[...], preferred_element_type=_F32)
        hs.append((h, _rms_rows(h, nf_ref[...]).astype(_BF16)))
    outs = [h for h, _ in hs]
    for c0, c1 in zip(FFN_EDGES[:-1], FFN_EDGES[1:]):
        for i, (_, hn) in enumerate(hs):
            gate = jnp.dot(hn, wfi_ref[:, c0:c1], preferred_element_type=_F32)
            up = jnp.dot(hn, wfi_ref[:, D_FF + c0:D_FF + c1], preferred_element_type=_F32)
            act = (jax.nn.silu(gate) * up).astype(_BF16)
            outs[i] = outs[i] + jnp.dot(act, wfo_ref[c0:c1, :], preferred_element_type=_F32)
    for p, out in zip(pieces, outs):
        obuf[slot, p, :] = out
    put(step, slot).start()

    @pl.when(step == last)
    def _():
        put(step, slot).wait()

        @pl.when(step >= 1)
        def _():
            put(step - 1, 1 - slot).wait()


def _tail(x, attn, y_t, norm_mix, w_g, w_glu, abn, sbn, w_out, norm_ffn, w_fi, w_fo):
    b, s, d = x.shape
    nchunks = s // CHUNK
    x4 = x.reshape(b, nchunks, CHUNK, d)
    a4 = attn.reshape(b, nchunks, CHUNK, d)
    const = lambda a: pl.BlockSpec(a.shape, lambda i, j: (0,) * a.ndim, pipeline_mode=pl.Buffered(1))
    hbm = pl.BlockSpec(memory_space=pl.ANY)
    out = pl.pallas_call(
        _tail_kernel,
        grid=(b, CHUNK),
        in_specs=[
            hbm, hbm,
            pl.BlockSpec((None, d, nchunks), lambda i, j: (i, 0, j)),
            const(norm_mix), const(w_g), const(w_glu), const(abn), const(sbn), const(w_out),
            const(norm_ffn), const(w_fi), const(w_fo),
        ],
        out_specs=hbm,
        out_shape=jax.ShapeDtypeStruct(x4.shape, x.dtype),
        scratch_shapes=[pltpu.VMEM((2, nchunks, d), x.dtype), pltpu.VMEM((2, nchunks, d), attn.dtype),
                        pltpu.VMEM((2, nchunks, d), x.dtype),
                        pltpu.SemaphoreType.DMA((2,)), pltpu.SemaphoreType.DMA((2,)),
                        pltpu.SemaphoreType.DMA((2,))],
        compiler_params=pltpu.CompilerParams(
            dimension_semantics=("arbitrary", "arbitrary"), vmem_limit_bytes=VMEM_LIMIT),
        name="tail",
    )(x4, a4, y_t, norm_mix, w_g, w_glu, abn, sbn, w_out, norm_ffn, w_fi, w_fo)
    return out.reshape(b, s, d)


def kernel(x, meta_tokens, norm_mix, w_in, q_norm, k_norm, attn_sinks, lam_re, lam_im, log_dt,
           ssm_b_re, ssm_b_im, ssm_c_re, ssm_c_im, ssm_d, w_glu, attn_branch_norm, ssm_branch_norm,
           w_out, norm_ffn, w_ffn_in, w_ffn_out):
    seq = x.shape[1]
    levels = (seq // CHUNK).bit_length() - 1
    w = w_in[0]
    w_qkv_t = w[:, :QKV_W].T.astype(_BF16)
    w_u_t = w[:, QKV_W:QKV_W + D_MODEL].T.astype(_BF16)
    w_g = w[:, QKV_W + D_MODEL:].astype(_BF16)
    scale = HEAD_DIM ** -0.5 * LOG2_E
    qk_gain = jnp.concatenate([jnp.tile(q_norm[0] * scale, N_Q_HEADS),
                               jnp.tile(k_norm[0], N_KV_HEADS)])[:, None].astype(_F32)
    no_gain = jnp.zeros((8, 1), _F32)
    sink_rows = jnp.repeat((attn_sinks[0] * LOG2_E).reshape(N_KV_HEADS, 1, GQA_GROUP), WINDOW,
                           axis=2).astype(_F32)
    dt = jnp.exp(log_dt[0].astype(_F32))
    lr, li = lam_re[0].astype(_F32), lam_im[0].astype(_F32)
    dtb = jnp.broadcast_to(dt[:, None], lr.shape)
    colp = jnp.stack([lr, li, dtb, jnp.zeros_like(lr)], axis=2)
    d_grp = ssm_d[0].astype(_F32).reshape(SSM_GROUPS, SSM_GROUP_CH, 1)

    meta = meta_tokens.astype(_F32)[None]
    qkv_meta_t = _proj_t(meta, w_qkv_t, norm_mix, qk_gain, norm_heads=N_Q_HEADS + N_KV_HEADS,
                         tm=N_META, name="proj_qkv_meta")
    u_meta_t = _proj_t(meta, w_u_t, norm_mix, no_gain, norm_heads=0, tm=N_META, name="proj_u_meta")
    a_meta = (u_meta_t[0].astype(_F32).reshape(SSM_GROUPS, SSM_GROUP_CH, CHUNK)
              .transpose(0, 2, 1).reshape(SSM_GROUPS, 1, TOEP))
    w1, w3, pwr, pwi, s_meta = _ssm_prep(
        colp, ssm_b_re[0].astype(_F32), ssm_b_im[0].astype(_F32),
        ssm_c_re[0].astype(_F32), ssm_c_im[0].astype(_F32), a_meta, d_grp, levels)

    qkv_t = _proj_t(x, w_qkv_t, norm_mix, qk_gain, norm_heads=N_Q_HEADS + N_KV_HEADS,
                    tm=4 * PROJ_PIECE, name="proj_qkv")
    u_t = _proj_u_perm(x, w_u_t, norm_mix)
    attn = _attention(qkv_t, qkv_meta_t, sink_rows)
    y_t = _ssm_core(u_t, w1, w3, pwr, pwi, s_meta, levels)
    return _tail(x, attn, y_t, norm_mix, w_g, w_glu[0].astype(_BF16), attn_branch_norm,
                 ssm_branch_norm, w_out[0].astype(_BF16), norm_ffn, w_ffn_in[0].astype(_BF16),
                 w_ffn_out[0].astype(_BF16))
```

```python
import functools

import jax
import jax.numpy as jnp
from jax import lax
from jax.experimental import pallas as pl
from jax.experimental.pallas import tpu as pltpu

D_MODEL = 1024
N_META = 16
HEAD_DIM = 64
N_Q_HEADS = 16
N_KV_HEADS = 4
GQA_GROUP = N_Q_HEADS // N_KV_HEADS
WINDOW = 128
SSM_GROUP_CH = 16
SSM_GROUPS = D_MODEL // SSM_GROUP_CH
SSM_STATE = 64
D_FF = 2816
Q_W = N_Q_HEADS * HEAD_DIM
KV_W = N_KV_HEADS * HEAD_DIM
QKV_W = Q_W + 2 * KV_W
EPS = 1e-6
LOG2_E = 1.4426950408889634

LANES = 128
CHUNK = 16
TOEP = CHUNK * SSM_GROUP_CH
GROUPS_PER_STEP = 8
MXU_DIM = 256
FFN_EDGES = (0, 6 * MXU_DIM, D_FF)
ATTN_BLOCKS = 4
PROJ_PIECE = 256
TAIL_PIECE = 256
PROJ_U_SETS = 2
VMEM_LIMIT = 56 * 1024 * 1024

_F32 = jnp.float32
_BF16 = jnp.bfloat16
_NT = (((1,), (1,)), ((), ()))
_TN = (((0,), (0,)), ((), ()))
_HI = lax.Precision.HIGHEST


def _rms_rows(x, gain):
    return x * lax.rsqrt(jnp.mean(x * x, axis=-1, keepdims=True) + EPS) * gain


def _proj_t_stages(x_ref, w_ref, nm_ref, gain_ref, o_ref, norm_heads):
    tm_all = x_ref.shape[0]
    tm = min(tm_all, PROJ_PIECE)

    def piece(c):
        cols = slice(c * tm, (c + 1) * tm)
        xn = _rms_rows(x_ref[cols, :], nm_ref[...]).astype(_BF16)
        p = lax.dot_general(w_ref[...], xn, _NT, preferred_element_type=_F32)
        if norm_heads:
            rows = norm_heads * HEAD_DIM
            hd = p[:rows].reshape(norm_heads, HEAD_DIM, tm)
            ms = jnp.mean(hd * hd, axis=1, keepdims=True)
            hd = hd * lax.rsqrt(ms + EPS) * gain_ref[...].reshape(norm_heads, HEAD_DIM, 1)
            o_ref[:rows, cols] = hd.reshape(rows, tm).astype(o_ref.dtype)
            o_ref[rows:, cols] = p[rows:].astype(o_ref.dtype)
        else:
            o_ref[:, cols] = p.astype(o_ref.dtype)

    return [functools.partial(piece, c) for c in range(tm_all // tm)]


def _proj_t_body(x_ref, w_ref, nm_ref, gain_ref, o_ref, norm_heads):
    for stage in _proj_t_stages(x_ref, w_ref, nm_ref, gain_ref, o_ref, norm_heads):
        stage()


def _proj_t_kernel(x_ref, w_ref, nm_ref, gain_ref, o_ref, *, norm_heads):
    _proj_t_body(x_ref, w_ref, nm_ref, gain_ref, o_ref, norm_heads)


def _proj_t(x, w_t, norm_mix, gain, *, norm_heads, tm, name):
    b, s, d = x.shape
    n = w_t.shape[0]
    return pl.pallas_call(
        functools.partial(_proj_t_kernel, norm_heads=norm_heads),
        grid=(b, s // tm),
        in_specs=[
            pl.BlockSpec((None, tm, d), lambda i, j: (i, j, 0)),
            pl.BlockSpec((n, d), lambda i, j: (0, 0)),
            pl.BlockSpec((1, d), lambda i, j: (0, 0)),
            pl.BlockSpec(gain.shape, lambda i, j: (0, 0)),
        ],
        out_specs=pl.BlockSpec((None, n, tm), lambda i, j: (i, 0, j)),
        out_shape=jax.ShapeDtypeStruct((b, n, s), _BF16),
        compiler_params=pltpu.CompilerParams(
            dimension_semantics=("arbitrary", "arbitrary"), vmem_limit_bytes=VMEM_LIMIT),
        name=name,
    )(x, w_t, norm_mix, gain)


def _row_set_copy(hbm4, buf, sem, bb, jj, slot):
    return hbm4.at[bb, :, jj, :], buf.at[slot], sem.at[slot]


def _proj_u_kernel(x_hbm, w_ref, nm_ref, colp_ref, bre_ref, bim_ref, cre_ref, cim_ref, am_ref, d_ref,
                   o_ref, w1_ref, w3_ref, pwr_ref, pwi_ref, sm_ref, xbuf, sem, *, levels):
    nj = pl.num_programs(1)
    last = pl.num_programs(0) * nj - 1
    step = pl.program_id(0) * nj + pl.program_id(1)
    slot = step % 2
    nchunks = x_hbm.shape[1]

    def fetch(s, sl):
        return [pltpu.make_async_copy(x_hbm.at[s // nj, :, (s % nj) * PROJ_U_SETS + k, :],
                                      xbuf.at[sl, k * nchunks:(k + 1) * nchunks, :], sem.at[sl, k])
                for k in range(PROJ_U_SETS)]

    @pl.when(step == 0)
    def _():
        for c in fetch(step, slot):
            c.start()

    @pl.when(step < last)
    def _():
        for c in fetch(step + 1, 1 - slot):
            c.start()

    for c in fetch(step, slot):
        c.wait()
    proj = _proj_t_stages(xbuf.at[slot], w_ref, nm_ref, None, o_ref, 0)
    prep = _ssm_prep_stages(colp_ref, bre_ref, bim_ref, cre_ref, cim_ref, am_ref, d_ref,
                            w1_ref, w3_ref, pwr_ref, pwi_ref, sm_ref, levels)
    for stage in prep + proj:
        stage()


def _proj_u_perm_and_ssm_prep(x, w_t, norm_mix, colp, b_re, b_im, c_re, c_im, a_meta, d_grp, levels):
    b, s, d = x.shape
    n = w_t.shape[0]
    nchunks = s // CHUNK
    x4 = x.reshape(b, nchunks, CHUNK, d)
    nj = CHUNK // PROJ_U_SETS
    g, ns, gc = SSM_GROUPS, SSM_STATE, SSM_GROUP_CH
    gpp = g // (b * nj)
    assert gpp * b * nj == g
    gspec = lambda *tail: pl.BlockSpec((gpp,) + tail, lambda i, j: (i * nj + j,) + (0,) * len(tail))
    return pl.pallas_call(
        functools.partial(_proj_u_kernel, levels=levels),
        grid=(b, nj),
        in_specs=[
            pl.BlockSpec(memory_space=pl.ANY),
            pl.BlockSpec((n, d), lambda i, j: (0, 0)),
            pl.BlockSpec((1, d), lambda i, j: (0, 0)),
            gspec(ns, 4), gspec(ns, gc), gspec(ns, gc), gspec(gc, ns), gspec(gc, ns), gspec(1, TOEP),
            gspec(gc, 1),
        ],
        out_specs=[
            pl.BlockSpec((None, n, PROJ_U_SETS * nchunks), lambda i, j: (i, 0, j)),
            gspec(2 * ns, TOEP), gspec(TOEP, TOEP + 2 * ns), gspec(levels, ns, LANES),
            gspec(levels, ns, LANES), gspec(4, ns, LANES),
        ],
        out_shape=[jax.ShapeDtypeStruct((b, n, s), _BF16),
                   jax.ShapeDtypeStruct((g, 2 * ns, TOEP), _BF16),
                   jax.ShapeDtypeStruct((g, TOEP, TOEP + 2 * ns), _BF16),
                   jax.ShapeDtypeStruct((g, levels, ns, LANES), _F32),
                   jax.ShapeDtypeStruct((g, levels, ns, LANES), _F32),
                   jax.ShapeDtypeStruct((g, 4, ns, LANES), _F32)],
        scratch_shapes=[pltpu.VMEM((2, PROJ_U_SETS * nchunks, d), x.dtype),
                        pltpu.SemaphoreType.DMA((2, PROJ_U_SETS))],
        compiler_params=pltpu.CompilerParams(
            dimension_semantics=("arbitrary", "arbitrary"), vmem_limit_bytes=VMEM_LIMIT),
        name="proj_u_perm_ssm_prep",
    )(x4, w_t, norm_mix, colp, b_re, b_im, c_re, c_im, a_meta, d_grp)


def _ssm_prep_stages(colp_ref, bre_ref, bim_ref, cre_ref, cim_ref, am_ref, d_ref,
                     w1_ref, w3_ref, pwr_ref, pwi_ref, sm_ref, levels):
    ns, gc = SSM_STATE, SSM_GROUP_CH
    expand = lambda a, m: jnp.dot(a, m, precision=_HI, preferred_element_type=_F32)
    l16 = lax.broadcasted_iota(jnp.int32, (gc, TOEP), 1)
    r16 = lax.broadcasted_iota(jnp.int32, (gc, TOEP), 0)
    tile_c = (l16 % gc == r16).astype(_F32)
    tau_r = lax.broadcasted_iota(jnp.int32, (LANES, TOEP), 0)
    jc_l = lax.broadcasted_iota(jnp.int32, (LANES, TOEP), 1)
    pick_rev = (tau_r == CHUNK - 1 - jc_l // gc).astype(_F32)
    tc_r = lax.broadcasted_iota(jnp.int32, (TOEP, LANES), 0)
    tau_l = lax.broadcasted_iota(jnp.int32, (TOEP, LANES), 1)
    pick_next = (tau_l == tc_r // gc + 1).astype(_F32)
    lane = lax.broadcasted_iota(jnp.int32, (ns, LANES), 1)
    lvl = jnp.clip(lane - CHUNK, 0, levels - 1)
    expo = jnp.where(lane <= CHUNK, lane,
                     jnp.where(lane < CHUNK + levels, CHUNK * jnp.left_shift(1, lvl), 0)).astype(_F32)
    is_tau = lane <= CHUNK

    def group(gl):
        lr, li, dt = colp_ref[gl, :, 0:1], colp_ref[gl, :, 1:2], colp_ref[gl, :, 2:3]
        mag = jnp.exp(lr * dt * expo)
        ang = li * dt * expo
        pw_r, pw_i = mag * jnp.cos(ang), mag * jnp.sin(ang)
        ar, ai = pw_r[:, 1:2], pw_i[:, 1:2]
        den = lr * lr + li * li
        nr, ni = ar - 1.0, ai
        fr, fi = (nr * lr + ni * li) / den, (ni * lr - nr * li) / den
        b_re, b_im = bre_ref[gl], bim_ref[gl]
        bbr_t = expand(fr * b_re - fi * b_im, tile_c)
        bbi_t = expand(fr * b_im + fi * b_re, tile_c)
        tau_re, tau_im = jnp.where(is_tau, pw_r, 0.0), jnp.where(is_tau, pw_i, 0.0)
        pr, pi = expand(tau_re, pick_rev), expand(tau_im, pick_rev)
        xr = pr * bbr_t - pi * bbi_t
        xi = pr * bbi_t + pi * bbr_t
        krev = (jnp.dot(cre_ref[gl], xr, precision=_HI, preferred_element_type=_F32)
                - jnp.dot(cim_ref[gl], xi, precision=_HI, preferred_element_type=_F32))
        krev = krev + jnp.where(l16 - (CHUNK - 1) * gc == r16, d_ref[gl], 0.0)
        rz = jnp.concatenate([krev, jnp.zeros_like(krev)], axis=1)
        toep = jnp.concatenate(
            [rz[:, (CHUNK - 1 - t) * gc:(CHUNK - 1 - t) * gc + TOEP] for t in range(CHUNK)], axis=0)
        w1_ref[gl] = jnp.concatenate([xr, xi], axis=0).astype(w1_ref.dtype)
        gar = expand(pick_next, tau_re.T)
        gai = expand(pick_next, tau_im.T)
        crt = jnp.tile(cre_ref[gl], (CHUNK, 1))
        cit = jnp.tile(cim_ref[gl], (CHUNK, 1))
        w3_ref[gl] = jnp.concatenate([toep, crt * gar - cit * gai, -(crt * gai + cit * gar)],
                                     axis=1).astype(w3_ref.dtype)
        for l in range(levels):
            pwr_ref[gl, l] = jnp.broadcast_to(pw_r[:, CHUNK + l:CHUNK + l + 1], (ns, LANES))
            pwi_ref[gl, l] = jnp.broadcast_to(pw_i[:, CHUNK + l:CHUNK + l + 1], (ns, LANES))
        am = am_ref[gl]
        sr = jnp.sum(xr * am, axis=1, keepdims=True)
        si = jnp.sum(xi * am, axis=1, keepdims=True)
        a16r, a16i = pw_r[:, CHUNK:CHUNK + 1], pw_i[:, CHUNK:CHUNK + 1]
        sm_ref[gl, 0] = jnp.broadcast_to(sr, (ns, LANES))
        sm_ref[gl, 1] = jnp.broadcast_to(si, (ns, LANES))
        sm_ref[gl, 2] = jnp.broadcast_to(a16r * sr - a16i * si, (ns, LANES))
        sm_ref[gl, 3] = jnp.broadcast_to(a16r * si + a16i * sr, (ns, LANES))

    return [functools.partial(group, gl) for gl in range(colp_ref.shape[0])]


def _ssm_stages(u_ref, w1_ref, w3_ref, pwr_ref, pwi_ref, sm_ref, y_ref, bre_ref, bim_ref, nchunks, levels):
    ns, gc, gps = SSM_STATE, SSM_GROUP_CH, GROUPS_PER_STEP
    nblk = nchunks // LANES
    blocks = [slice(h * LANES, (h + 1) * LANES) for h in range(nblk)]
    group_rows = [slice(gl * ns, (gl + 1) * ns) for gl in range(gps)]

    def chunk_operand(gl):
        r0 = gl * gc
        return jnp.concatenate(
            [u_ref[r0:r0 + gc, j * nchunks:(j + 1) * nchunks] for j in range(CHUNK)], axis=0)

    def shifted(vals, sh, first):
        if sh % LANES == 0:
            k = sh // LANES
            return [None if h < k else vals[h - k] for h in range(nblk)]
        lane = lax.broadcasted_iota(jnp.int32, (ns, LANES), 1)
        rot = [pltpu.roll(v, sh, axis=1) for v in vals]
        keep = lane >= sh
        return [jnp.where(keep, rot[h], first if h == 0 else rot[h - 1]) for h in range(nblk)]

    def state_increments():
        lane0 = lax.broadcasted_iota(jnp.int32, (ns, LANES), 1) == 0
        for gl, rs in enumerate(group_rows):
            r1 = jnp.dot(w1_ref[gl], chunk_operand(gl), preferred_element_type=_F32)
            bre_ref[rs, :] = r1[:ns]
            bim_ref[rs, :] = r1[ns:]
            bre_ref[rs, blocks[0]] += jnp.where(lane0, sm_ref[gl, 2], 0.0)
            bim_ref[rs, blocks[0]] += jnp.where(lane0, sm_ref[gl, 3], 0.0)

    def scan_level(lvl):
        for gl, rs in enumerate(group_rows):
            p_re, p_im = pwr_ref[gl, lvl], pwi_ref[gl, lvl]
            s_re = [bre_ref[rs, blk] for blk in blocks]
            s_im = [bim_ref[rs, blk] for blk in blocks]
            t_re, t_im = shifted(s_re, 1 << lvl, 0.0), shifted(s_im, 1 << lvl, 0.0)
            for h in range(nblk):
                if t_re[h] is not None:
                    bre_ref[rs, blocks[h]] = s_re[h] + p_re * t_re[h] - p_im * t_im[h]
                    bim_ref[rs, blocks[h]] = s_im[h] + p_re * t_im[h] + p_im * t_re[h]

    def entering_states():
        for gl, rs in enumerate(group_rows):
            prev_re = shifted([bre_ref[rs, blk] for blk in blocks], 1, sm_ref[gl, 0])
            prev_im = shifted([bim_ref[rs, blk] for blk in blocks], 1, sm_ref[gl, 1])
            for h in range(nblk):
                bre_ref[rs, blocks[h]] = prev_re[h]
                bim_ref[rs, blocks[h]] = prev_im[h]

    def outputs(gl):
        r0, rs = gl * gc, group_rows[gl]
        rhs = jnp.concatenate([chunk_operand(gl), bre_ref[rs, :].astype(_BF16),
                               bim_ref[rs, :].astype(_BF16)], axis=0)
        y = jnp.dot(w3_ref[gl], rhs, preferred_element_type=_F32)
        for t in range(CHUNK):
            y_ref[r0:r0 + gc, t * nchunks:(t + 1) * nchunks] = y[t * gc:(t + 1) * gc, :]

    return ([state_increments] + [functools.partial(scan_level, lvl) for lvl in range(levels)]
            + [entering_states] + [functools.partial(outputs, gl) for gl in range(gps)])


def _ssm_proj_kernel(u_ref, w1_ref, w3_ref, pwr_ref, pwi_ref, sm_ref, x_ref, wq_ref, nm_ref, gain_ref,
                     y_ref, qkv_ref, bre_ref, bim_ref, *, nchunks, levels, norm_heads):
    ssm = _ssm_stages(u_ref, w1_ref, w3_ref, pwr_ref, pwi_ref, sm_ref, y_ref, bre_ref, bim_ref,
                      nchunks, levels)
    proj = _proj_t_stages(x_ref, wq_ref, nm_ref, gain_ref, qkv_ref, norm_heads)
    every = -(-len(ssm) // len(proj))
    for i, stage in enumerate(ssm):
        if i % every == 0 and proj:
            proj.pop(0)()
        stage()
    for stage in proj:
        stage()


def _ssm_core_and_proj_qkv(u_t, w1, w3, pwr, pwi, s_meta, x, wq_t, norm_mix, gain, levels, norm_heads):
    b, n, s = u_t.shape
    gps, ns = GROUPS_PER_STEP, SSM_STATE
    rows = gps * SSM_GROUP_CH
    nchunks = s // CHUNK
    nq = wq_t.shape[0]
    d = x.shape[2]
    tm = b * s // ((n // rows) * b)
    tiles = s // tm
    wspec = lambda *tail: pl.BlockSpec((gps,) + tail, lambda gb, i: (gb,) + (0,) * len(tail))
    const = lambda a: pl.BlockSpec(a.shape, lambda gb, i: (0,) * a.ndim)
    step = lambda gb, i: gb * b + i
    return pl.pallas_call(
        functools.partial(_ssm_proj_kernel, nchunks=nchunks, levels=levels, norm_heads=norm_heads),
        grid=(n // rows, b),
        in_specs=[
            pl.BlockSpec((None, rows, s), lambda gb, i: (i, gb, 0)),
            wspec(2 * ns, TOEP), wspec(TOEP, TOEP + 2 * ns),
            wspec(levels, ns, LANES), wspec(levels, ns, LANES), wspec(4, ns, LANES),
            pl.BlockSpec((None, tm, d), lambda gb, i: (step(gb, i) // tiles, step(gb, i) % tiles, 0)),
            const(wq_t), const(norm_mix), const(gain),
        ],
        out_specs=[
            pl.BlockSpec((None, rows, s), lambda gb, i: (i, gb, 0)),
            pl.BlockSpec((None, nq, tm), lambda gb, i: (step(gb, i) // tiles, 0, step(gb, i) % tiles)),
        ],
        out_shape=[jax.ShapeDtypeStruct((b, n, s), _F32), jax.ShapeDtypeStruct((b, nq, s), _BF16)],
        scratch_shapes=[pltpu.VMEM((gps * ns, nchunks), _F32), pltpu.VMEM((gps * ns, nchunks), _F32)],
        compiler_params=pltpu.CompilerParams(
            dimension_semantics=("arbitrary", "arbitrary"), vmem_limit_bytes=VMEM_LIMIT),
        name="ssm_core_proj_qkv",
    )(u_t, w1, w3, pwr, pwi, s_meta, x, wq_t, norm_mix, gain)


def _attn_kernel(q_ref, k_ref, v_ref, kp_ref, vp_ref, km_ref, vm_ref, sink_ref, tri_ref, o_ref, acc_ref):
    first_step = pl.program_id(1) == 0
    wq = GQA_GROUP * WINDOW
    kj = lax.broadcasted_iota(jnp.int32, (WINDOW, wq), 0)
    qi = lax.broadcasted_iota(jnp.int32, (WINDOW, wq), 1) % WINDOW
    in_cur = kj <= qi
    units = [(blk, h) for blk in range(ATTN_BLOCKS) for h in range(N_KV_HEADS)]

    def kv_rows(ref, prev_ref, blk, h):
        rows = slice(h * HEAD_DIM, (h + 1) * HEAD_DIM)
        cur = ref[rows, blk * WINDOW:(blk + 1) * WINDOW]
        prev = prev_ref[rows, :] if blk == 0 else ref[rows, (blk - 1) * WINDOW:blk * WINDOW]
        return cur, prev

    scores = []
    for blk, h in units:
        q4 = jnp.concatenate(
            [q_ref[(h * GQA_GROUP + r) * HEAD_DIM:(h * GQA_GROUP + r + 1) * HEAD_DIM,
                   blk * WINDOW:(blk + 1) * WINDOW] for r in range(GQA_GROUP)], axis=1)
        k_cur, k_prev = kv_rows(k_ref, kp_ref, blk, h)
        k_meta = km_ref[h * HEAD_DIM:(h + 1) * HEAD_DIM, :]
        scores.append(tuple(lax.dot_general(k, q4, _TN, preferred_element_type=_F32)
                            for k in (k_cur, k_prev, k_meta)))
    for (blk, h), (s_cur, s_prev, s_meta) in zip(units, scores):
        if blk == 0:
            s_prev = jnp.where(first_step, -jnp.inf, s_prev)
        s_sel = jnp.where(in_cur, s_cur, s_prev)
        sink = sink_ref[h]
        m = jnp.maximum(jnp.maximum(jnp.max(s_sel, axis=0, keepdims=True),
                                    jnp.max(s_meta, axis=0, keepdims=True)), sink)
        e_sel = jnp.exp2(s_sel - m)
        e_meta = jnp.exp2(s_meta - m)
        den = (jnp.sum(e_sel, axis=0, keepdims=True) + jnp.sum(e_meta, axis=0, keepdims=True)
               + jnp.exp2(sink - m))
        e_bf = e_sel.astype(_BF16)
        p_cur = e_bf * tri_ref[...]
        p_all = jnp.concatenate([p_cur, e_bf - p_cur, e_meta.astype(_BF16)], axis=0)
        v_cur, v_prev = kv_rows(v_ref, vp_ref, blk, h)
        v_all = jnp.concatenate([v_cur, v_prev, vm_ref[h * HEAD_DIM:(h + 1) * HEAD_DIM, :]], axis=1)
        o = jnp.dot(v_all, p_all, preferred_element_type=_F32) * (1.0 / den)
        for r in range(GQA_GROUP):
            hq = h * GQA_GROUP + r
            acc_ref[hq * HEAD_DIM:(hq + 1) * HEAD_DIM, blk * WINDOW:(blk + 1) * WINDOW] = (
                o[:, r * WINDOW:(r + 1) * WINDOW])
    o_ref[...] = acc_ref[...].T.astype(o_ref.dtype)


def _attention(qkv_t, qkv_meta_t, sink_rows):
    b, _, s = qkv_t.shape
    kj = lax.broadcasted_iota(jnp.int32, (WINDOW, GQA_GROUP * WINDOW), 0)
    qi = lax.broadcasted_iota(jnp.int32, (WINDOW, GQA_GROUP * WINDOW), 1) % WINDOW
    tri = (kj <= qi).astype(_BF16)
    tq = ATTN_BLOCKS * WINDOW
    kblk = Q_W // KV_W
    prev_blk = lambda n: jnp.maximum(n * ATTN_BLOCKS - 1, 0)
    return pl.pallas_call(
        _attn_kernel,
        grid=(b, s // tq),
        in_specs=[
            pl.BlockSpec((None, Q_W, tq), lambda i, n: (i, 0, n)),
            pl.BlockSpec((None, KV_W, tq), lambda i, n: (i, kblk, n)),
            pl.BlockSpec((None, KV_W, tq), lambda i, n: (i, kblk + 1, n)),
            pl.BlockSpec((None, KV_W, WINDOW), lambda i, n: (i, kblk, prev_blk(n))),
            pl.BlockSpec((None, KV_W, WINDOW), lambda i, n: (i, kblk + 1, prev_blk(n))),
            pl.BlockSpec((None, KV_W, N_META), lambda i, n: (0, kblk, 0)),
            pl.BlockSpec((None, KV_W, N_META), lambda i, n: (0, kblk + 1, 0)),
            pl.BlockSpec(sink_rows.shape, lambda i, n: (0, 0, 0)),
            pl.BlockSpec(tri.shape, lambda i, n: (0, 0)),
        ],
        out_specs=pl.BlockSpec((None, tq, Q_W), lambda i, n: (i, n, 0)),
        out_shape=jax.ShapeDtypeStruct((b, s, Q_W), _F32),
        scratch_shapes=[pltpu.VMEM((Q_W, tq), _F32)],
        compiler_params=pltpu.CompilerParams(
            dimension_semantics=("arbitrary", "arbitrary"), vmem_limit_bytes=VMEM_LIMIT),
        name="swa_attention",
    )(qkv_t, qkv_t, qkv_t, qkv_t, qkv_t, qkv_meta_t, qkv_meta_t, sink_rows, tri)


def _tail_kernel(x_hbm, attn_hbm, y_ref, nm_ref, wg_ref, wglu_ref, abn_ref, sbn_ref, wout_ref,
                 nf_ref, wfi_ref, wfo_ref, o_hbm, xbuf, abuf, obuf, sem_x, sem_a, sem_o):
    d = D_MODEL
    nj = pl.num_programs(1)
    last = pl.num_programs(0) * nj - 1
    step = pl.program_id(0) * nj + pl.program_id(1)
    slot = step % 2

    def fetch(s, sl):
        bb, jj = s // nj, s % nj
        return (pltpu.make_async_copy(*_row_set_copy(x_hbm, xbuf, sem_x, bb, jj, sl)),
                pltpu.make_async_copy(*_row_set_copy(attn_hbm, abuf, sem_a, bb, jj, sl)))

    def put(s, sl):
        dst, src, sem = _row_set_copy(o_hbm, obuf, sem_o, s // nj, s % nj, sl)
        return pltpu.make_async_copy(src, dst, sem)

    @pl.when(step == 0)
    def _():
        for c in fetch(step, slot):
            c.start()

    @pl.when(step < last)
    def _():
        for c in fetch(step + 1, 1 - slot):
            c.start()

    for c in fetch(step, slot):
        c.wait()

    @pl.when(step >= 2)
    def _():
        put(step - 2, slot).wait()

    nrows = xbuf.shape[1]
    pieces = [slice(r, r + TAIL_PIECE) for r in range(0, nrows, TAIL_PIECE)]
    xs =[xbuf[slot, p, :] for p in pieces]
    front = []
    for p, x in zip(pieces, xs):
        xn = _rms_rows(x, nm_ref[...]).astype(_BF16)
        gates = jnp.dot(xn, wg_ref[...], preferred_element_type=_F32)
        z = jax.nn.gelu(y_ref[:, p]).astype(_BF16)
        zz = lax.dot_general(z, wglu_ref[...], _TN, preferred_element_type=_F32)
        front.append((gates, zz))
    hs = []
    for p, x, (gates, zz) in zip(pieces, xs, front):
        ssm = zz[:, :d] * jax.nn.sigmoid(zz[:, d:])
        merged = (jax.nn.sigmoid(gates[:, :d]) * _rms_rows(abuf[slot, p, :], abn_ref[...])
                  + jax.nn.sigmoid(gates[:, d:]) * _rms_rows(ssm, sbn_ref[...]))
        h = x + jnp.dot(merged.astype(_BF16), wout_ref[...], preferred_element_type=_F32)
        hs.append((h, _rms_rows(h, nf_ref[...]).astype(_BF16)))
    outs = [h for h, _ in hs]
    for c0, c1 in zip(FFN_EDGES[:-1], FFN_EDGES[1:]):
        for i, (_, hn) in enumerate(hs):
            gate = jnp.dot(hn, wfi_ref[:, c0:c1], preferred_element_type=_F32)
            up = jnp.dot(hn, wfi_ref[:, D_FF + c0:D_FF + c1], preferred_element_type=_F32)
            act = (jax.nn.silu(gate) * up).astype(_BF16)
            outs[i] = outs[i] + jnp.dot(act, wfo_ref[c0:c1, :], preferred_element_type=_F32)
    for p, out in zip(pieces, outs):
        obuf[slot, p, :] = out
    put(step, slot).start()

    @pl.when(step == last)
    def _():
        put(step, slot).wait()

        @pl.when(step >= 1)
        def _():
            put(step - 1, 1 - slot).wait()


def _tail(x, attn, y_t, norm_mix, w_g, w_glu, abn, sbn, w_out, norm_ffn, w_fi, w_fo):
    b, s, d = x.shape
    nchunks = s // CHUNK
    x4 = x.reshape(b, nchunks, CHUNK, d)
    a4 = attn.reshape(b, nchunks, CHUNK, d)
    const = lambda a: pl.BlockSpec(a.shape, lambda i, j: (0,) * a.ndim, pipeline_mode=pl.Buffered(1))
    hbm = pl.BlockSpec(memory_space=pl.ANY)
    out = pl.pallas_call(
        _tail_kernel,
        grid=(b, CHUNK),
        in_specs=[
            hbm, hbm,
            pl.BlockSpec((None, d, nchunks), lambda i, j: (i, 0, j)),
            const(norm_mix), const(w_g), const(w_glu), const(abn), const(sbn), const(w_out),
            const(norm_ffn), const(w_fi), const(w_fo),
        ],
        out_specs=hbm,
        out_shape=jax.ShapeDtypeStruct(x4.shape, x.dtype),
        scratch_shapes=[pltpu.VMEM((2, nchunks, d), x.dtype), pltpu.VMEM((2, nchunks, d), attn.dtype),
                        pltpu.VMEM((2, nchunks, d), x.dtype),
                        pltpu.SemaphoreType.DMA((2,)), pltpu.SemaphoreType.DMA((2,)),
                        pltpu.SemaphoreType.DMA((2,))],
        compiler_params=pltpu.CompilerParams(
            dimension_semantics=("arbitrary", "arbitrary"), vmem_limit_bytes=VMEM_LIMIT),
        name="tail",
    )(x4, a4, y_t, norm_mix, w_g, w_glu, abn, sbn, w_out, norm_ffn, w_fi, w_fo)
    return out.reshape(b, s, d)


def kernel(x, meta_tokens, norm_mix, w_in, q_norm, k_norm, attn_sinks, lam_re, lam_im, log_dt,
           ssm_b_re, ssm_b_im, ssm_c_re, ssm_c_im, ssm_d, w_glu, attn_branch_norm, ssm_branch_norm,
           w_out, norm_ffn, w_ffn_in, w_ffn_out):
    seq = x.shape[1]
    levels = (seq // CHUNK).bit_length() - 1
    w = w_in[0]
    w_qkv_t = w[:, :QKV_W].T.astype(_BF16)
    w_u_t = w[:, QKV_W:QKV_W + D_MODEL].T.astype(_BF16)
    w_g = w[:, QKV_W + D_MODEL:].astype(_BF16)
    scale = HEAD_DIM ** -0.5 * LOG2_E
    qk_gain = jnp.concatenate([jnp.tile(q_norm[0] * scale, N_Q_HEADS),
                               jnp.tile(k_norm[0], N_KV_HEADS)])[:, None].astype(_F32)
    no_gain = jnp.zeros((8, 1), _F32)
    sink_rows = jnp.repeat((attn_sinks[0] * LOG2_E).reshape(N_KV_HEADS, 1, GQA_GROUP), WINDOW,
                           axis=2).astype(_F32)
    dt = jnp.exp(log_dt[0].astype(_F32))
    lr, li = lam_re[0].astype(_F32), lam_im[0].astype(_F32)
    dtb = jnp.broadcast_to(dt[:, None], lr.shape)
    colp = jnp.stack([lr, li, dtb, jnp.zeros_like(lr)], axis=2)
    d_grp = ssm_d[0].astype(_F32).reshape(SSM_GROUPS, SSM_GROUP_CH, 1)

    meta = meta_tokens.astype(_F32)[None]
    qkv_meta_t = _proj_t(meta, w_qkv_t, norm_mix, qk_gain, norm_heads=N_Q_HEADS + N_KV_HEADS,
                         tm=N_META, name="proj_qkv_meta")
    u_meta_t = _proj_t(meta, w_u_t, norm_mix, no_gain, norm_heads=0, tm=N_META, name="proj_u_meta")
    a_meta = (u_meta_t[0].astype(_F32).reshape(SSM_GROUPS, SSM_GROUP_CH, CHUNK)
              .transpose(0, 2, 1).reshape(SSM_GROUPS, 1, TOEP))

    u_t, w1, w3, pwr, pwi, s_meta = _proj_u_perm_and_ssm_prep(
        x, w_u_t, norm_mix, colp, ssm_b_re[0].astype(_F32), ssm_b_im[0].astype(_F32),
        ssm_c_re[0].astype(_F32), ssm_c_im[0].astype(_F32), a_meta, d_grp, levels)
    y_t, qkv_t = _ssm_core_and_proj_qkv(u_t, w1, w3, pwr, pwi, s_meta, x, w_qkv_t, norm_mix, qk_gain,
                                        levels, N_Q_HEADS + N_KV_HEADS)
    attn = _attention(qkv_t, qkv_meta_t, sink_rows)
    return _tail(x, attn, y_t, norm_mix, w_g, w_glu[0].astype(_BF16), attn_branch_norm,
                 ssm_branch_norm, w_out[0].astype(_BF16), norm_ffn, w_ffn_in[0].astype(_BF16),
                 w_ffn_out[0].astype(_BF16))
```

```python
import functools

import jax
import jax.numpy as jnp
from jax import lax
from jax.experimental import pallas as pl
from jax.experimental.pallas import tpu as pltpu

D_MODEL = 1024
N_META = 16
HEAD_DIM = 64
N_Q_HEADS = 16
N_KV_HEADS = 4
GQA_GROUP = N_Q_HEADS // N_KV_HEADS
WINDOW = 128
SSM_GROUP_CH = 16
SSM_GROUPS = D_MODEL // SSM_GROUP_CH
SSM_STATE = 64
D_FF = 2816
Q_W = N_Q_HEADS * HEAD_DIM
KV_W = N_KV_HEADS * HEAD_DIM
QKV_W = Q_W + 2 * KV_W
EPS = 1e-6
LOG2_E = 1.4426950408889634

LANES = 128
CHUNK = 16
TOEP = CHUNK * SSM_GROUP_CH
GROUPS_PER_STEP = 8
MXU_DIM = 256
FFN_EDGES = (0, 6 * MXU_DIM, D_FF)
ATTN_BLOCKS = 8
PROJ_PIECE = 256
TAIL_PIECE = 256
PROJ_U_SETS = 2
VMEM_LIMIT = 56 * 1024 * 1024

_F32 = jnp.float32
_BF16 = jnp.bfloat16
_NT = (((1,), (1,)), ((), ()))
_TN = (((0,), (0,)), ((), ()))
_HI = lax.Precision.HIGHEST


def _rms_rows(x, gain):
    return x * lax.rsqrt(jnp.mean(x * x, axis=-1, keepdims=True) + EPS) * gain


def _proj_t_stages(x_ref, w_ref, nm_ref, gain_ref, o_ref, norm_heads, wrow_ref=None, orow_ref=None):
    tm_all = x_ref.shape[0]
    tm = min(tm_all, PROJ_PIECE)

    def piece(c):
        cols = slice(c * tm, (c + 1) * tm)
        xn = _rms_rows(x_ref[cols, :], nm_ref[...]).astype(_BF16)
        if orow_ref is not None:
            orow_ref[cols, :] = jnp.dot(xn, wrow_ref[...], preferred_element_type=_F32).astype(orow_ref.dtype)
        p = lax.dot_general(w_ref[...], xn, _NT, preferred_element_type=_F32)
        if norm_heads:
            rows = norm_heads * HEAD_DIM
            hd = p[:rows].reshape(norm_heads, HEAD_DIM, tm)
            ms = jnp.mean(hd * hd, axis=1, keepdims=True)
            hd = hd * lax.rsqrt(ms + EPS) * gain_ref[...].reshape(norm_heads, HEAD_DIM, 1)
            o_ref[:rows, cols] = hd.reshape(rows, tm).astype(o_ref.dtype)
            o_ref[rows:, cols] = p[rows:].astype(o_ref.dtype)
        else:
            o_ref[:, cols] = p.astype(o_ref.dtype)

    return [functools.partial(piece, c) for c in range(tm_all // tm)]


def _proj_t_body(x_ref, w_ref, nm_ref, gain_ref, o_ref, norm_heads):
    for stage in _proj_t_stages(x_ref, w_ref, nm_ref, gain_ref, o_ref, norm_heads):
        stage()


def _proj_t_kernel(x_ref, w_ref, nm_ref, gain_ref, o_ref, *, norm_heads):
    _proj_t_body(x_ref, w_ref, nm_ref, gain_ref, o_ref, norm_heads)


def _proj_t(x, w_t, norm_mix, gain, *, norm_heads, tm, name):
    b, s, d = x.shape
    n = w_t.shape[0]
    return pl.pallas_call(
        functools.partial(_proj_t_kernel, norm_heads=norm_heads),
        grid=(b, s // tm),
        in_specs=[
            pl.BlockSpec((None, tm, d), lambda i, j: (i, j, 0)),
            pl.BlockSpec((n, d), lambda i, j: (0, 0)),
            pl.BlockSpec((1, d), lambda i, j: (0, 0)),
            pl.BlockSpec(gain.shape, lambda i, j: (0, 0)),
        ],
        out_specs=pl.BlockSpec((None, n, tm), lambda i, j: (i, 0, j)),
        out_shape=jax.ShapeDtypeStruct((b, n, s), _BF16),
        compiler_params=pltpu.CompilerParams(
            dimension_semantics=("arbitrary", "arbitrary"), vmem_limit_bytes=VMEM_LIMIT),
        name=name,
    )(x, w_t, norm_mix, gain)


def _row_set_copy(hbm4, buf, sem, bb, jj, slot):
    return hbm4.at[bb, :, jj, :], buf.at[slot], sem.at[slot]


def _proj_u_kernel(x_hbm, w_ref, wg_ref, nm_ref, colp_ref, bre_ref, bim_ref, cre_ref, cim_ref, am_ref, d_ref,
                   o_ref, g_ref, w1_ref, w3_ref, pwr_ref, pwi_ref, sm_ref, xbuf, sem, *, levels):
    nj = pl.num_programs(1)
    last = pl.num_programs(0) * nj - 1
    step = pl.program_id(0) * nj + pl.program_id(1)
    slot = step % 2
    nchunks = x_hbm.shape[1]

    def fetch(s, sl):
        return [pltpu.make_async_copy(x_hbm.at[s // nj, :, (s % nj) * PROJ_U_SETS + k, :],
                                      xbuf.at[sl, k * nchunks:(k + 1) * nchunks, :], sem.at[sl, k])
                for k in range(PROJ_U_SETS)]

    @pl.when(step == 0)
    def _():
        for c in fetch(step, slot):
            c.start()

    @pl.when(step < last)
    def _():
        for c in fetch(step + 1, 1 - slot):
            c.start()

    for c in fetch(step, slot):
        c.wait()
    proj = _proj_t_stages(xbuf.at[slot], w_ref, nm_ref, None, o_ref, 0, wg_ref, g_ref)
    prep =_ssm_prep_stages(colp_ref, bre_ref, bim_ref, cre_ref, cim_ref, am_ref, d_ref,
                            w1_ref, w3_ref, pwr_ref, pwi_ref, sm_ref, levels)
    for stage in proj[:1] + prep + proj[1:]:
        stage()


def _proj_u_perm_and_ssm_prep(x, w_t, w_gates, norm_mix, colp, b_re, b_im, c_re, c_im, a_meta, d_grp, levels):
    b, s, d = x.shape
    n = w_t.shape[0]
    nchunks = s // CHUNK
    x4 = x.reshape(b, nchunks, CHUNK, d)
    nj = CHUNK // PROJ_U_SETS
    g, ns, gc = SSM_GROUPS, SSM_STATE, SSM_GROUP_CH
    gpp = g // (b * nj)
    assert gpp * b * nj == g
    gspec = lambda *tail: pl.BlockSpec((gpp,) + tail, lambda i, j: (i * nj + j,) + (0,) * len(tail))
    return pl.pallas_call(
        functools.partial(_proj_u_kernel, levels=levels),
        grid=(b, nj),
        in_specs=[
            pl.BlockSpec(memory_space=pl.ANY),
            pl.BlockSpec((n, d), lambda i, j: (0, 0)),
            pl.BlockSpec(w_gates.shape, lambda i, j: (0, 0)),
            pl.BlockSpec((1, d), lambda i, j: (0, 0)),
            gspec(ns, 4), gspec(ns, gc), gspec(ns, gc), gspec(gc, ns), gspec(gc, ns), gspec(1, TOEP),
            gspec(gc, 1),
        ],
        out_specs=[
            pl.BlockSpec((None, n, PROJ_U_SETS * nchunks), lambda i, j: (i, 0, j)),
            pl.BlockSpec((None, PROJ_U_SETS * nchunks, w_gates.shape[1]), lambda i, j: (i, j, 0)),
            gspec(2 * ns, TOEP), gspec(TOEP, TOEP + 2 * ns), gspec(levels, ns, LANES),
            gspec(levels, ns, LANES), gspec(4, ns, LANES),
        ],
        out_shape=[jax.ShapeDtypeStruct((b, n, s), _BF16),
                   jax.ShapeDtypeStruct((b, s, w_gates.shape[1]), _BF16),
                   jax.ShapeDtypeStruct((g, 2 * ns, TOEP), _BF16),
                   jax.ShapeDtypeStruct((g, TOEP, TOEP + 2 * ns), _BF16),
                   jax.ShapeDtypeStruct((g, levels, ns, LANES), _F32),
                   jax.ShapeDtypeStruct((g, levels, ns, LANES), _F32),
                   jax.ShapeDtypeStruct((g, 4, ns, LANES), _F32)],
        scratch_shapes=[pltpu.VMEM((2, PROJ_U_SETS * nchunks, d), x.dtype),
                        pltpu.SemaphoreType.DMA((2, PROJ_U_SETS))],
        compiler_params=pltpu.CompilerParams(
            dimension_semantics=("arbitrary", "arbitrary"), vmem_limit_bytes=VMEM_LIMIT),
        name="proj_u_perm_ssm_prep",
    )(x4, w_t, w_gates, norm_mix, colp, b_re, b_im, c_re, c_im, a_meta, d_grp)


def _ssm_prep_stages(colp_ref, bre_ref, bim_ref, cre_ref, cim_ref, am_ref, d_ref,
                     w1_ref, w3_ref, pwr_ref, pwi_ref, sm_ref, levels):
    ns, gc = SSM_STATE, SSM_GROUP_CH

    def split3(a):
        hi = a.astype(_BF16)
        rest = a - hi.astype(_F32)
        mid = rest.astype(_BF16)
        return hi, mid, (rest - mid.astype(_F32)).astype(_BF16)

    pick_cols = lambda a, m3: jnp.dot(jnp.concatenate(split3(a), axis=1), m3, preferred_element_type=_F32)
    pick_rows = lambda m3, a: jnp.dot(m3, jnp.concatenate(split3(a), axis=0), preferred_element_type=_F32)
    l16 = lax.broadcasted_iota(jnp.int32, (gc, TOEP), 1)
    r16 = lax.broadcasted_iota(jnp.int32, (gc, TOEP), 0)
    rc = lax.broadcasted_iota(jnp.int32, (3 * gc, TOEP), 0)
    lc = lax.broadcasted_iota(jnp.int32, (3 * gc, TOEP), 1)
    tile_c3 = (lc % gc == rc % gc).astype(_BF16)
    tau_r = lax.broadcasted_iota(jnp.int32, (3 * LANES, TOEP), 0) % LANES
    jc_l = lax.broadcasted_iota(jnp.int32, (3 * LANES, TOEP), 1)
    pick_rev3 = (tau_r == CHUNK - 1 - jc_l // gc).astype(_BF16)
    tc_r = lax.broadcasted_iota(jnp.int32, (TOEP, 3 * LANES), 0)
    tau_l = lax.broadcasted_iota(jnp.int32, (TOEP, 3 * LANES), 1) % LANES
    pick_next3 = (tau_l == tc_r // gc + 1).astype(_BF16)
    lane = lax.broadcasted_iota(jnp.int32, (ns, LANES), 1)
    lvl = jnp.clip(lane - CHUNK, 0, levels - 1)
    expo = jnp.where(lane <= CHUNK, lane,
                     jnp.where(lane < CHUNK + levels, CHUNK * jnp.left_shift(1, lvl), 0)).astype(_F32)
    is_tau = lane <= CHUNK

    def group(gl):
        lr, li, dt = colp_ref[gl, :, 0:1], colp_ref[gl, :, 1:2], colp_ref[gl, :, 2:3]
        mag = jnp.exp(lr * dt * expo)
        ang = li * dt * expo
        pw_r, pw_i = mag * jnp.cos(ang), mag * jnp.sin(ang)
        ar, ai = pw_r[:, 1:2], pw_i[:, 1:2]
        den = lr * lr + li * li
        nr, ni = ar - 1.0, ai
        fr, fi = (nr * lr + ni * li) / den, (ni * lr - nr * li) / den
        b_re, b_im = bre_ref[gl], bim_ref[gl]
        bbr_t = pick_cols(fr * b_re - fi * b_im, tile_c3)
        bbi_t = pick_cols(fr * b_im + fi * b_re, tile_c3)
        tau_re, tau_im = jnp.where(is_tau, pw_r, 0.0), jnp.where(is_tau, pw_i, 0.0)
        pr, pi = pick_cols(tau_re, pick_rev3), pick_cols(tau_im, pick_rev3)
        xr = pr * bbr_t - pi * bbi_t
        xi = pr * bbi_t + pi * bbr_t
        krev = (jnp.dot(cre_ref[gl], xr, precision=_HI, preferred_element_type=_F32)
                - jnp.dot(cim_ref[gl], xi, precision=_HI, preferred_element_type=_F32))
        krev = krev + jnp.where(l16 - (CHUNK - 1) * gc == r16, d_ref[gl], 0.0)
        rz = jnp.concatenate([krev, jnp.zeros_like(krev)], axis=1)
        toep = jnp.concatenate(
            [rz[:, (CHUNK - 1 - t) * gc:(CHUNK - 1 - t) * gc + TOEP] for t in range(CHUNK)], axis=0)
        w1_ref[gl] = jnp.concatenate([xr, xi], axis=0).astype(w1_ref.dtype)
        gar = pick_rows(pick_next3, tau_re.T)
        gai = pick_rows(pick_next3, tau_im.T)
        crt = jnp.tile(cre_ref[gl], (CHUNK, 1))
        cit = jnp.tile(cim_ref[gl], (CHUNK, 1))
        w3_ref[gl] = jnp.concatenate([toep, crt * gar - cit * gai, -(crt * gai + cit * gar)],
                                     axis=1).astype(w3_ref.dtype)
        for l in range(levels):
            pwr_ref[gl, l] = jnp.broadcast_to(pw_r[:, CHUNK + l:CHUNK + l + 1], (ns, LANES))
            pwi_ref[gl, l] = jnp.broadcast_to(pw_i[:, CHUNK + l:CHUNK + l + 1], (ns, LANES))
        am = am_ref[gl]
        sr = jnp.sum(xr * am, axis=1, keepdims=True)
        si = jnp.sum(xi * am, axis=1, keepdims=True)
        a16r, a16i = pw_r[:, CHUNK:CHUNK + 1], pw_i[:, CHUNK:CHUNK + 1]
        sm_ref[gl, 0] = jnp.broadcast_to(sr, (ns, LANES))
        sm_ref[gl, 1] = jnp.broadcast_to(si, (ns, LANES))
        sm_ref[gl, 2] = jnp.broadcast_to(a16r * sr - a16i * si, (ns, LANES))
        sm_ref[gl, 3] = jnp.broadcast_to(a16r * si + a16i * sr, (ns, LANES))

    return [functools.partial(group, gl) for gl in range(colp_ref.shape[0])]


def _ssm_stages(u_ref, w1_ref, w3_ref, pwr_ref, pwi_ref, sm_ref, y_ref, bre_ref, bim_ref, nchunks, levels):
    ns, gc, gps = SSM_STATE, SSM_GROUP_CH, GROUPS_PER_STEP
    nblk = nchunks // LANES
    blocks = [slice(h * LANES, (h + 1) * LANES) for h in range(nblk)]
    group_rows = [slice(gl * ns, (gl + 1) * ns) for gl in range(gps)]

    def chunk_operand(gl):
        r0 = gl * gc
        return jnp.concatenate(
            [u_ref[r0:r0 + gc, j * nchunks:(j + 1) * nchunks] for j in range(CHUNK)], axis=0)

    def shifted(vals, sh, first):
        if sh % LANES == 0:
            k = sh // LANES
            return [None if h < k else vals[h - k] for h in range(nblk)]
        lane = lax.broadcasted_iota(jnp.int32, (ns, LANES), 1)
        rot = [pltpu.roll(v, sh, axis=1) for v in vals]
        keep = lane >= sh
        return [jnp.where(keep, rot[h], first if h == 0 else rot[h - 1]) for h in range(nblk)]

    def state_increments():
        lane0 = lax.broadcasted_iota(jnp.int32, (ns, LANES), 1) == 0
        for gl, rs in enumerate(group_rows):
            r1 = jnp.dot(w1_ref[gl], chunk_operand(gl), preferred_element_type=_F32)
            bre_ref[rs, :] = r1[:ns]
            bim_ref[rs, :] = r1[ns:]
            bre_ref[rs, blocks[0]] += jnp.where(lane0, sm_ref[gl, 2], 0.0)
            bim_ref[rs, blocks[0]] += jnp.where(lane0, sm_ref[gl, 3], 0.0)

    def scan_level(lvl):
        for gl, rs in enumerate(group_rows):
            p_re, p_im = pwr_ref[gl, lvl], pwi_ref[gl, lvl]
            s_re = [bre_ref[rs, blk] for blk in blocks]
            s_im = [bim_ref[rs, blk] for blk in blocks]
            t_re, t_im = shifted(s_re, 1 << lvl, 0.0), shifted(s_im, 1 << lvl, 0.0)
            for h in range(nblk):
                if t_re[h] is not None:
                    bre_ref[rs, blocks[h]] = s_re[h] + p_re * t_re[h] - p_im * t_im[h]
                    bim_ref[rs, blocks[h]] = s_im[h] + p_re * t_im[h] + p_im * t_re[h]

    def entering_states():
        for gl, rs in enumerate(group_rows):
            prev_re = shifted([bre_ref[rs, blk] for blk in blocks], 1, sm_ref[gl, 0])
            prev_im = shifted([bim_ref[rs, blk] for blk in blocks], 1, sm_ref[gl, 1])
            for h in range(nblk):
                bre_ref[rs, blocks[h]] = prev_re[h]
                bim_ref[rs, blocks[h]] = prev_im[h]

    def outputs(gl):
        r0, rs = gl * gc, group_rows[gl]
        rhs = jnp.concatenate([chunk_operand(gl), bre_ref[rs, :].astype(_BF16),
                               bim_ref[rs, :].astype(_BF16)], axis=0)
        y = jnp.dot(w3_ref[gl], rhs, preferred_element_type=_F32)
        for t in range(CHUNK):
            y_ref[r0:r0 + gc, t * nchunks:(t + 1) * nchunks] = y[t * gc:(t + 1) * gc, :]

    return ([state_increments] + [functools.partial(scan_level, lvl) for lvl in range(levels)]
            + [entering_states] + [functools.partial(outputs, gl) for gl in range(gps)])


def _ssm_proj_kernel(u_ref, w1_ref, w3_ref, pwr_ref, pwi_ref, sm_ref, x_ref, wq_ref, nm_ref, gain_ref,
                     y_ref, qkv_ref, bre_ref, bim_ref, *, nchunks, levels, norm_heads):
    ssm = _ssm_stages(u_ref, w1_ref, w3_ref, pwr_ref, pwi_ref, sm_ref, y_ref, bre_ref, bim_ref,
                      nchunks, levels)
    proj = _proj_t_stages(x_ref, wq_ref, nm_ref, gain_ref, qkv_ref, norm_heads)
    every = -(-len(ssm) // len(proj))
    for i, stage in enumerate(ssm):
        if i % every == 0 and proj:
            proj.pop(0)()
        stage()
    for stage in proj:
        stage()


def _ssm_core_and_proj_qkv(u_t, w1, w3, pwr, pwi, s_meta, x, wq_t, norm_mix, gain, levels, norm_heads):
    b, n, s = u_t.shape
    gps, ns = GROUPS_PER_STEP, SSM_STATE
    rows = gps * SSM_GROUP_CH
    nchunks = s // CHUNK
    nq = wq_t.shape[0]
    d = x.shape[2]
    tm = b * s // ((n // rows) * b)
    tiles = s // tm
    wspec = lambda *tail: pl.BlockSpec((gps,) + tail, lambda gb, i: (gb,) + (0,) * len(tail))
    const = lambda a: pl.BlockSpec(a.shape, lambda gb, i: (0,) * a.ndim)
    step = lambda gb, i: gb * b + i
    return pl.pallas_call(
        functools.partial(_ssm_proj_kernel, nchunks=nchunks, levels=levels, norm_heads=norm_heads),
        grid=(n // rows, b),
        in_specs=[
            pl.BlockSpec((None, rows, s), lambda gb, i: (i, gb, 0)),
            wspec(2 * ns, TOEP), wspec(TOEP, TOEP + 2 * ns),
            wspec(levels, ns, LANES), wspec(levels, ns, LANES), wspec(4, ns, LANES),
            pl.BlockSpec((None, tm, d), lambda gb, i: (step(gb, i) // tiles, step(gb, i) % tiles, 0)),
            const(wq_t), const(norm_mix), const(gain),
        ],
        out_specs=[
            pl.BlockSpec((None, rows, s), lambda gb, i: (i, gb, 0)),
            pl.BlockSpec((None, nq, tm), lambda gb, i: (step(gb, i) // tiles, 0, step(gb, i) % tiles)),
        ],
        out_shape=[jax.ShapeDtypeStruct((b, n, s), _F32), jax.ShapeDtypeStruct((b, nq, s), _BF16)],
        scratch_shapes=[pltpu.VMEM((gps * ns, nchunks), _F32), pltpu.VMEM((gps * ns, nchunks), _F32)],
        compiler_params=pltpu.CompilerParams(
            dimension_semantics=("arbitrary", "arbitrary"), vmem_limit_bytes=VMEM_LIMIT),
        name="ssm_core_proj_qkv",
    )(u_t, w1, w3, pwr, pwi, s_meta, x, wq_t, norm_mix, gain)


def _attn_kernel(q_ref, k_ref, v_ref, kp_ref, vp_ref, km_ref, vm_ref, sink_ref, tri_ref, o_ref, acc_ref):
    first_step = pl.program_id(1) == 0
    wq = GQA_GROUP * WINDOW
    kj = lax.broadcasted_iota(jnp.int32, (WINDOW, wq), 0)
    qi = lax.broadcasted_iota(jnp.int32, (WINDOW, wq), 1) % WINDOW
    in_cur = kj <= qi
    units = [(blk, h) for blk in range(ATTN_BLOCKS) for h in range(N_KV_HEADS)]

    def kv_rows(ref, prev_ref, blk, h):
        rows = slice(h * HEAD_DIM, (h + 1) * HEAD_DIM)
        cur = ref[rows, blk * WINDOW:(blk + 1) * WINDOW]
        prev = prev_ref[rows, :] if blk == 0 else ref[rows, (blk - 1) * WINDOW:blk * WINDOW]
        return cur, prev

    scores = []
    for blk, h in units:
        q4 = jnp.concatenate(
            [q_ref[(h * GQA_GROUP + r) * HEAD_DIM:(h * GQA_GROUP + r + 1) * HEAD_DIM,
                   blk * WINDOW:(blk + 1) * WINDOW] for r in range(GQA_GROUP)], axis=1)
        k_cur, k_prev = kv_rows(k_ref, kp_ref, blk, h)
        k_meta = km_ref[h * HEAD_DIM:(h + 1) * HEAD_DIM, :]
        scores.append(tuple(lax.dot_general(k, q4, _TN, preferred_element_type=_F32)
                            for k in (k_cur, k_prev, k_meta)))
    for (blk, h), (s_cur, s_prev, s_meta) in zip(units, scores):
        if blk == 0:
            s_prev = jnp.where(first_step, -jnp.inf, s_prev)
        s_sel = jnp.where(in_cur, s_cur, s_prev)
        sink = sink_ref[h]
        m = jnp.maximum(jnp.maximum(jnp.max(s_sel, axis=0, keepdims=True),
                                    jnp.max(s_meta, axis=0, keepdims=True)), sink)
        e_sel = jnp.exp2(s_sel - m)
        e_meta = jnp.exp2(s_meta - m)
        den = (jnp.sum(e_sel, axis=0, keepdims=True) + jnp.sum(e_meta, axis=0, keepdims=True)
               + jnp.exp2(sink - m))
        e_bf = e_sel.astype(_BF16)
        p_cur = e_bf * tri_ref[...]
        p_all = jnp.concatenate([p_cur, e_bf - p_cur, e_meta.astype(_BF16)], axis=0)
        v_cur, v_prev = kv_rows(v_ref, vp_ref, blk, h)
        v_all = jnp.concatenate([v_cur, v_prev, vm_ref[h * HEAD_DIM:(h + 1) * HEAD_DIM, :]], axis=1)
        o = jnp.dot(v_all, p_all, preferred_element_type=_F32) * (1.0 / den)
        for r in range(GQA_GROUP):
            hq = h * GQA_GROUP + r
            acc_ref[hq * HEAD_DIM:(hq + 1) * HEAD_DIM, blk * WINDOW:(blk + 1) * WINDOW] = (
                o[:, r * WINDOW:(r + 1) * WINDOW])
    o_ref[...] = acc_ref[...].T.astype(o_ref.dtype)


def _attention(qkv_t, qkv_meta_t, sink_rows):
    b, _, s = qkv_t.shape
    kj = lax.broadcasted_iota(jnp.int32, (WINDOW, GQA_GROUP * WINDOW), 0)
    qi = lax.broadcasted_iota(jnp.int32, (WINDOW, GQA_GROUP * WINDOW), 1) % WINDOW
    tri = (kj <= qi).astype(_BF16)
    tq = ATTN_BLOCKS * WINDOW
    kblk = Q_W // KV_W
    prev_blk = lambda n: jnp.maximum(n * ATTN_BLOCKS - 1, 0)
    return pl.pallas_call(
        _attn_kernel,
        grid=(b, s // tq),
        in_specs=[
            pl.BlockSpec((None, Q_W, tq), lambda i, n: (i, 0, n)),
            pl.BlockSpec((None, KV_W, tq), lambda i, n: (i, kblk, n)),
            pl.BlockSpec((None, KV_W, tq), lambda i, n: (i, kblk + 1, n)),
            pl.BlockSpec((None, KV_W, WINDOW), lambda i, n: (i, kblk, prev_blk(n))),
            pl.BlockSpec((None, KV_W, WINDOW), lambda i, n: (i, kblk + 1, prev_blk(n))),
            pl.BlockSpec((None, KV_W, N_META), lambda i, n: (0, kblk, 0)),
            pl.BlockSpec((None, KV_W, N_META), lambda i, n: (0, kblk + 1, 0)),
            pl.BlockSpec(sink_rows.shape, lambda i, n: (0, 0, 0)),
            pl.BlockSpec(tri.shape, lambda i, n: (0, 0)),
        ],
        out_specs=pl.BlockSpec((None, tq, Q_W), lambda i, n: (i, n, 0)),
        out_shape=jax.ShapeDtypeStruct((b, s, Q_W), _F32),
        scratch_shapes=[pltpu.VMEM((Q_W, tq), _F32)],
        compiler_params=pltpu.CompilerParams(
            dimension_semantics=("arbitrary", "arbitrary"), vmem_limit_bytes=VMEM_LIMIT),
        name="swa_attention",
    )(qkv_t, qkv_t, qkv_t, qkv_t, qkv_t, qkv_meta_t, qkv_meta_t, sink_rows, tri)


def _tail_kernel(x_hbm, attn_hbm, y_ref, g_ref, wglu_ref, abn_ref, sbn_ref, wout_ref,
                 nf_ref, wfi_ref, wfo_ref, o_hbm, xbuf, abuf, obuf, sem_x, sem_a, sem_o):
    d = D_MODEL
    nj = pl.num_programs(1)
    last = pl.num_programs(0) * nj - 1
    step = pl.program_id(0) * nj + pl.program_id(1)
    slot = step % 2

    def fetch(s, sl):
        bb, jj = s // nj, s % nj
        return (pltpu.make_async_copy(*_row_set_copy(x_hbm, xbuf, sem_x, bb, jj, sl)),
                pltpu.make_async_copy(*_row_set_copy(attn_hbm, abuf, sem_a, bb, jj, sl)))

    def put(s, sl):
        dst, src, sem = _row_set_copy(o_hbm, obuf, sem_o, s // nj, s % nj, sl)
        return pltpu.make_async_copy(src, dst, sem)

    @pl.when(step == 0)
    def _():
        for c in fetch(step, slot):
            c.start()

    @pl.when(step < last)
    def _():
        for c in fetch(step + 1, 1 - slot):
            c.start()

    for c in fetch(step, slot):
        c.wait()

    @pl.when(step >= 2)
    def _():
        put(step - 2, slot).wait()

    nrows = xbuf.shape[1]
    pieces = [slice(r, r + TAIL_PIECE) for r in range(0, nrows, TAIL_PIECE)]
    xs =[xbuf[slot, p, :] for p in pieces]
    front = []
    for p in pieces:
        z = jax.nn.gelu(y_ref[:, p]).astype(_BF16)
        front.append(lax.dot_general(z, wglu_ref[...], _TN, preferred_element_type=_F32))
    hs = []
    for p, x, zz in zip(pieces, xs, front):
        ssm = zz[:, :d] * jax.nn.sigmoid(zz[:, d:])
        merged = (jax.nn.sigmoid(g_ref[p, :d].astype(_F32)) * _rms_rows(abuf[slot, p, :], abn_ref[...])
                  + jax.nn.sigmoid(g_ref[p, d:].astype(_F32)) * _rms_rows(ssm, sbn_ref[...]))
        h = x + jnp.dot(merged.astype(_BF16), wout_ref[...], preferred_element_type=_F32)
        hs.append((h, _rms_rows(h, nf_ref[...]).astype(_BF16)))
    outs = [h for h, _ in hs]
    for c0, c1 in zip(FFN_EDGES[:-1], FFN_EDGES[1:]):
        for i, (_, hn) in enumerate(hs):
            gate = jnp.dot(hn, wfi_ref[:, c0:c1], preferred_element_type=_F32)
            up = jnp.dot(hn, wfi_ref[:, D_FF + c0:D_FF + c1], preferred_element_type=_F32)
            act = (jax.nn.silu(gate) * up).astype(_BF16)
            outs[i] = outs[i] + jnp.dot(act, wfo_ref[c0:c1, :], preferred_element_type=_F32)
    for p, out in zip(pieces, outs):
        obuf[slot, p, :] = out
    put(step, slot).start()

    @pl.when(step == last)
    def _():
        put(step, slot).wait()

        @pl.when(step >= 1)
        def _():
            put(step - 1, 1 - slot).wait()


def _tail(x, attn, y_t, gates, w_glu, abn, sbn, w_out, norm_ffn, w_fi, w_fo):
    b, s, d = x.shape
    nchunks = s // CHUNK
    x4 = x.reshape(b, nchunks, CHUNK, d)
    a4 = attn.reshape(b, nchunks, CHUNK, d)
    const = lambda a: pl.BlockSpec(a.shape, lambda i, j: (0,) * a.ndim, pipeline_mode=pl.Buffered(1))
    hbm = pl.BlockSpec(memory_space=pl.ANY)
    out = pl.pallas_call(
        _tail_kernel,
        grid=(b, CHUNK),
        in_specs=[
            hbm, hbm,
            pl.BlockSpec((None, d, nchunks), lambda i, j: (i, 0, j)),
            pl.BlockSpec((None, nchunks, gates.shape[2]), lambda i, j: (i, j, 0)),
            const(w_glu), const(abn), const(sbn), const(w_out),
            const(norm_ffn), const(w_fi), const(w_fo),
        ],
        out_specs=hbm,
        out_shape=jax.ShapeDtypeStruct(x4.shape, x.dtype),
        scratch_shapes=[pltpu.VMEM((2, nchunks, d), x.dtype), pltpu.VMEM((2, nchunks, d), attn.dtype),
                        pltpu.VMEM((2, nchunks, d), x.dtype),
                        pltpu.SemaphoreType.DMA((2,)), pltpu.SemaphoreType.DMA((2,)),
                        pltpu.SemaphoreType.DMA((2,))],
        compiler_params=pltpu.CompilerParams(
            dimension_semantics=("arbitrary", "arbitrary"), vmem_limit_bytes=VMEM_LIMIT),
        name="tail",
    )(x4, a4, y_t, gates, w_glu, abn, sbn, w_out, norm_ffn, w_fi, w_fo)
    return out.reshape(b, s, d)


def kernel(x, meta_tokens, norm_mix, w_in, q_norm, k_norm, attn_sinks, lam_re, lam_im, log_dt,
           ssm_b_re, ssm_b_im, ssm_c_re, ssm_c_im, ssm_d, w_glu, attn_branch_norm, ssm_branch_norm,
           w_out, norm_ffn, w_ffn_in, w_ffn_out):
    seq = x.shape[1]
    levels = (seq // CHUNK).bit_length() - 1
    w = w_in[0]
    w_qkv_t = w[:, :QKV_W].T.astype(_BF16)
    w_u_t = w[:, QKV_W:QKV_W + D_MODEL].T.astype(_BF16)
    w_g = w[:, QKV_W + D_MODEL:].astype(_BF16)
    scale = HEAD_DIM ** -0.5 * LOG2_E
    qk_gain = jnp.concatenate([jnp.tile(q_norm[0] * scale, N_Q_HEADS),
                               jnp.tile(k_norm[0], N_KV_HEADS)])[:, None].astype(_F32)
    no_gain = jnp.zeros((8, 1), _F32)
    sink_rows = jnp.repeat((attn_sinks[0] * LOG2_E).reshape(N_KV_HEADS, 1, GQA_GROUP), WINDOW,
                           axis=2).astype(_F32)
    dt = jnp.exp(log_dt[0].astype(_F32))
    lr, li = lam_re[0].astype(_F32), lam_im[0].astype(_F32)
    dtb = jnp.broadcast_to(dt[:, None], lr.shape)
    colp = jnp.stack([lr, li, dtb, jnp.zeros_like(lr)], axis=2)
    d_grp = ssm_d[0].astype(_F32).reshape(SSM_GROUPS, SSM_GROUP_CH, 1)

    meta = meta_tokens.astype(_F32)[None]
    qkv_meta_t = _proj_t(meta, w_qkv_t, norm_mix, qk_gain, norm_heads=N_Q_HEADS + N_KV_HEADS,
                         tm=N_META, name="proj_qkv_meta")
    u_meta_t = _proj_t(meta, w_u_t, norm_mix, no_gain, norm_heads=0, tm=N_META, name="proj_u_meta")
    a_meta = (u_meta_t[0].astype(_F32).reshape(SSM_GROUPS, SSM_GROUP_CH, CHUNK)
              .transpose(0, 2, 1).reshape(SSM_GROUPS, 1, TOEP))

    u_t, gates, w1, w3, pwr, pwi, s_meta = _proj_u_perm_and_ssm_prep(
        x, w_u_t, w_g, norm_mix, colp, ssm_b_re[0].astype(_F32), ssm_b_im[0].astype(_F32),
        ssm_c_re[0].astype(_F32), ssm_c_im[0].astype(_F32), a_meta, d_grp, levels)
    y_t, qkv_t = _ssm_core_and_proj_qkv(u_t, w1, w3, pwr, pwi, s_meta, x, w_qkv_t, norm_mix, qk_gain,
                                        levels, N_Q_HEADS + N_KV_HEADS)
    attn = _attention(qkv_t, qkv_meta_t, sink_rows)
    return _tail(x, attn, y_t, gates, w_glu[0].astype(_BF16), attn_branch_norm,
                 ssm_branch_norm, w_out[0].astype(_BF16), norm_ffn, w_ffn_in[0].astype(_BF16),
                 w_ffn_out[0].astype(_BF16))
```

```python
import functools

import jax
import jax.numpy as jnp
from jax import lax
from jax.experimental import pallas as pl
from jax.experimental.pallas import tpu as pltpu

D_MODEL = 1024
N_META = 16
HEAD_DIM = 64
N_Q_HEADS = 16
N_KV_HEADS = 4
GQA_GROUP = N_Q_HEADS // N_KV_HEADS
WINDOW = 128
SSM_GROUP_CH = 16
SSM_GROUPS = D_MODEL // SSM_GROUP_CH
SSM_STATE = 64
D_FF = 2816
Q_W = N_Q_HEADS * HEAD_DIM
KV_W = N_KV_HEADS * HEAD_DIM
QKV_W = Q_W + 2 * KV_W
EPS = 1e-6
LOG2_E = 1.4426950408889634

LANES = 128
CHUNK = 16
TOEP = CHUNK * SSM_GROUP_CH
GROUPS_PER_STEP = 8
MXU_DIM = 256
FFN_EDGES = (0, 6 * MXU_DIM, D_FF)
ATTN_BLOCKS = 8
PROJ_PIECE = 256
TAIL_PIECE = 256
PROJ_U_SETS = 2
VMEM_LIMIT = 56 * 1024 * 1024

_F32 = jnp.float32
_BF16 = jnp.bfloat16
_NT = (((1,), (1,)), ((), ()))
_TN = (((0,), (0,)), ((), ()))
_HI = lax.Precision.HIGHEST


def _rms_rows(x, gain):
    return x * lax.rsqrt(jnp.mean(x * x, axis=-1, keepdims=True) + EPS) * gain


def _proj_rows_stages(x_ref, w_ref, nm_ref, o_ref):
    tm = min(x_ref.shape[0], PROJ_PIECE)

    def piece(c):
        rows = slice(c * tm, (c + 1) * tm)
        xn = _rms_rows(x_ref[rows, :], nm_ref[...]).astype(_BF16)
        o_ref[rows, :] = jnp.dot(xn, w_ref[...], preferred_element_type=_F32).astype(o_ref.dtype)

    return [functools.partial(piece, c) for c in range(x_ref.shape[0] // tm)]


def _proj_t_stages(x_ref, w_ref, nm_ref, gain_ref, o_ref, norm_heads):
    tm_all = x_ref.shape[0]
    tm = min(tm_all, PROJ_PIECE)

    def piece(c):
        cols = slice(c * tm, (c + 1) * tm)
        xn = _rms_rows(x_ref[cols, :], nm_ref[...]).astype(_BF16)
        p = lax.dot_general(w_ref[...], xn, _NT, preferred_element_type=_F32)
        if norm_heads:
            rows = norm_heads * HEAD_DIM
            hd = p[:rows].reshape(norm_heads, HEAD_DIM, tm)
            ms = jnp.mean(hd * hd, axis=1, keepdims=True)
            hd = hd * lax.rsqrt(ms + EPS) * gain_ref[...].reshape(norm_heads, HEAD_DIM, 1)
            o_ref[:rows, cols] = hd.reshape(rows, tm).astype(o_ref.dtype)
            o_ref[rows:, cols] = p[rows:].astype(o_ref.dtype)
        else:
            o_ref[:, cols] = p.astype(o_ref.dtype)

    return [functools.partial(piece, c) for c in range(tm_all // tm)]


def _proj_t_body(x_ref, w_ref, nm_ref, gain_ref, o_ref, norm_heads):
    for stage in _proj_t_stages(x_ref, w_ref, nm_ref, gain_ref, o_ref, norm_heads):
        stage()


def _proj_t_kernel(x_ref, w_ref, nm_ref, gain_ref, o_ref, *, norm_heads):
    _proj_t_body(x_ref, w_ref, nm_ref, gain_ref, o_ref, norm_heads)


def _proj_t(x, w_t, norm_mix, gain, *, norm_heads, tm, name):
    b, s, d = x.shape
    n = w_t.shape[0]
    return pl.pallas_call(
        functools.partial(_proj_t_kernel, norm_heads=norm_heads),
        grid=(b, s // tm),
        in_specs=[
            pl.BlockSpec((None, tm, d), lambda i, j: (i, j, 0)),
            pl.BlockSpec((n, d), lambda i, j: (0, 0)),
            pl.BlockSpec((1, d), lambda i, j: (0, 0)),
            pl.BlockSpec(gain.shape, lambda i, j: (0, 0)),
        ],
        out_specs=pl.BlockSpec((None, n, tm), lambda i, j: (i, 0, j)),
        out_shape=jax.ShapeDtypeStruct((b, n, s), _BF16),
        compiler_params=pltpu.CompilerParams(
            dimension_semantics=("arbitrary", "arbitrary"), vmem_limit_bytes=VMEM_LIMIT),
        name=name,
    )(x, w_t, norm_mix, gain)


def _row_set_copy(hbm4, buf, sem, bb, jj, slot):
    return hbm4.at[bb, :, jj, :], buf.at[slot], sem.at[slot]


def _fetch_row_sets(x_hbm, xbuf, sem):
    nj = CHUNK // PROJ_U_SETS
    last = pl.num_programs(0) * pl.num_programs(1) - 1
    step = pl.program_id(0) * pl.num_programs(1) + pl.program_id(1)
    slot = step % 2
    nchunks = x_hbm.shape[1]

    def fetch(s, sl):
        return [pltpu.make_async_copy(x_hbm.at[s // nj, :, (s % nj) * PROJ_U_SETS + k, :],
                                      xbuf.at[sl, k * nchunks:(k + 1) * nchunks, :], sem.at[sl, k])
                for k in range(PROJ_U_SETS)]

    @pl.when(step == 0)
    def _():
        for c in fetch(step, slot):
            c.start()

    @pl.when(step < last)
    def _():
        for c in fetch(step + 1, 1 - slot):
            c.start()

    for c in fetch(step, slot):
        c.wait()
    return slot


def _proj_kernel(x_hbm, xnat_ref, wu_ref, wq_ref, nm_ref, gain_ref,
                 colp_ref, bre_ref, bim_ref, cre_ref, cim_ref, am_ref, d_ref,
                 u_ref, qkv_ref, w1_ref, w3_ref, pwr_ref, pwi_ref, sm_ref, xbuf, sem, *, levels, norm_heads):
    slot = _fetch_row_sets(x_hbm, xbuf, sem)
    proj_u = _proj_t_stages(xbuf.at[slot], wu_ref, nm_ref, None, u_ref, 0)
    proj_q = _proj_t_stages(xnat_ref, wq_ref, nm_ref, gain_ref, qkv_ref, norm_heads)
    prep = _ssm_prep_stages(colp_ref, bre_ref, bim_ref, cre_ref, cim_ref, am_ref, d_ref,
                            w1_ref, w3_ref, pwr_ref, pwi_ref, sm_ref, levels)
    mixed = [st for pair in zip(proj_u, proj_q) for st in pair]
    for stage in mixed[:1] + prep + mixed[1:]:
        stage()


def _projections_and_ssm_prep(x, wu_t, wq_t, norm_mix, qk_gain, norm_heads, colp, b_re, b_im, c_re, c_im,
                              a_meta, d_grp, levels):
    b, s, d = x.shape
    nu, nq = wu_t.shape[0], wq_t.shape[0]
    nchunks = s // CHUNK
    x4 = x.reshape(b, nchunks, CHUNK, d)
    nj = CHUNK // PROJ_U_SETS
    tm = s // nj
    g, ns, gc = SSM_GROUPS, SSM_STATE, SSM_GROUP_CH
    gpp = g // (b * nj)
    assert gpp * b * nj == g
    gspec = lambda *tail: pl.BlockSpec((gpp,) + tail, lambda i, j: (i * nj + j,) + (0,) * len(tail))
    const = lambda a: pl.BlockSpec(a.shape, lambda i, j: (0,) * a.ndim)
    return pl.pallas_call(
        functools.partial(_proj_kernel, levels=levels, norm_heads=norm_heads),
        grid=(b, nj),
        in_specs=[
            pl.BlockSpec(memory_space=pl.ANY),
            pl.BlockSpec((None, tm, d), lambda i, j: (i, j, 0)),
            const(wu_t), const(wq_t), const(norm_mix), const(qk_gain),
            gspec(ns, 4), gspec(ns, gc), gspec(ns, gc), gspec(gc, ns), gspec(gc, ns), gspec(1, TOEP),
            gspec(gc, 1),
        ],
        out_specs=[
            pl.BlockSpec((None, nu, PROJ_U_SETS * nchunks), lambda i, j: (i, 0, j)),
            pl.BlockSpec((None, nq, tm), lambda i, j: (i, 0, j)),
            gspec(2 * ns, TOEP), gspec(TOEP, TOEP + 2 * ns), gspec(levels, ns, LANES),
            gspec(levels, ns, LANES), gspec(4, ns, LANES),
        ],
        out_shape=[jax.ShapeDtypeStruct((b, nu, s), _BF16),
                   jax.ShapeDtypeStruct((b, nq, s), _BF16),
                   jax.ShapeDtypeStruct((g, 2 * ns, TOEP), _BF16),
                   jax.ShapeDtypeStruct((g, TOEP, TOEP + 2 * ns), _BF16),
                   jax.ShapeDtypeStruct((g, levels, ns, LANES), _F32),
                   jax.ShapeDtypeStruct((g, levels, ns, LANES), _F32),
                   jax.ShapeDtypeStruct((g, 4, ns, LANES), _F32)],
        scratch_shapes=[pltpu.VMEM((2, PROJ_U_SETS * nchunks, d), x.dtype),
                        pltpu.SemaphoreType.DMA((2, PROJ_U_SETS))],
        compiler_params=pltpu.CompilerParams(
            dimension_semantics=("arbitrary", "arbitrary"), vmem_limit_bytes=VMEM_LIMIT),
        name="projections_ssm_prep",
    )(x4, x, wu_t, wq_t, norm_mix, qk_gain, colp, b_re, b_im, c_re, c_im, a_meta, d_grp)


def _ssm_prep_stages(colp_ref, bre_ref, bim_ref, cre_ref, cim_ref, am_ref, d_ref,
                     w1_ref, w3_ref, pwr_ref, pwi_ref, sm_ref, levels):
    ns, gc = SSM_STATE, SSM_GROUP_CH

    def split3(a):
        hi = a.astype(_BF16)
        rest = a - hi.astype(_F32)
        mid = rest.astype(_BF16)
        return hi, mid, (rest - mid.astype(_F32)).astype(_BF16)

    pick_cols = lambda a, m3: jnp.dot(jnp.concatenate(split3(a), axis=1), m3, preferred_element_type=_F32)
    pick_rows = lambda m3, a: jnp.dot(m3, jnp.concatenate(split3(a), axis=0), preferred_element_type=_F32)
    l16 = lax.broadcasted_iota(jnp.int32, (gc, TOEP), 1)
    r16 = lax.broadcasted_iota(jnp.int32, (gc, TOEP), 0)
    rc = lax.broadcasted_iota(jnp.int32, (3 * gc, TOEP), 0)
    lc = lax.broadcasted_iota(jnp.int32, (3 * gc, TOEP), 1)
    tile_c3 = (lc % gc == rc % gc).astype(_BF16)
    tau_r = lax.broadcasted_iota(jnp.int32, (3 * LANES, TOEP), 0) % LANES
    jc_l = lax.broadcasted_iota(jnp.int32, (3 * LANES, TOEP), 1)
    pick_rev3 = (tau_r == CHUNK - 1 - jc_l // gc).astype(_BF16)
    tc_r = lax.broadcasted_iota(jnp.int32, (TOEP, 3 * LANES), 0)
    tau_l = lax.broadcasted_iota(jnp.int32, (TOEP, 3 * LANES), 1) % LANES
    pick_next3 = (tau_l == tc_r // gc + 1).astype(_BF16)
    lane = lax.broadcasted_iota(jnp.int32, (ns, LANES), 1)
    lvl = jnp.clip(lane - CHUNK, 0, levels - 1)
    expo = jnp.where(lane <= CHUNK, lane,
                     jnp.where(lane < CHUNK + levels, CHUNK * jnp.left_shift(1, lvl), 0)).astype(_F32)
    is_tau = lane <= CHUNK

    def group(gl):
        lr, li, dt = colp_ref[gl, :, 0:1], colp_ref[gl, :, 1:2], colp_ref[gl, :, 2:3]
        mag = jnp.exp(lr * dt * expo)
        ang = li * dt * expo
        pw_r, pw_i = mag * jnp.cos(ang), mag * jnp.sin(ang)
        ar, ai = pw_r[:, 1:2], pw_i[:, 1:2]
        den = lr * lr + li * li
        nr, ni = ar - 1.0, ai
        fr, fi = (nr * lr + ni * li) / den, (ni * lr - nr * li) / den
        b_re, b_im = bre_ref[gl], bim_ref[gl]
        bbr_t = pick_cols(fr * b_re - fi * b_im, tile_c3)
        bbi_t = pick_cols(fr * b_im + fi * b_re, tile_c3)
        tau_re, tau_im = jnp.where(is_tau, pw_r, 0.0), jnp.where(is_tau, pw_i, 0.0)
        pr, pi = pick_cols(tau_re, pick_rev3), pick_cols(tau_im, pick_rev3)
        xr = pr * bbr_t - pi * bbi_t
        xi = pr * bbi_t + pi * bbr_t
        krev = (jnp.dot(cre_ref[gl], xr, precision=_HI, preferred_element_type=_F32)
                - jnp.dot(cim_ref[gl], xi, precision=_HI, preferred_element_type=_F32))
        krev = krev + jnp.where(l16 - (CHUNK - 1) * gc == r16, d_ref[gl], 0.0)
        rz = jnp.concatenate([krev, jnp.zeros_like(krev)], axis=1)
        toep = jnp.concatenate(
            [rz[:, (CHUNK - 1 - t) * gc:(CHUNK - 1 - t) * gc + TOEP] for t in range(CHUNK)], axis=0)
        w1_ref[gl] = jnp.concatenate([xr, xi], axis=0).astype(w1_ref.dtype)
        gar = pick_rows(pick_next3, tau_re.T)
        gai = pick_rows(pick_next3, tau_im.T)
        crt = jnp.tile(cre_ref[gl], (CHUNK, 1))
        cit = jnp.tile(cim_ref[gl], (CHUNK, 1))
        w3_ref[gl] = jnp.concatenate([toep, crt * gar - cit * gai, -(crt * gai + cit * gar)],
                                     axis=1).astype(w3_ref.dtype)
        for l in range(levels):
            pwr_ref[gl, l] = jnp.broadcast_to(pw_r[:, CHUNK + l:CHUNK + l + 1], (ns, LANES))
            pwi_ref[gl, l] = jnp.broadcast_to(pw_i[:, CHUNK + l:CHUNK + l + 1], (ns, LANES))
        am = am_ref[gl]
        sr = jnp.sum(xr * am, axis=1, keepdims=True)
        si = jnp.sum(xi * am, axis=1, keepdims=True)
        a16r, a16i = pw_r[:, CHUNK:CHUNK + 1], pw_i[:, CHUNK:CHUNK + 1]
        sm_ref[gl, 0] = jnp.broadcast_to(sr, (ns, LANES))
        sm_ref[gl, 1] = jnp.broadcast_to(si, (ns, LANES))
        sm_ref[gl, 2] = jnp.broadcast_to(a16r * sr - a16i * si, (ns, LANES))
        sm_ref[gl, 3] = jnp.broadcast_to(a16r * si + a16i * sr, (ns, LANES))

    return [functools.partial(group, gl) for gl in range(colp_ref.shape[0])]


def _ssm_stages(u_ref, w1_ref, w3_ref, pwr_ref, pwi_ref, sm_ref, y_ref, bre_ref, bim_ref, nchunks, levels):
    ns, gc, gps = SSM_STATE, SSM_GROUP_CH, GROUPS_PER_STEP
    nblk = nchunks // LANES
    blocks = [slice(h * LANES, (h + 1) * LANES) for h in range(nblk)]
    group_rows = [slice(gl * ns, (gl + 1) * ns) for gl in range(gps)]

    def chunk_operand(gl):
        r0 = gl * gc
        return jnp.concatenate(
            [u_ref[r0:r0 + gc, j * nchunks:(j + 1) * nchunks] for j in range(CHUNK)], axis=0)

    def shifted(vals, sh, first):
        if sh % LANES == 0:
            k = sh // LANES
            return [None if h < k else vals[h - k] for h in range(nblk)]
        lane = lax.broadcasted_iota(jnp.int32, (ns, LANES), 1)
        rot = [pltpu.roll(v, sh, axis=1) for v in vals]
        keep = lane >= sh
        return [jnp.where(keep, rot[h], first if h == 0 else rot[h - 1]) for h in range(nblk)]

    def state_increments():
        lane0 = lax.broadcasted_iota(jnp.int32, (ns, LANES), 1) == 0
        for gl, rs in enumerate(group_rows):
            r1 = jnp.dot(w1_ref[gl], chunk_operand(gl), preferred_element_type=_F32)
            bre_ref[rs, :] = r1[:ns]
            bim_ref[rs, :] = r1[ns:]
            bre_ref[rs, blocks[0]] += jnp.where(lane0, sm_ref[gl, 2], 0.0)
            bim_ref[rs, blocks[0]] += jnp.where(lane0, sm_ref[gl, 3], 0.0)

    def scan_level(lvl):
        for gl, rs in enumerate(group_rows):
            p_re, p_im = pwr_ref[gl, lvl], pwi_ref[gl, lvl]
            s_re = [bre_ref[rs, blk] for blk in blocks]
            s_im = [bim_ref[rs, blk] for blk in blocks]
            t_re, t_im = shifted(s_re, 1 << lvl, 0.0), shifted(s_im, 1 << lvl, 0.0)
            for h in range(nblk):
                if t_re[h] is not None:
                    bre_ref[rs, blocks[h]] = s_re[h] + p_re * t_re[h] - p_im * t_im[h]
                    bim_ref[rs, blocks[h]] = s_im[h] + p_re * t_im[h] + p_im * t_re[h]

    def entering_states():
        for gl, rs in enumerate(group_rows):
            prev_re = shifted([bre_ref[rs, blk] for blk in blocks], 1, sm_ref[gl, 0])
            prev_im = shifted([bim_ref[rs, blk] for blk in blocks], 1, sm_ref[gl, 1])
            for h in range(nblk):
                bre_ref[rs, blocks[h]] = prev_re[h]
                bim_ref[rs, blocks[h]] = prev_im[h]

    def outputs(gl):
        r0, rs = gl * gc, group_rows[gl]
        rhs = jnp.concatenate([chunk_operand(gl), bre_ref[rs, :].astype(_BF16),
                               bim_ref[rs, :].astype(_BF16)], axis=0)
        y = jnp.dot(w3_ref[gl], rhs, preferred_element_type=_F32)
        for t in range(CHUNK):
            y_ref[r0:r0 + gc, t * nchunks:(t + 1) * nchunks] = y[t * gc:(t + 1) * gc, :]

    return ([state_increments] + [functools.partial(scan_level, lvl) for lvl in range(levels)]
            + [entering_states] + [functools.partial(outputs, gl) for gl in range(gps)])


def _ssm_gates_kernel(u_ref, w1_ref, w3_ref, pwr_ref, pwi_ref, sm_ref, x_hbm, wg_ref, nm_ref,
                      y_ref, g_ref, bre_ref, bim_ref, xbuf, sem, *, nchunks, levels):
    slot = _fetch_row_sets(x_hbm, xbuf, sem)
    ssm = _ssm_stages(u_ref, w1_ref, w3_ref, pwr_ref, pwi_ref, sm_ref, y_ref, bre_ref, bim_ref,
                      nchunks, levels)
    proj = _proj_rows_stages(xbuf.at[slot], wg_ref, nm_ref, g_ref)
    every = -(-len(ssm) // len(proj))
    for i, stage in enumerate(ssm):
        if i % every == 0 and proj:
            proj.pop(0)()
        stage()
    for stage in proj:
        stage()


def _ssm_core_and_gates(u_t, w1, w3, pwr, pwi, s_meta, x, w_gates, norm_mix, levels):
    b, n, s = u_t.shape
    gps, ns = GROUPS_PER_STEP, SSM_STATE
    rows = gps * SSM_GROUP_CH
    nchunks = s // CHUNK
    d = x.shape[2]
    x4 = x.reshape(b, nchunks, CHUNK, d)
    nj = CHUNK // PROJ_U_SETS
    set_rows = PROJ_U_SETS * nchunks
    assert (n // rows) * b == b * nj
    wspec = lambda *tail: pl.BlockSpec((gps,) + tail, lambda gb, i: (gb,) + (0,) * len(tail))
    step = lambda gb, i: gb * b + i
    return pl.pallas_call(
        functools.partial(_ssm_gates_kernel, nchunks=nchunks, levels=levels),
        grid=(n // rows, b),
        in_specs=[
            pl.BlockSpec((None, rows, s), lambda gb, i: (i, gb, 0)),
            wspec(2 * ns, TOEP), wspec(TOEP, TOEP + 2 * ns),
            wspec(levels, ns, LANES), wspec(levels, ns, LANES), wspec(4, ns, LANES),
            pl.BlockSpec(memory_space=pl.ANY),
            pl.BlockSpec(w_gates.shape, lambda gb, i: (0, 0), pipeline_mode=pl.Buffered(1)),
            pl.BlockSpec(norm_mix.shape, lambda gb, i: (0, 0)),
        ],
        out_specs=[
            pl.BlockSpec((None, rows, s), lambda gb, i: (i, gb, 0)),
            pl.BlockSpec((None, set_rows, w_gates.shape[1]),
                         lambda gb, i: (step(gb, i) // nj, step(gb, i) % nj, 0)),
        ],
        out_shape=[jax.ShapeDtypeStruct((b, n, s), _F32),
                   jax.ShapeDtypeStruct((b, s, w_gates.shape[1]), _BF16)],
        scratch_shapes=[pltpu.VMEM((gps * ns, nchunks), _F32), pltpu.VMEM((gps * ns, nchunks), _F32),
                        pltpu.VMEM((2, set_rows, d), x.dtype), pltpu.SemaphoreType.DMA((2, PROJ_U_SETS))],
        compiler_params=pltpu.CompilerParams(
            dimension_semantics=("arbitrary", "arbitrary"), vmem_limit_bytes=VMEM_LIMIT),
        name="ssm_core_gates",
    )(u_t, w1, w3, pwr, pwi, s_meta, x4, w_gates, norm_mix)


def _attn_kernel(q_ref, k_ref, v_ref, kp_ref, vp_ref, km_ref, vm_ref, sink_ref, tri_ref, o_ref, acc_ref):
    first_step = pl.program_id(1) == 0
    wq = GQA_GROUP * WINDOW
    kj = lax.broadcasted_iota(jnp.int32, (WINDOW, wq), 0)
    qi = lax.broadcasted_iota(jnp.int32, (WINDOW, wq), 1) % WINDOW
    in_cur = kj <= qi
    units = [(blk, h) for blk in range(ATTN_BLOCKS) for h in range(N_KV_HEADS)]

    def kv_rows(ref, prev_ref, blk, h):
        rows = slice(h * HEAD_DIM, (h + 1) * HEAD_DIM)
        cur = ref[rows, blk * WINDOW:(blk + 1) * WINDOW]
        prev = prev_ref[rows, :] if blk == 0 else ref[rows, (blk - 1) * WINDOW:blk * WINDOW]
        return cur, prev

    scores = []
    for blk, h in units:
        q4 = jnp.concatenate(
            [q_ref[(h * GQA_GROUP + r) * HEAD_DIM:(h * GQA_GROUP + r + 1) * HEAD_DIM,
                   blk * WINDOW:(blk + 1) * WINDOW] for r in range(GQA_GROUP)], axis=1)
        k_cur, k_prev = kv_rows(k_ref, kp_ref, blk, h)
        k_meta = km_ref[h * HEAD_DIM:(h + 1) * HEAD_DIM, :]
        scores.append(tuple(lax.dot_general(k, q4, _TN, preferred_element_type=_F32)
                            for k in (k_cur, k_prev, k_meta)))
    for (blk, h), (s_cur, s_prev, s_meta) in zip(units, scores):
        if blk == 0:
            s_prev = jnp.where(first_step, -jnp.inf, s_prev)
        s_sel = jnp.where(in_cur, s_cur, s_prev)
        sink = sink_ref[h]
        m = jnp.maximum(jnp.maximum(jnp.max(s_sel, axis=0, keepdims=True),
                                    jnp.max(s_meta, axis=0, keepdims=True)), sink)
        e_sel = jnp.exp2(s_sel - m)
        e_meta = jnp.exp2(s_meta - m)
        den = (jnp.sum(e_sel, axis=0, keepdims=True) + jnp.sum(e_meta, axis=0, keepdims=True)
               + jnp.exp2(sink - m))
        e_bf = e_sel.astype(_BF16)
        p_cur = e_bf * tri_ref[...]
        p_all = jnp.concatenate([p_cur, e_bf - p_cur, e_meta.astype(_BF16)], axis=0)
        v_cur, v_prev = kv_rows(v_ref, vp_ref, blk, h)
        v_all = jnp.concatenate([v_cur, v_prev, vm_ref[h * HEAD_DIM:(h + 1) * HEAD_DIM, :]], axis=1)
        o = jnp.dot(v_all, p_all, preferred_element_type=_F32) * (1.0 / den)
        for r in range(GQA_GROUP):
            hq = h * GQA_GROUP + r
            acc_ref[hq * HEAD_DIM:(hq + 1) * HEAD_DIM, blk * WINDOW:(blk + 1) * WINDOW] = (
                o[:, r * WINDOW:(r + 1) * WINDOW])
    o_ref[...] = acc_ref[...].T.astype(o_ref.dtype)


def _attention(qkv_t, qkv_meta_t, sink_rows):
    b, _, s = qkv_t.shape
    kj = lax.broadcasted_iota(jnp.int32, (WINDOW, GQA_GROUP * WINDOW), 0)
    qi = lax.broadcasted_iota(jnp.int32, (WINDOW, GQA_GROUP * WINDOW), 1) % WINDOW
    tri = (kj <= qi).astype(_BF16)
    tq = ATTN_BLOCKS * WINDOW
    kblk = Q_W // KV_W
    prev_blk = lambda n: jnp.maximum(n * ATTN_BLOCKS - 1, 0)
    return pl.pallas_call(
        _attn_kernel,
        grid=(b, s // tq),
        in_specs=[
            pl.BlockSpec((None, Q_W, tq), lambda i, n: (i, 0, n)),
            pl.BlockSpec((None, KV_W, tq), lambda i, n: (i, kblk, n)),
            pl.BlockSpec((None, KV_W, tq), lambda i, n: (i, kblk + 1, n)),
            pl.BlockSpec((None, KV_W, WINDOW), lambda i, n: (i, kblk, prev_blk(n))),
            pl.BlockSpec((None, KV_W, WINDOW), lambda i, n: (i, kblk + 1, prev_blk(n))),
            pl.BlockSpec((None, KV_W, N_META), lambda i, n: (0, kblk, 0)),
            pl.BlockSpec((None, KV_W, N_META), lambda i, n: (0, kblk + 1, 0)),
            pl.BlockSpec(sink_rows.shape, lambda i, n: (0, 0, 0)),
            pl.BlockSpec(tri.shape, lambda i, n: (0, 0)),
        ],
        out_specs=pl.BlockSpec((None, tq, Q_W), lambda i, n: (i, n, 0)),
        out_shape=jax.ShapeDtypeStruct((b, s, Q_W), _F32),
        scratch_shapes=[pltpu.VMEM((Q_W, tq), _F32)],
        compiler_params=pltpu.CompilerParams(
            dimension_semantics=("arbitrary", "arbitrary"), vmem_limit_bytes=VMEM_LIMIT),
        name="swa_attention",
    )(qkv_t, qkv_t, qkv_t, qkv_t, qkv_t, qkv_meta_t, qkv_meta_t, sink_rows, tri)


def _tail_kernel(x_hbm, attn_hbm, y_ref, g_ref, wglu_ref, abn_ref, sbn_ref, wout_ref,
                 nf_ref, wfi_ref, wfo_ref, o_hbm, xbuf, abuf, obuf, sem_x, sem_a, sem_o):
    d = D_MODEL
    nj = pl.num_programs(1)
    last = pl.num_programs(0) * nj - 1
    step = pl.program_id(0) * nj + pl.program_id(1)
    slot = step % 2

    def fetch(s, sl):
        bb, jj = s // nj, s % nj
        return (pltpu.make_async_copy(*_row_set_copy(x_hbm, xbuf, sem_x, bb, jj, sl)),
                pltpu.make_async_copy(*_row_set_copy(attn_hbm, abuf, sem_a, bb, jj, sl)))

    def put(s, sl):
        dst, src, sem = _row_set_copy(o_hbm, obuf, sem_o, s // nj, s % nj, sl)
        return pltpu.make_async_copy(src, dst, sem)

    @pl.when(step == 0)
    def _():
        for c in fetch(step, slot):
            c.start()

    @pl.when(step < last)
    def _():
        for c in fetch(step + 1, 1 - slot):
            c.start()

    for c in fetch(step, slot):
        c.wait()

    @pl.when(step >= 2)
    def _():
        put(step - 2, slot).wait()

    nrows = xbuf.shape[1]
    pieces = [slice(r, r + TAIL_PIECE) for r in range(0, nrows, TAIL_PIECE)]
    xs =[xbuf[slot, p, :] for p in pieces]
    front = []
    for p in pieces:
        z = jax.nn.gelu(y_ref[:, p]).astype(_BF16)
        front.append(lax.dot_general(z, wglu_ref[...], _TN, preferred_element_type=_F32))
    hs = []
    for p, x, zz in zip(pieces, xs, front):
        ssm = zz[:, :d] * jax.nn.sigmoid(zz[:, d:])
        merged = (jax.nn.sigmoid(g_ref[p, :d].astype(_F32)) * _rms_rows(abuf[slot, p, :], abn_ref[...])
                  + jax.nn.sigmoid(g_ref[p, d:].astype(_F32)) * _rms_rows(ssm, sbn_ref[...]))
        h = x + jnp.dot(merged.astype(_BF16), wout_ref[...], preferred_element_type=_F32)
        hs.append((h, _rms_rows(h, nf_ref[...]).astype(_BF16)))
    outs = [h for h, _ in hs]
    for c0, c1 in zip(FFN_EDGES[:-1], FFN_EDGES[1:]):
        for i, (_, hn) in enumerate(hs):
            gate = jnp.dot(hn, wfi_ref[:, c0:c1], preferred_element_type=_F32)
            up = jnp.dot(hn, wfi_ref[:, D_FF + c0:D_FF + c1], preferred_element_type=_F32)
            act = (jax.nn.silu(gate) * up).astype(_BF16)
            outs[i] = outs[i] + jnp.dot(act, wfo_ref[c0:c1, :], preferred_element_type=_F32)
    for p, out in zip(pieces, outs):
        obuf[slot, p, :] = out
    put(step, slot).start()

    @pl.when(step == last)
    def _():
        put(step, slot).wait()

        @pl.when(step >= 1)
        def _():
            put(step - 1, 1 - slot).wait()


def _tail(x, attn, y_t, gates, w_glu, abn, sbn, w_out, norm_ffn, w_fi, w_fo):
    b, s, d = x.shape
    nchunks = s // CHUNK
    x4 = x.reshape(b, nchunks, CHUNK, d)
    a4 = attn.reshape(b, nchunks, CHUNK, d)
    const = lambda a: pl.BlockSpec(a.shape, lambda i, j: (0,) * a.ndim, pipeline_mode=pl.Buffered(1))
    hbm = pl.BlockSpec(memory_space=pl.ANY)
    out = pl.pallas_call(
        _tail_kernel,
        grid=(b, CHUNK),
        in_specs=[
            hbm, hbm,
            pl.BlockSpec((None, d, nchunks), lambda i, j: (i, 0, j)),
            pl.BlockSpec((None, nchunks, gates.shape[2]), lambda i, j: (i, j, 0)),
            const(w_glu), const(abn), const(sbn), const(w_out),
            const(norm_ffn), const(w_fi), const(w_fo),
        ],
        out_specs=hbm,
        out_shape=jax.ShapeDtypeStruct(x4.shape, x.dtype),
        scratch_shapes=[pltpu.VMEM((2, nchunks, d), x.dtype), pltpu.VMEM((2, nchunks, d), attn.dtype),
                        pltpu.VMEM((2, nchunks, d), x.dtype),
                        pltpu.SemaphoreType.DMA((2,)), pltpu.SemaphoreType.DMA((2,)),
                        pltpu.SemaphoreType.DMA((2,))],
        compiler_params=pltpu.CompilerParams(
            dimension_semantics=("arbitrary", "arbitrary"), vmem_limit_bytes=VMEM_LIMIT),
        name="tail",
    )(x4, a4, y_t, gates, w_glu, abn, sbn, w_out, norm_ffn, w_fi, w_fo)
    return out.reshape(b, s, d)


def kernel(x, meta_tokens, norm_mix, w_in, q_norm, k_norm, attn_sinks, lam_re, lam_im, log_dt,
           ssm_b_re, ssm_b_im, ssm_c_re, ssm_c_im, ssm_d, w_glu, attn_branch_norm, ssm_branch_norm,
           w_out, norm_ffn, w_ffn_in, w_ffn_out):
    seq = x.shape[1]
    levels = (seq // CHUNK).bit_length() - 1
    w = w_in[0]
    w_qkv_t = w[:, :QKV_W].T.astype(_BF16)
    w_u_t = w[:, QKV_W:QKV_W + D_MODEL].T.astype(_BF16)
    w_g = w[:, QKV_W + D_MODEL:].astype(_BF16)
    scale = HEAD_DIM ** -0.5 * LOG2_E
    qk_gain = jnp.concatenate([jnp.tile(q_norm[0] * scale, N_Q_HEADS),
                               jnp.tile(k_norm[0], N_KV_HEADS)])[:, None].astype(_F32)
    no_gain = jnp.zeros((8, 1), _F32)
    sink_rows = jnp.repeat((attn_sinks[0] * LOG2_E).reshape(N_KV_HEADS, 1, GQA_GROUP), WINDOW,
                           axis=2).astype(_F32)
    dt = jnp.exp(log_dt[0].astype(_F32))
    lr, li = lam_re[0].astype(_F32), lam_im[0].astype(_F32)
    dtb = jnp.broadcast_to(dt[:, None], lr.shape)
    colp = jnp.stack([lr, li, dtb, jnp.zeros_like(lr)], axis=2)
    d_grp = ssm_d[0].astype(_F32).reshape(SSM_GROUPS, SSM_GROUP_CH, 1)

    meta = meta_tokens.astype(_F32)[None]
    qkv_meta_t = _proj_t(meta, w_qkv_t, norm_mix, qk_gain, norm_heads=N_Q_HEADS + N_KV_HEADS,
                         tm=N_META, name="proj_qkv_meta")
    u_meta_t = _proj_t(meta, w_u_t, norm_mix, no_gain, norm_heads=0, tm=N_META, name="proj_u_meta")
    a_meta = (u_meta_t[0].astype(_F32).reshape(SSM_GROUPS, SSM_GROUP_CH, CHUNK)
              .transpose(0, 2, 1).reshape(SSM_GROUPS, 1, TOEP))

    u_t, qkv_t, w1, w3, pwr, pwi, s_meta = _projections_and_ssm_prep(
        x, w_u_t, w_qkv_t, norm_mix, qk_gain, N_Q_HEADS + N_KV_HEADS, colp,
        ssm_b_re[0].astype(_F32), ssm_b_im[0].astype(_F32),
        ssm_c_re[0].astype(_F32), ssm_c_im[0].astype(_F32), a_meta, d_grp, levels)
    y_t, gates = _ssm_core_and_gates(u_t, w1, w3, pwr, pwi, s_meta, x, w_g, norm_mix, levels)
    attn = _attention(qkv_t, qkv_meta_t, sink_rows)
    return _tail(x, attn, y_t, gates, w_glu[0].astype(_BF16), attn_branch_norm,
                 ssm_branch_norm, w_out[0].astype(_BF16), norm_ffn, w_ffn_in[0].astype(_BF16),
                 w_ffn_out[0].astype(_BF16))
```

```python
import functools

import jax
import jax.numpy as jnp
from jax import lax
from jax.experimental import pallas as pl
from jax.experimental.pallas import tpu as pltpu

D_MODEL = 1024
N_META = 16
HEAD_DIM = 64
N_Q_HEADS = 16
N_KV_HEADS = 4
GQA_GROUP = N_Q_HEADS // N_KV_HEADS
WINDOW = 128
SSM_GROUP_CH = 16
SSM_GROUPS = D_MODEL // SSM_GROUP_CH
SSM_STATE = 64
D_FF = 2816
Q_W = N_Q_HEADS * HEAD_DIM
KV_W = N_KV_HEADS * HEAD_DIM
QKV_W = Q_W + 2 * KV_W
EPS = 1e-6
LOG2_E = 1.4426950408889634

LANES = 128
CHUNK = 16
TOEP = CHUNK * SSM_GROUP_CH
GROUPS_PER_STEP = 8
MXU_DIM = 256
FFN_EDGES = (0, 6 * MXU_DIM, D_FF)
ATTN_BLOCKS = 8
PROJ_PIECE = 256
TAIL_PIECE = 256
PROJ_U_SETS = 2
VMEM_LIMIT = 56 * 1024 * 1024

_F32 = jnp.float32
_BF16 = jnp.bfloat16
_NT = (((1,), (1,)), ((), ()))
_TN = (((0,), (0,)), ((), ()))
_HI = lax.Precision.HIGHEST


def _rms_rows(x, gain):
    return x * lax.rsqrt(jnp.mean(x * x, axis=-1, keepdims=True) + EPS) * gain


def _unit_rms(x):
    return x * lax.rsqrt(jnp.mean(x * x, axis=-1, keepdims=True) + EPS)


def _proj_rows_stages(xn_ref, w_ref, o_ref):
    tm = min(xn_ref.shape[0], PROJ_PIECE)

    def piece(c):
        rows = slice(c * tm, (c + 1) * tm)
        o_ref[rows, :] = jnp.dot(xn_ref[rows, :], w_ref[...], preferred_element_type=_F32).astype(o_ref.dtype)

    return [functools.partial(piece, c) for c in range(xn_ref.shape[0] // tm)]


def _proj_t_stages(x_ref, w_ref, gain_ref, o_ref, norm_heads, xn_ref=None):
    tm_all = x_ref.shape[0]
    tm = min(tm_all, PROJ_PIECE)

    def piece(c):
        cols = slice(c * tm, (c + 1) * tm)
        xn = _unit_rms(x_ref[cols, :]).astype(_BF16)
        if xn_ref is not None:
            xn_ref[cols, :] = xn
        p = lax.dot_general(w_ref[...], xn, _NT, preferred_element_type=_F32)
        if norm_heads:
            rows = norm_heads * HEAD_DIM
            hd = p[:rows].reshape(norm_heads, HEAD_DIM, tm)
            ms = jnp.mean(hd * hd, axis=1, keepdims=True)
            hd = hd * lax.rsqrt(ms + EPS) * gain_ref[...].reshape(norm_heads, HEAD_DIM, 1)
            o_ref[:rows, cols] = hd.reshape(rows, tm).astype(o_ref.dtype)
            o_ref[rows:, cols] = p[rows:].astype(o_ref.dtype)
        else:
            o_ref[:, cols] = p.astype(o_ref.dtype)

    return [functools.partial(piece, c) for c in range(tm_all // tm)]


def _proj_t_kernel(x_ref, w_ref, gain_ref, o_ref, *, norm_heads):
    for stage in _proj_t_stages(x_ref, w_ref, gain_ref, o_ref, norm_heads):
        stage()


def _proj_t(x, w_t, gain, *, norm_heads, tm, name):
    b, s, d = x.shape
    n = w_t.shape[0]
    return pl.pallas_call(
        functools.partial(_proj_t_kernel, norm_heads=norm_heads),
        grid=(b, s // tm),
        in_specs=[
            pl.BlockSpec((None, tm, d), lambda i, j: (i, j, 0)),
            pl.BlockSpec((n, d), lambda i, j: (0, 0)),
            pl.BlockSpec(gain.shape, lambda i, j: (0, 0)),
        ],
        out_specs=pl.BlockSpec((None, n, tm), lambda i, j: (i, 0, j)),
        out_shape=jax.ShapeDtypeStruct((b, n, s), _BF16),
        compiler_params=pltpu.CompilerParams(
            dimension_semantics=("arbitrary", "arbitrary"), vmem_limit_bytes=VMEM_LIMIT),
        name=name,
    )(x, w_t, gain)


def _row_set_copy(hbm4, buf, sem, bb, jj, slot):
    return hbm4.at[bb, :, jj, :], buf.at[slot], sem.at[slot]


def _fetch_row_sets(x_hbm, xbuf, sem):
    nj = CHUNK // PROJ_U_SETS
    last = pl.num_programs(0) * pl.num_programs(1) - 1
    step = pl.program_id(0) * pl.num_programs(1) + pl.program_id(1)
    slot = step % 2
    nchunks = x_hbm.shape[1]

    def fetch(s, sl):
        return [pltpu.make_async_copy(x_hbm.at[s // nj, :, (s % nj) * PROJ_U_SETS + k, :],
                                      xbuf.at[sl, k * nchunks:(k + 1) * nchunks, :], sem.at[sl, k])
                for k in range(PROJ_U_SETS)]

    @pl.when(step == 0)
    def _():
        for c in fetch(step, slot):
            c.start()

    @pl.when(step < last)
    def _():
        for c in fetch(step + 1, 1 - slot):
            c.start()

    for c in fetch(step, slot):
        c.wait()
    return slot


def _proj_kernel(x_hbm, xnat_ref, wu_ref, wq_ref, gain_ref,
                 colp_ref, bre_ref, bim_ref, cre_ref, cim_ref, am_ref, d_ref,
                 u_ref, xn_ref, qkv_ref, w1_ref, w3_ref, pwr_ref, pwi_ref, sm_ref, xbuf, sem,
                 *, levels, norm_heads):
    slot = _fetch_row_sets(x_hbm, xbuf, sem)
    proj_u = _proj_t_stages(xbuf.at[slot], wu_ref, None, u_ref, 0, xn_ref)
    proj_q = _proj_t_stages(xnat_ref, wq_ref, gain_ref, qkv_ref, norm_heads)
    prep = _ssm_prep_stages(colp_ref, bre_ref, bim_ref, cre_ref, cim_ref, am_ref, d_ref,
                            w1_ref, w3_ref, pwr_ref, pwi_ref, sm_ref, levels)
    mixed = [st for pair in zip(proj_u, proj_q) for st in pair]
    for stage in mixed[:3] + prep[:1] + mixed[3:5] + prep[1:] + mixed[5:]:
        stage()


def _projections_and_ssm_prep(x, wu_t, wq_t, qk_gain, norm_heads, colp, b_re, b_im, c_re, c_im,
                              a_meta, d_grp, levels):
    b, s, d = x.shape
    nu, nq = wu_t.shape[0], wq_t.shape[0]
    nchunks = s // CHUNK
    x4 = x.reshape(b, nchunks, CHUNK, d)
    nj = CHUNK // PROJ_U_SETS
    tm = s // nj
    g, ns, gc = SSM_GROUPS, SSM_STATE, SSM_GROUP_CH
    gpp = g // (b * nj)
    assert gpp * b * nj == g
    gspec = lambda *tail: pl.BlockSpec((gpp,) + tail, lambda i, j: (i * nj + j,) + (0,) * len(tail))
    const = lambda a: pl.BlockSpec(a.shape, lambda i, j: (0,) * a.ndim)
    return pl.pallas_call(
        functools.partial(_proj_kernel, levels=levels, norm_heads=norm_heads),
        grid=(b, nj),
        in_specs=[
            pl.BlockSpec(memory_space=pl.ANY),
            pl.BlockSpec((None, tm, d), lambda i, j: (i, j, 0)),
            const(wu_t), const(wq_t), const(qk_gain),
            gspec(ns, 4), gspec(ns, gc), gspec(ns, gc), gspec(gc, ns), gspec(gc, ns), gspec(1, TOEP),
            gspec(gc, 1),
        ],
        out_specs=[
            pl.BlockSpec((None, nu, PROJ_U_SETS * nchunks), lambda i, j: (i, 0, j)),
            pl.BlockSpec((None, PROJ_U_SETS * nchunks, d), lambda i, j: (i, j, 0)),
            pl.BlockSpec((None, nq, tm), lambda i, j: (i, 0, j)),
            gspec(2 * ns, TOEP), gspec(TOEP, TOEP + 2 * ns), gspec(levels, ns, LANES),
            gspec(levels, ns, LANES), gspec(4, ns, LANES),
        ],
        out_shape=[jax.ShapeDtypeStruct((b, nu, s), _BF16),
                   jax.ShapeDtypeStruct((b, s, d), _BF16),
                   jax.ShapeDtypeStruct((b, nq, s), _BF16),
                   jax.ShapeDtypeStruct((g, 2 * ns, TOEP), _BF16),
                   jax.ShapeDtypeStruct((g, TOEP, TOEP + 2 * ns), _BF16),
                   jax.ShapeDtypeStruct((g, levels, ns, LANES), _F32),
                   jax.ShapeDtypeStruct((g, levels, ns, LANES), _F32),
                   jax.ShapeDtypeStruct((g, 4, ns, LANES), _F32)],
        scratch_shapes=[pltpu.VMEM((2, PROJ_U_SETS * nchunks, d), x.dtype),
                        pltpu.SemaphoreType.DMA((2, PROJ_U_SETS))],
        compiler_params=pltpu.CompilerParams(
            dimension_semantics=("arbitrary", "arbitrary"), vmem_limit_bytes=VMEM_LIMIT),
        name="projections_ssm_prep",
    )(x4, x, wu_t, wq_t, qk_gain, colp, b_re, b_im, c_re, c_im, a_meta, d_grp)


def _ssm_prep_stages(colp_ref, bre_ref, bim_ref, cre_ref, cim_ref, am_ref, d_ref,
                     w1_ref, w3_ref, pwr_ref, pwi_ref, sm_ref, levels):
    ns, gc = SSM_STATE, SSM_GROUP_CH

    def split3(a):
        hi = a.astype(_BF16)
        rest = a - hi.astype(_F32)
        mid = rest.astype(_BF16)
        return hi, mid, (rest - mid.astype(_F32)).astype(_BF16)

    pick_cols = lambda a, m3: jnp.dot(jnp.concatenate(split3(a), axis=1), m3, preferred_element_type=_F32)
    pick_rows = lambda m3, a: jnp.dot(m3, jnp.concatenate(split3(a), axis=0), preferred_element_type=_F32)
    l16 = lax.broadcasted_iota(jnp.int32, (gc, TOEP), 1)
    r16 = lax.broadcasted_iota(jnp.int32, (gc, TOEP), 0)
    rc = lax.broadcasted_iota(jnp.int32, (3 * gc, TOEP), 0)
    lc = lax.broadcasted_iota(jnp.int32, (3 * gc, TOEP), 1)
    tile_c3 = (lc % gc == rc % gc).astype(_BF16)
    tau_r = lax.broadcasted_iota(jnp.int32, (3 * LANES, TOEP), 0) % LANES
    jc_l = lax.broadcasted_iota(jnp.int32, (3 * LANES, TOEP), 1)
    pick_rev3 = (tau_r == CHUNK - 1 - jc_l // gc).astype(_BF16)
    tc_r = lax.broadcasted_iota(jnp.int32, (TOEP, 3 * LANES), 0)
    tau_l = lax.broadcasted_iota(jnp.int32, (TOEP, 3 * LANES), 1) % LANES
    pick_next3 = (tau_l == tc_r // gc + 1).astype(_BF16)
    lane = lax.broadcasted_iota(jnp.int32, (ns, LANES), 1)
    lvl = jnp.clip(lane - CHUNK, 0, levels - 1)
    expo = jnp.where(lane <= CHUNK, lane,
                     jnp.where(lane < CHUNK + levels, CHUNK * jnp.left_shift(1, lvl), 0)).astype(_F32)
    is_tau = lane <= CHUNK

    def group(gl):
        lr, li, dt = colp_ref[gl, :, 0:1], colp_ref[gl, :, 1:2], colp_ref[gl, :, 2:3]
        mag = jnp.exp(lr * dt * expo)
        ang = li * dt * expo
        pw_r, pw_i = mag * jnp.cos(ang), mag * jnp.sin(ang)
        ar, ai = pw_r[:, 1:2], pw_i[:, 1:2]
        den = lr * lr + li * li
        nr, ni = ar - 1.0, ai
        fr, fi = (nr * lr + ni * li) / den, (ni * lr - nr * li) / den
        b_re, b_im = bre_ref[gl], bim_ref[gl]
        bbr_t = pick_cols(fr * b_re - fi * b_im, tile_c3)
        bbi_t = pick_cols(fr * b_im + fi * b_re, tile_c3)
        tau_re, tau_im = jnp.where(is_tau, pw_r, 0.0), jnp.where(is_tau, pw_i, 0.0)
        pr, pi = pick_cols(tau_re, pick_rev3), pick_cols(tau_im, pick_rev3)
        xr = pr * bbr_t - pi * bbi_t
        xi = pr * bbi_t + pi * bbr_t
        krev = (jnp.dot(cre_ref[gl], xr, precision=_HI, preferred_element_type=_F32)
                - jnp.dot(cim_ref[gl], xi, precision=_HI, preferred_element_type=_F32))
        krev = krev + jnp.where(l16 - (CHUNK - 1) * gc == r16, d_ref[gl], 0.0)
        rz = jnp.concatenate([krev, jnp.zeros_like(krev)], axis=1)
        toep = jnp.concatenate(
            [rz[:, (CHUNK - 1 - t) * gc:(CHUNK - 1 - t) * gc + TOEP] for t in range(CHUNK)], axis=0)
        w1_ref[gl] = jnp.concatenate([xr, xi], axis=0).astype(w1_ref.dtype)
        gar = pick_rows(pick_next3, tau_re.T)
        gai = pick_rows(pick_next3, tau_im.T)
        crt = jnp.tile(cre_ref[gl], (CHUNK, 1))
        cit = jnp.tile(cim_ref[gl], (CHUNK, 1))
        w3_ref[gl] = jnp.concatenate([toep, crt * gar - cit * gai, -(crt * gai + cit * gar)],
                                     axis=1).astype(w3_ref.dtype)
        for l in range(levels):
            pwr_ref[gl, l] = jnp.broadcast_to(pw_r[:, CHUNK + l:CHUNK + l + 1], (ns, LANES))
            pwi_ref[gl, l] = jnp.broadcast_to(pw_i[:, CHUNK + l:CHUNK + l + 1], (ns, LANES))
        am = am_ref[gl]
        sr = jnp.sum(xr * am, axis=1, keepdims=True)
        si = jnp.sum(xi * am, axis=1, keepdims=True)
        a16r, a16i = pw_r[:, CHUNK:CHUNK + 1], pw_i[:, CHUNK:CHUNK + 1]
        sm_ref[gl, 0] = jnp.broadcast_to(sr, (ns, LANES))
        sm_ref[gl, 1] = jnp.broadcast_to(si, (ns, LANES))
        sm_ref[gl, 2] = jnp.broadcast_to(a16r * sr - a16i * si, (ns, LANES))
        sm_ref[gl, 3] = jnp.broadcast_to(a16r * si + a16i * sr, (ns, LANES))

    return [functools.partial(group, gl) for gl in range(colp_ref.shape[0])]


def _ssm_stages(u_ref, w1_ref, w3_ref, pwr_ref, pwi_ref, sm_ref, y_ref, bre_ref, bim_ref, nchunks, levels):
    ns, gc, gps = SSM_STATE, SSM_GROUP_CH, GROUPS_PER_STEP
    nblk = nchunks // LANES
    blocks = [slice(h * LANES, (h + 1) * LANES) for h in range(nblk)]
    group_rows = [slice(gl * ns, (gl + 1) * ns) for gl in range(gps)]

    def chunk_operand(gl):
        r0 = gl * gc
        return jnp.concatenate(
            [u_ref[r0:r0 + gc, j * nchunks:(j + 1) * nchunks] for j in range(CHUNK)], axis=0)

    def shifted(vals, sh, first):
        if sh % LANES == 0:
            k = sh // LANES
            return [None if h < k else vals[h - k] for h in range(nblk)]
        lane = lax.broadcasted_iota(jnp.int32, (ns, LANES), 1)
        rot = [pltpu.roll(v, sh, axis=1) for v in vals]
        keep = lane >= sh
        return [jnp.where(keep, rot[h], first if h == 0 else rot[h - 1]) for h in range(nblk)]

    def state_increments():
        lane0 = lax.broadcasted_iota(jnp.int32, (ns, LANES), 1) == 0
        for gl, rs in enumerate(group_rows):
            r1 = jnp.dot(w1_ref[gl], chunk_operand(gl), preferred_element_type=_F32)
            bre_ref[rs, :] = r1[:ns]
            bim_ref[rs, :] = r1[ns:]
            bre_ref[rs, blocks[0]] += jnp.where(lane0, sm_ref[gl, 2], 0.0)
            bim_ref[rs, blocks[0]] += jnp.where(lane0, sm_ref[gl, 3], 0.0)

    def scan_level(lvl):
        for gl, rs in enumerate(group_rows):
            p_re, p_im = pwr_ref[gl, lvl], pwi_ref[gl, lvl]
            s_re = [bre_ref[rs, blk] for blk in blocks]
            s_im = [bim_ref[rs, blk] for blk in blocks]
            t_re, t_im = shifted(s_re, 1 << lvl, 0.0), shifted(s_im, 1 << lvl, 0.0)
            for h in range(nblk):
                if t_re[h] is not None:
                    bre_ref[rs, blocks[h]] = s_re[h] + p_re * t_re[h] - p_im * t_im[h]
                    bim_ref[rs, blocks[h]] = s_im[h] + p_re * t_im[h] + p_im * t_re[h]

    def entering_states():
        for gl, rs in enumerate(group_rows):
            prev_re = shifted([bre_ref[rs, blk] for blk in blocks], 1, sm_ref[gl, 0])
            prev_im = shifted([bim_ref[rs, blk] for blk in blocks], 1, sm_ref[gl, 1])
            for h in range(nblk):
                bre_ref[rs, blocks[h]] = prev_re[h]
                bim_ref[rs, blocks[h]] = prev_im[h]

    def outputs(gl):
        r0, rs = gl * gc, group_rows[gl]
        rhs = jnp.concatenate([chunk_operand(gl), bre_ref[rs, :].astype(_BF16),
                               bim_ref[rs, :].astype(_BF16)], axis=0)
        y = jnp.dot(w3_ref[gl], rhs, preferred_element_type=_F32)
        for t in range(CHUNK):
            y_ref[r0:r0 + gc, t * nchunks:(t + 1) * nchunks] = y[t * gc:(t + 1) * gc, :]

    return ([state_increments] + [functools.partial(scan_level, lvl) for lvl in range(levels)]
            + [entering_states] + [functools.partial(outputs, gl) for gl in range(gps)])


def _ssm_gates_kernel(u_ref, w1_ref, w3_ref, pwr_ref, pwi_ref, sm_ref, xn_ref, wg_ref,
                      y_ref, g_ref, bre_ref, bim_ref, *, nchunks, levels):
    ssm = _ssm_stages(u_ref, w1_ref, w3_ref, pwr_ref, pwi_ref, sm_ref, y_ref, bre_ref, bim_ref,
                      nchunks, levels)
    proj = _proj_rows_stages(xn_ref, wg_ref, g_ref)
    every = -(-len(ssm) // len(proj))
    for i, stage in enumerate(ssm):
        if i % every == 0 and proj:
            proj.pop(0)()
        stage()
    for stage in proj:
        stage()


def _ssm_core_and_gates(u_t, w1, w3, pwr, pwi, s_meta, xn, w_gates, levels):
    b, n, s = u_t.shape
    gps, ns = GROUPS_PER_STEP, SSM_STATE
    rows = gps * SSM_GROUP_CH
    nchunks = s // CHUNK
    d = xn.shape[2]
    tm = s // (n // rows)
    tiles = s // tm
    wspec = lambda *tail: pl.BlockSpec((gps,) + tail, lambda gb, i: (gb,) + (0,) * len(tail))
    tile = lambda gb, i: ((gb * b + i) // tiles, (gb * b + i) % tiles, 0)
    return pl.pallas_call(
        functools.partial(_ssm_gates_kernel, nchunks=nchunks, levels=levels),
        grid=(n // rows, b),
        in_specs=[
            pl.BlockSpec((None, rows, s), lambda gb, i: (i, gb, 0)),
            wspec(2 * ns, TOEP), wspec(TOEP, TOEP + 2 * ns),
            wspec(levels, ns, LANES), wspec(levels, ns, LANES), wspec(4, ns, LANES),
            pl.BlockSpec((None, tm, d), tile),
            pl.BlockSpec(w_gates.shape, lambda gb, i: (0, 0)),
        ],
        out_specs=[
            pl.BlockSpec((None, rows, s), lambda gb, i: (i, gb, 0)),
            pl.BlockSpec((None, tm, w_gates.shape[1]), tile),
        ],
        out_shape=[jax.ShapeDtypeStruct((b, n, s), _F32),
                   jax.ShapeDtypeStruct((b, s, w_gates.shape[1]), _BF16)],
        scratch_shapes=[pltpu.VMEM((gps * ns, nchunks), _F32), pltpu.VMEM((gps * ns, nchunks), _F32)],
        compiler_params=pltpu.CompilerParams(
            dimension_semantics=("arbitrary", "arbitrary"), vmem_limit_bytes=VMEM_LIMIT),
        name="ssm_core_gates",
    )(u_t, w1, w3, pwr, pwi, s_meta, xn, w_gates)


def _attn_kernel(q_ref, k_ref, v_ref, kp_ref, vp_ref, km_ref, vm_ref, sink_ref, tri_ref, o_ref, acc_ref):
    first_step = pl.program_id(1) == 0
    wq = GQA_GROUP * WINDOW
    kj = lax.broadcasted_iota(jnp.int32, (WINDOW, wq), 0)
    qi = lax.broadcasted_iota(jnp.int32, (WINDOW, wq), 1) % WINDOW
    in_cur = kj <= qi
    units = [(blk, h) for blk in range(ATTN_BLOCKS) for h in range(N_KV_HEADS)]

    def kv_rows(ref, prev_ref, blk, h):
        rows = slice(h * HEAD_DIM, (h + 1) * HEAD_DIM)
        cur = ref[rows, blk * WINDOW:(blk + 1) * WINDOW]
        prev = prev_ref[rows, :] if blk == 0 else ref[rows, (blk - 1) * WINDOW:blk * WINDOW]
        return cur, prev

    scores = []
    for blk, h in units:
        q4 = jnp.concatenate(
            [q_ref[(h * GQA_GROUP + r) * HEAD_DIM:(h * GQA_GROUP + r + 1) * HEAD_DIM,
                   blk * WINDOW:(blk + 1) * WINDOW] for r in range(GQA_GROUP)], axis=1)
        k_cur, k_prev = kv_rows(k_ref, kp_ref, blk, h)
        k_meta = km_ref[h * HEAD_DIM:(h + 1) * HEAD_DIM, :]
        scores.append(tuple(lax.dot_general(k, q4, _TN, preferred_element_type=_F32)
                            for k in (k_cur, k_prev, k_meta)))
    for (blk, h), (s_cur, s_prev, s_meta) in zip(units, scores):
        if blk == 0:
            s_prev = jnp.where(first_step, -jnp.inf, s_prev)
        s_sel = jnp.where(in_cur, s_cur, s_prev)
        sink = sink_ref[h]
        m = jnp.maximum(jnp.maximum(jnp.max(s_sel, axis=0, keepdims=True),
                                    jnp.max(s_meta, axis=0, keepdims=True)), sink)
        e_sel = jnp.exp2(s_sel - m)
        e_meta = jnp.exp2(s_meta - m)
        den = (jnp.sum(e_sel, axis=0, keepdims=True) + jnp.sum(e_meta, axis=0, keepdims=True)
               + jnp.exp2(sink - m))
        e_bf = e_sel.astype(_BF16)
        p_cur = e_bf * tri_ref[...]
        p_all = jnp.concatenate([p_cur, e_bf - p_cur, e_meta.astype(_BF16)], axis=0)
        v_cur, v_prev = kv_rows(v_ref, vp_ref, blk, h)
        v_all = jnp.concatenate([v_cur, v_prev, vm_ref[h * HEAD_DIM:(h + 1) * HEAD_DIM, :]], axis=1)
        o = jnp.dot(v_all, p_all, preferred_element_type=_F32) * (1.0 / den)
        for r in range(GQA_GROUP):
            hq = h * GQA_GROUP + r
            acc_ref[hq * HEAD_DIM:(hq + 1) * HEAD_DIM, blk * WINDOW:(blk + 1) * WINDOW] = (
                o[:, r * WINDOW:(r + 1) * WINDOW])
    o_ref[...] = acc_ref[...].T.astype(o_ref.dtype)


def _attention(qkv_t, qkv_meta_t, sink_rows):
    b, _, s = qkv_t.shape
    kj = lax.broadcasted_iota(jnp.int32, (WINDOW, GQA_GROUP * WINDOW), 0)
    qi = lax.broadcasted_iota(jnp.int32, (WINDOW, GQA_GROUP * WINDOW), 1) % WINDOW
    tri = (kj <= qi).astype(_BF16)
    tq = ATTN_BLOCKS * WINDOW
    kblk = Q_W // KV_W
    prev_blk = lambda n: jnp.maximum(n * ATTN_BLOCKS - 1, 0)
    return pl.pallas_call(
        _attn_kernel,
        grid=(b, s // tq),
        in_specs=[
            pl.BlockSpec((None, Q_W, tq), lambda i, n: (i, 0, n)),
            pl.BlockSpec((None, KV_W, tq), lambda i, n: (i, kblk, n)),
            pl.BlockSpec((None, KV_W, tq), lambda i, n: (i, kblk + 1, n)),
            pl.BlockSpec((None, KV_W, WINDOW), lambda i, n: (i, kblk, prev_blk(n))),
            pl.BlockSpec((None, KV_W, WINDOW), lambda i, n: (i, kblk + 1, prev_blk(n))),
            pl.BlockSpec((None, KV_W, N_META), lambda i, n: (0, kblk, 0)),
            pl.BlockSpec((None, KV_W, N_META), lambda i, n: (0, kblk + 1, 0)),
            pl.BlockSpec(sink_rows.shape, lambda i, n: (0, 0, 0)),
            pl.BlockSpec(tri.shape, lambda i, n: (0, 0)),
        ],
        out_specs=pl.BlockSpec((None, tq, Q_W), lambda i, n: (i, n, 0)),
        out_shape=jax.ShapeDtypeStruct((b, s, Q_W), _F32),
        scratch_shapes=[pltpu.VMEM((Q_W, tq), _F32)],
        compiler_params=pltpu.CompilerParams(
            dimension_semantics=("arbitrary", "arbitrary"), vmem_limit_bytes=VMEM_LIMIT),
        name="swa_attention",
    )(qkv_t, qkv_t, qkv_t, qkv_t, qkv_t, qkv_meta_t, qkv_meta_t, sink_rows, tri)


def _tail_kernel(x_hbm, attn_hbm, y_ref, g_ref, wglu_ref, abn_ref, sbn_ref, wout_ref,
                 wfi_ref, wfo_ref, o_hbm, xbuf, abuf, obuf, sem_x, sem_a, sem_o):
    d = D_MODEL
    nj = pl.num_programs(1)
    last = pl.num_programs(0) * nj - 1
    step = pl.program_id(0) * nj + pl.program_id(1)
    slot = step % 2

    def fetch(s, sl):
        bb, jj = s // nj, s % nj
        return (pltpu.make_async_copy(*_row_set_copy(x_hbm, xbuf, sem_x, bb, jj, sl)),
                pltpu.make_async_copy(*_row_set_copy(attn_hbm, abuf, sem_a, bb, jj, sl)))

    def put(s, sl):
        dst, src, sem = _row_set_copy(o_hbm, obuf, sem_o, s // nj, s % nj, sl)
        return pltpu.make_async_copy(src, dst, sem)

    @pl.when(step == 0)
    def _():
        for c in fetch(step, slot):
            c.start()

    @pl.when(step < last)
    def _():
        for c in fetch(step + 1, 1 - slot):
            c.start()

    for c in fetch(step, slot):
        c.wait()

    @pl.when(step >= 2)
    def _():
        put(step - 2, slot).wait()

    nrows = xbuf.shape[1]
    pieces = [slice(r, r + TAIL_PIECE) for r in range(0, nrows, TAIL_PIECE)]
    xs =[xbuf[slot, p, :] for p in pieces]
    front = []
    for p in pieces:
        z = jax.nn.gelu(y_ref[:, p]).astype(_BF16)
        front.append(lax.dot_general(z, wglu_ref[...], _TN, preferred_element_type=_F32))
    hs = []
    for p, x, zz in zip(pieces, xs, front):
        ssm = zz[:, :d] * jax.nn.sigmoid(zz[:, d:])
        merged = (jax.nn.sigmoid(g_ref[p, :d].astype(_F32)) * _rms_rows(abuf[slot, p, :], abn_ref[...])
                  + jax.nn.sigmoid(g_ref[p, d:].astype(_F32)) * _rms_rows(ssm, sbn_ref[...]))
        h = x + jnp.dot(merged.astype(_BF16), wout_ref[...], preferred_element_type=_F32)
        hs.append((h, _unit_rms(h).astype(_BF16)))
    outs = [h for h, _ in hs]
    for c0, c1 in zip(FFN_EDGES[:-1], FFN_EDGES[1:]):
        for i, (_, hn) in enumerate(hs):
            gate = jnp.dot(hn, wfi_ref[:, c0:c1], preferred_element_type=_F32)
            up = jnp.dot(hn, wfi_ref[:, D_FF + c0:D_FF + c1], preferred_element_type=_F32)
            act = (jax.nn.silu(gate) * up).astype(_BF16)
            outs[i] = outs[i] + jnp.dot(act, wfo_ref[c0:c1, :], preferred_element_type=_F32)
    for p, out in zip(pieces, outs):
        obuf[slot, p, :] = out
    put(step, slot).start()

    @pl.when(step == last)
    def _():
        put(step, slot).wait()

        @pl.when(step >= 1)
        def _():
            put(step - 1, 1 - slot).wait()


def _tail(x, attn, y_t, gates, w_glu, abn, sbn, w_out, w_fi, w_fo):
    b, s, d = x.shape
    nchunks = s // CHUNK
    x4 = x.reshape(b, nchunks, CHUNK, d)
    a4 = attn.reshape(b, nchunks, CHUNK, d)
    const = lambda a: pl.BlockSpec(a.shape, lambda i, j: (0,) * a.ndim, pipeline_mode=pl.Buffered(1))
    hbm = pl.BlockSpec(memory_space=pl.ANY)
    out = pl.pallas_call(
        _tail_kernel,
        grid=(b, CHUNK),
        in_specs=[
            hbm, hbm,
            pl.BlockSpec((None, d, nchunks), lambda i, j: (i, 0, j)),
            pl.BlockSpec((None, nchunks, gates.shape[2]), lambda i, j: (i, j, 0)),
            const(w_glu), const(abn), const(sbn), const(w_out),
            const(w_fi), const(w_fo),
        ],
        out_specs=hbm,
        out_shape=jax.ShapeDtypeStruct(x4.shape, x.dtype),
        scratch_shapes=[pltpu.VMEM((2, nchunks, d), x.dtype), pltpu.VMEM((2, nchunks, d), attn.dtype),
                        pltpu.VMEM((2, nchunks, d), x.dtype),
                        pltpu.SemaphoreType.DMA((2,)), pltpu.SemaphoreType.DMA((2,)),
                        pltpu.SemaphoreType.DMA((2,))],
        compiler_params=pltpu.CompilerParams(
            dimension_semantics=("arbitrary", "arbitrary"), vmem_limit_bytes=VMEM_LIMIT),
        name="tail",
    )(x4, a4, y_t, gates, w_glu, abn, sbn, w_out, w_fi, w_fo)
    return out.reshape(b, s, d)


def kernel(x, meta_tokens, norm_mix, w_in, q_norm, k_norm, attn_sinks, lam_re, lam_im, log_dt,
           ssm_b_re, ssm_b_im, ssm_c_re, ssm_c_im, ssm_d, w_glu, attn_branch_norm, ssm_branch_norm,
           w_out, norm_ffn, w_ffn_in, w_ffn_out):
    seq = x.shape[1]
    levels = (seq // CHUNK).bit_length() - 1
    w = w_in[0] * norm_mix[0].astype(_F32)[:, None]
    w_qkv_t = w[:, :QKV_W].T.astype(_BF16)
    w_u_t = w[:, QKV_W:QKV_W + D_MODEL].T.astype(_BF16)
    w_g = w[:, QKV_W + D_MODEL:].astype(_BF16)
    scale = HEAD_DIM ** -0.5 * LOG2_E
    qk_gain = jnp.concatenate([jnp.tile(q_norm[0] * scale, N_Q_HEADS),
                               jnp.tile(k_norm[0], N_KV_HEADS)])[:, None].astype(_F32)
    no_gain = jnp.zeros((8, 1), _F32)
    sink_rows = jnp.repeat((attn_sinks[0] * LOG2_E).reshape(N_KV_HEADS, 1, GQA_GROUP), WINDOW,
                           axis=2).astype(_F32)
    dt = jnp.exp(log_dt[0].astype(_F32))
    lr, li = lam_re[0].astype(_F32), lam_im[0].astype(_F32)
    dtb = jnp.broadcast_to(dt[:, None], lr.shape)
    colp = jnp.stack([lr, li, dtb, jnp.zeros_like(lr)], axis=2)
    d_grp = ssm_d[0].astype(_F32).reshape(SSM_GROUPS, SSM_GROUP_CH, 1)

    meta = meta_tokens.astype(_F32)[None]
    qkv_meta_t = _proj_t(meta, w_qkv_t, qk_gain, norm_heads=N_Q_HEADS + N_KV_HEADS,
                         tm=N_META, name="proj_qkv_meta")
    u_meta_t = _proj_t(meta, w_u_t, no_gain, norm_heads=0, tm=N_META, name="proj_u_meta")
    a_meta = (u_meta_t[0].astype(_F32).reshape(SSM_GROUPS, SSM_GROUP_CH, CHUNK)
              .transpose(0, 2, 1).reshape(SSM_GROUPS, 1, TOEP))

    u_t, xn, qkv_t, w1, w3, pwr, pwi, s_meta = _projections_and_ssm_prep(
        x, w_u_t, w_qkv_t, qk_gain, N_Q_HEADS + N_KV_HEADS, colp,
        ssm_b_re[0].astype(_F32), ssm_b_im[0].astype(_F32),
        ssm_c_re[0].astype(_F32), ssm_c_im[0].astype(_F32), a_meta, d_grp, levels)
    y_t, gates = _ssm_core_and_gates(u_t, w1, w3, pwr, pwi, s_meta, xn, w_g, levels)
    attn = _attention(qkv_t, qkv_meta_t, sink_rows)
    return _tail(x, attn, y_t, gates, w_glu[0].astype(_BF16), attn_branch_norm,
                 ssm_branch_norm, w_out[0].astype(_BF16),
                 (w_ffn_in[0] * norm_ffn[0].astype(_F32)[:, None]).astype(_BF16),
                 w_ffn_out[0].astype(_BF16))
```

```python
import functools

import jax
import jax.numpy as jnp
from jax import lax
from jax.experimental import pallas as pl
from jax.experimental.pallas import tpu as pltpu

D_MODEL = 1024
N_META = 16
HEAD_DIM = 64
N_Q_HEADS = 16
N_KV_HEADS = 4
GQA_GROUP = N_Q_HEADS // N_KV_HEADS
WINDOW = 128
SSM_GROUP_CH = 16
SSM_GROUPS = D_MODEL // SSM_GROUP_CH
SSM_STATE = 64
D_FF = 2816
Q_W = N_Q_HEADS * HEAD_DIM
KV_W = N_KV_HEADS * HEAD_DIM
QKV_W = Q_W + 2 * KV_W
EPS = 1e-6
LOG2_E = 1.4426950408889634

LANES = 128
CHUNK = 16
TOEP = CHUNK * SSM_GROUP_CH
GROUPS_PER_STEP = 8
MXU_DIM = 256
FFN_EDGES = (0, 6 * MXU_DIM, D_FF)
ATTN_BLOCKS = 8
PROJ_PIECE = 256
TAIL_PIECE = 256
PROJ_U_SETS = 2
VMEM_LIMIT = 56 * 1024 * 1024

_F32 = jnp.float32
_BF16 = jnp.bfloat16
_NT = (((1,), (1,)), ((), ()))
_TN = (((0,), (0,)), ((), ()))
_HI = lax.Precision.HIGHEST


def _rms_rows(x, gain):
    return x * lax.rsqrt(jnp.mean(x * x, axis=-1, keepdims=True) + EPS) * gain


def _unit_rms(x):
    return x * lax.rsqrt(jnp.mean(x * x, axis=-1, keepdims=True) + EPS)


def _proj_rows_stages(xn_ref, w_ref, o_ref):
    tm = min(xn_ref.shape[0], PROJ_PIECE)

    def piece(c):
        rows = slice(c * tm, (c + 1) * tm)
        o_ref[rows, :] = jnp.dot(xn_ref[rows, :], w_ref[...], preferred_element_type=_F32).astype(o_ref.dtype)

    return [functools.partial(piece, c) for c in range(xn_ref.shape[0] // tm)]


def _proj_t_stages(x_ref, w_ref, gain_ref, o_ref, norm_heads, xn_ref=None):
    tm_all = x_ref.shape[0]
    tm = min(tm_all, PROJ_PIECE)

    def piece(c):
        cols = slice(c * tm, (c + 1) * tm)
        xn = _unit_rms(x_ref[cols, :]).astype(_BF16)
        if xn_ref is not None:
            xn_ref[cols, :] = xn
        p = lax.dot_general(w_ref[...], xn, _NT, preferred_element_type=_F32)
        if norm_heads:
            rows = norm_heads * HEAD_DIM
            hd = p[:rows].reshape(norm_heads, HEAD_DIM, tm)
            ms = jnp.mean(hd * hd, axis=1, keepdims=True)
            hd = hd * lax.rsqrt(ms + EPS) * gain_ref[...].reshape(norm_heads, HEAD_DIM, 1)
            o_ref[:rows, cols] = hd.reshape(rows, tm).astype(o_ref.dtype)
            o_ref[rows:, cols] = p[rows:].astype(o_ref.dtype)
        else:
            o_ref[:, cols] = p.astype(o_ref.dtype)

    return [functools.partial(piece, c) for c in range(tm_all // tm)]


def _proj_t_kernel(x_ref, w_ref, gain_ref, o_ref, *, norm_heads):
    for stage in _proj_t_stages(x_ref, w_ref, gain_ref, o_ref, norm_heads):
        stage()


def _proj_t(x, w_t, gain, *, norm_heads, tm, name):
    b, s, d = x.shape
    n = w_t.shape[0]
    return pl.pallas_call(
        functools.partial(_proj_t_kernel, norm_heads=norm_heads),
        grid=(b, s // tm),
        in_specs=[
            pl.BlockSpec((None, tm, d), lambda i, j: (i, j, 0)),
            pl.BlockSpec((n, d), lambda i, j: (0, 0)),
            pl.BlockSpec(gain.shape, lambda i, j: (0, 0)),
        ],
        out_specs=pl.BlockSpec((None, n, tm), lambda i, j: (i, 0, j)),
        out_shape=jax.ShapeDtypeStruct((b, n, s), _BF16),
        compiler_params=pltpu.CompilerParams(
            dimension_semantics=("arbitrary", "arbitrary"), vmem_limit_bytes=VMEM_LIMIT),
        name=name,
    )(x, w_t, gain)


def _row_set_copy(hbm4, buf, sem, bb, jj, slot):
    return hbm4.at[bb, :, jj, :], buf.at[slot], sem.at[slot]


def _fetch_row_sets(x_hbm, xbuf, sem):
    nj = CHUNK // PROJ_U_SETS
    last = pl.num_programs(0) * pl.num_programs(1) - 1
    step = pl.program_id(0) * pl.num_programs(1) + pl.program_id(1)
    slot = step % 2
    nchunks = x_hbm.shape[1]

    def fetch(s, sl):
        return [pltpu.make_async_copy(x_hbm.at[s // nj, :, (s % nj) * PROJ_U_SETS + k, :],
                                      xbuf.at[sl, k * nchunks:(k + 1) * nchunks, :], sem.at[sl, k])
                for k in range(PROJ_U_SETS)]

    @pl.when(step == 0)
    def _():
        for c in fetch(step, slot):
            c.start()

    @pl.when(step < last)
    def _():
        for c in fetch(step + 1, 1 - slot):
            c.start()

    for c in fetch(step, slot):
        c.wait()
    return slot


def _proj_kernel(x_hbm, xnat_ref, wu_ref, wq_ref, gain_ref,
                 colp_ref, bre_ref, bim_ref, cre_ref, cim_ref, am_ref, d_ref,
                 u_ref, xn_ref, qkv_ref, w1_ref, w3_ref, pwr_ref, pwi_ref, sm_ref, xbuf, sem,
                 *, levels, norm_heads):
    slot = _fetch_row_sets(x_hbm, xbuf, sem)
    proj_u = _proj_t_stages(xbuf.at[slot], wu_ref, None, u_ref, 0, xn_ref)
    proj_q = _proj_t_stages(xnat_ref, wq_ref, gain_ref, qkv_ref, norm_heads)
    prep = _ssm_prep_stages(colp_ref, bre_ref, bim_ref, cre_ref, cim_ref, am_ref, d_ref,
                            w1_ref, w3_ref, pwr_ref, pwi_ref, sm_ref, levels)
    mixed = [st for pair in zip(proj_u, proj_q) for st in pair]
    for stage in mixed[:3] + prep[:1] + mixed[3:5] + prep[1:] + mixed[5:]:
        stage()


def _projections_and_ssm_prep(x, wu_t, wq_t, qk_gain, norm_heads, colp, b_re, b_im, c_re, c_im,
                              a_meta, d_grp, levels):
    b, s, d = x.shape
    nu, nq = wu_t.shape[0], wq_t.shape[0]
    nchunks = s // CHUNK
    x4 = x.reshape(b, nchunks, CHUNK, d)
    nj = CHUNK // PROJ_U_SETS
    tm = s // nj
    g, ns, gc = SSM_GROUPS, SSM_STATE, SSM_GROUP_CH
    gpp = g // (b * nj)
    assert gpp * b * nj == g
    gspec = lambda *tail: pl.BlockSpec((gpp,) + tail, lambda i, j: (i * nj + j,) + (0,) * len(tail))
    const = lambda a: pl.BlockSpec(a.shape, lambda i, j: (0,) * a.ndim)
    return pl.pallas_call(
        functools.partial(_proj_kernel, levels=levels, norm_heads=norm_heads),
        grid=(b, nj),
        in_specs=[
            pl.BlockSpec(memory_space=pl.ANY),
            pl.BlockSpec((None, tm, d), lambda i, j: (i, j, 0)),
            const(wu_t), const(wq_t), const(qk_gain),
            gspec(ns, 4), gspec(ns, gc), gspec(ns, gc), gspec(gc, ns), gspec(gc, ns), gspec(1, TOEP),
            gspec(gc, 1),
        ],
        out_specs=[
            pl.BlockSpec((None, nu, PROJ_U_SETS * nchunks), lambda i, j: (i, 0, j)),
            pl.BlockSpec((None, PROJ_U_SETS * nchunks, d), lambda i, j: (i, j, 0)),
            pl.BlockSpec((None, nq, tm), lambda i, j: (i, 0, j)),
            gspec(2 * ns, TOEP), gspec(TOEP, TOEP + 2 * ns), gspec(levels, ns, LANES),
            gspec(levels, ns, LANES), gspec(4, ns, LANES),
        ],
        out_shape=[jax.ShapeDtypeStruct((b, nu, s), _BF16),
                   jax.ShapeDtypeStruct((b, s, d), _BF16),
                   jax.ShapeDtypeStruct((b, nq, s), _BF16),
                   jax.ShapeDtypeStruct((g, 2 * ns, TOEP), _BF16),
                   jax.ShapeDtypeStruct((g, TOEP, TOEP + 2 * ns), _BF16),
                   jax.ShapeDtypeStruct((g, levels, ns, LANES), _F32),
                   jax.ShapeDtypeStruct((g, levels, ns, LANES), _F32),
                   jax.ShapeDtypeStruct((g, 4, ns, LANES), _F32)],
        scratch_shapes=[pltpu.VMEM((2, PROJ_U_SETS * nchunks, d), x.dtype),
                        pltpu.SemaphoreType.DMA((2, PROJ_U_SETS))],
        compiler_params=pltpu.CompilerParams(
            dimension_semantics=("arbitrary", "arbitrary"), vmem_limit_bytes=VMEM_LIMIT),
        name="projections_ssm_prep",
    )(x4, x, wu_t, wq_t, qk_gain, colp, b_re, b_im, c_re, c_im, a_meta, d_grp)


def _ssm_prep_stages(colp_ref, bre_ref, bim_ref, cre_ref, cim_ref, am_ref, d_ref,
                     w1_ref, w3_ref, pwr_ref, pwi_ref, sm_ref, levels):
    ns, gc = SSM_STATE, SSM_GROUP_CH

    def split3(a):
        hi = a.astype(_BF16)
        rest = a - hi.astype(_F32)
        mid = rest.astype(_BF16)
        return hi, mid, (rest - mid.astype(_F32)).astype(_BF16)

    pick_cols = lambda a, m3: jnp.dot(jnp.concatenate(split3(a), axis=1), m3, preferred_element_type=_F32)
    pick_rows = lambda m3, a: jnp.dot(m3, jnp.concatenate(split3(a), axis=0), preferred_element_type=_F32)
    l16 = lax.broadcasted_iota(jnp.int32, (gc, TOEP), 1)
    r16 = lax.broadcasted_iota(jnp.int32, (gc, TOEP), 0)
    rc = lax.broadcasted_iota(jnp.int32, (3 * gc, TOEP), 0)
    lc = lax.broadcasted_iota(jnp.int32, (3 * gc, TOEP), 1)
    tile_c3 = (lc % gc == rc % gc).astype(_BF16)
    tau_r = lax.broadcasted_iota(jnp.int32, (3 * LANES, TOEP), 0) % LANES
    jc_l = lax.broadcasted_iota(jnp.int32, (3 * LANES, TOEP), 1)
    pick_rev3 = (tau_r == CHUNK - 1 - jc_l // gc).astype(_BF16)
    tc_r = lax.broadcasted_iota(jnp.int32, (TOEP, 3 * LANES), 0)
    tau_l = lax.broadcasted_iota(jnp.int32, (TOEP, 3 * LANES), 1) % LANES
    pick_next3 = (tau_l == tc_r // gc + 1).astype(_BF16)
    lane = lax.broadcasted_iota(jnp.int32, (ns, LANES), 1)
    lvl = jnp.clip(lane - CHUNK, 0, levels - 1)
    expo = jnp.where(lane <= CHUNK, lane,
                     jnp.where(lane < CHUNK + levels, CHUNK * jnp.left_shift(1, lvl), 0)).astype(_F32)
    is_tau = lane <= CHUNK

    def group(gl):
        lr, li, dt = colp_ref[gl, :, 0:1], colp_ref[gl, :, 1:2], colp_ref[gl, :, 2:3]
        mag = jnp.exp(lr * dt * expo)
        ang = li * dt * expo
        pw_r, pw_i = mag * jnp.cos(ang), mag * jnp.sin(ang)
        ar, ai = pw_r[:, 1:2], pw_i[:, 1:2]
        den = lr * lr + li * li
        nr, ni = ar - 1.0, ai
        fr, fi = (nr * lr + ni * li) / den, (ni * lr - nr * li) / den
        b_re, b_im = bre_ref[gl], bim_ref[gl]
        bbr_t = pick_cols(fr * b_re - fi * b_im, tile_c3)
        bbi_t = pick_cols(fr * b_im + fi * b_re, tile_c3)
        tau_re, tau_im = jnp.where(is_tau, pw_r, 0.0), jnp.where(is_tau, pw_i, 0.0)
        pr, pi = pick_cols(tau_re, pick_rev3), pick_cols(tau_im, pick_rev3)
        xr = pr * bbr_t - pi * bbi_t
        xi = pr * bbi_t + pi * bbr_t
        krev = (jnp.dot(cre_ref[gl], xr, precision=_HI, preferred_element_type=_F32)
                - jnp.dot(cim_ref[gl], xi, precision=_HI, preferred_element_type=_F32))
        krev = krev + jnp.where(l16 - (CHUNK - 1) * gc == r16, d_ref[gl], 0.0)
        rz = jnp.concatenate([krev, jnp.zeros_like(krev)], axis=1)
        toep = jnp.concatenate(
            [rz[:, (CHUNK - 1 - t) * gc:(CHUNK - 1 - t) * gc + TOEP] for t in range(CHUNK)], axis=0)
        w1_ref[gl] = jnp.concatenate([xr, xi], axis=0).astype(w1_ref.dtype)
        gar = pick_rows(pick_next3, tau_re.T)
        gai = pick_rows(pick_next3, tau_im.T)
        crt = jnp.tile(cre_ref[gl], (CHUNK, 1))
        cit = jnp.tile(cim_ref[gl], (CHUNK, 1))
        w3_ref[gl] = jnp.concatenate([toep, crt * gar - cit * gai, -(crt * gai + cit * gar)],
                                     axis=1).astype(w3_ref.dtype)
        for l in range(levels):
            pwr_ref[gl, l] = jnp.broadcast_to(pw_r[:, CHUNK + l:CHUNK + l + 1], (ns, LANES))
            pwi_ref[gl, l] = jnp.broadcast_to(pw_i[:, CHUNK + l:CHUNK + l + 1], (ns, LANES))
        am = am_ref[gl]
        sr = jnp.sum(xr * am, axis=1, keepdims=True)
        si = jnp.sum(xi * am, axis=1, keepdims=True)
        a16r, a16i = pw_r[:, CHUNK:CHUNK + 1], pw_i[:, CHUNK:CHUNK + 1]
        sm_ref[gl, 0] = jnp.broadcast_to(sr, (ns, LANES))
        sm_ref[gl, 1] = jnp.broadcast_to(si, (ns, LANES))
        sm_ref[gl, 2] = jnp.broadcast_to(a16r * sr - a16i * si, (ns, LANES))
        sm_ref[gl, 3] = jnp.broadcast_to(a16r * si + a16i * sr, (ns, LANES))

    return [functools.partial(group, gl) for gl in range(colp_ref.shape[0])]


def _ssm_stages(u_ref, w1_ref, w3_ref, pwr_ref, pwi_ref, sm_ref, y_ref, bre_ref, bim_ref, nchunks, levels):
    ns, gc, gps = SSM_STATE, SSM_GROUP_CH, GROUPS_PER_STEP
    nblk = nchunks // LANES
    blocks = [slice(h * LANES, (h + 1) * LANES) for h in range(nblk)]
    group_rows = [slice(gl * ns, (gl + 1) * ns) for gl in range(gps)]

    def chunk_operand(gl):
        r0 = gl * gc
        return jnp.concatenate(
            [u_ref[r0:r0 + gc, j * nchunks:(j + 1) * nchunks] for j in range(CHUNK)], axis=0)

    def shifted(vals, sh, first):
        if sh % LANES == 0:
            k = sh // LANES
            return [None if h < k else vals[h - k] for h in range(nblk)]
        lane = lax.broadcasted_iota(jnp.int32, (ns, LANES), 1)
        rot = [pltpu.roll(v, sh, axis=1) for v in vals]
        keep = lane >= sh
        return [jnp.where(keep, rot[h], first if h == 0 else rot[h - 1]) for h in range(nblk)]

    def state_increments():
        lane0 = lax.broadcasted_iota(jnp.int32, (ns, LANES), 1) == 0
        for gl, rs in enumerate(group_rows):
            r1 = jnp.dot(w1_ref[gl], chunk_operand(gl), preferred_element_type=_F32)
            bre_ref[rs, :] = r1[:ns]
            bim_ref[rs, :] = r1[ns:]
            bre_ref[rs, blocks[0]] += jnp.where(lane0, sm_ref[gl, 2], 0.0)
            bim_ref[rs, blocks[0]] += jnp.where(lane0, sm_ref[gl, 3], 0.0)

    def scan_level(lvl):
        for gl, rs in enumerate(group_rows):
            p_re, p_im = pwr_ref[gl, lvl], pwi_ref[gl, lvl]
            s_re = [bre_ref[rs, blk] for blk in blocks]
            s_im = [bim_ref[rs, blk] for blk in blocks]
            t_re, t_im = shifted(s_re, 1 << lvl, 0.0), shifted(s_im, 1 << lvl, 0.0)
            for h in range(nblk):
                if t_re[h] is not None:
                    bre_ref[rs, blocks[h]] = s_re[h] + p_re * t_re[h] - p_im * t_im[h]
                    bim_ref[rs, blocks[h]] = s_im[h] + p_re * t_im[h] + p_im * t_re[h]

    def entering_states():
        for gl, rs in enumerate(group_rows):
            prev_re = shifted([bre_ref[rs, blk] for blk in blocks], 1, sm_ref[gl, 0])
            prev_im = shifted([bim_ref[rs, blk] for blk in blocks], 1, sm_ref[gl, 1])
            for h in range(nblk):
                bre_ref[rs, blocks[h]] = prev_re[h]
                bim_ref[rs, blocks[h]] = prev_im[h]

    def outputs(gl):
        r0, rs = gl * gc, group_rows[gl]
        rhs = jnp.concatenate([chunk_operand(gl), bre_ref[rs, :].astype(_BF16),
                               bim_ref[rs, :].astype(_BF16)], axis=0)
        y = jnp.dot(w3_ref[gl], rhs, preferred_element_type=_F32)
        for t in range(CHUNK):
            y_ref[r0:r0 + gc, t * nchunks:(t + 1) * nchunks] = y[t * gc:(t + 1) * gc, :]

    return ([state_increments] + [functools.partial(scan_level, lvl) for lvl in range(levels)]
            + [entering_states] + [functools.partial(outputs, gl) for gl in range(gps)])


def _ssm_gates_kernel(u_ref, w1_ref, w3_ref, pwr_ref, pwi_ref, sm_ref, xn_ref, wg_ref,
                      y_ref, g_ref, bre_ref, bim_ref, *, nchunks, levels):
    ssm = _ssm_stages(u_ref, w1_ref, w3_ref, pwr_ref, pwi_ref, sm_ref, y_ref, bre_ref, bim_ref,
                      nchunks, levels)
    proj = _proj_rows_stages(xn_ref, wg_ref, g_ref)
    every = -(-len(ssm) // len(proj))
    for i, stage in enumerate(ssm):
        if i % every == 0 and proj:
            proj.pop(0)()
        stage()
    for stage in proj:
        stage()


def _ssm_core_and_gates(u_t, w1, w3, pwr, pwi, s_meta, xn, w_gates, levels):
    b, n, s = u_t.shape
    gps, ns = GROUPS_PER_STEP, SSM_STATE
    rows = gps * SSM_GROUP_CH
    nchunks = s // CHUNK
    d = xn.shape[2]
    tm = s // (n // rows)
    tiles = s // tm
    wspec = lambda *tail: pl.BlockSpec((gps,) + tail, lambda gb, i: (gb,) + (0,) * len(tail))
    tile = lambda gb, i: ((gb * b + i) // tiles, (gb * b + i) % tiles, 0)
    return pl.pallas_call(
        functools.partial(_ssm_gates_kernel, nchunks=nchunks, levels=levels),
        grid=(n // rows, b),
        in_specs=[
            pl.BlockSpec((None, rows, s), lambda gb, i: (i, gb, 0)),
            wspec(2 * ns, TOEP), wspec(TOEP, TOEP + 2 * ns),
            wspec(levels, ns, LANES), wspec(levels, ns, LANES), wspec(4, ns, LANES),
            pl.BlockSpec((None, tm, d), tile),
            pl.BlockSpec(w_gates.shape, lambda gb, i: (0, 0)),
        ],
        out_specs=[
            pl.BlockSpec((None, rows, s), lambda gb, i: (i, gb, 0)),
            pl.BlockSpec((None, tm, w_gates.shape[1]), tile),
        ],
        out_shape=[jax.ShapeDtypeStruct((b, n, s), _F32),
                   jax.ShapeDtypeStruct((b, s, w_gates.shape[1]), _BF16)],
        scratch_shapes=[pltpu.VMEM((gps * ns, nchunks), _F32), pltpu.VMEM((gps * ns, nchunks), _F32)],
        compiler_params=pltpu.CompilerParams(
            dimension_semantics=("arbitrary", "arbitrary"), vmem_limit_bytes=VMEM_LIMIT),
        name="ssm_core_gates",
    )(u_t, w1, w3, pwr, pwi, s_meta, xn, w_gates)


def _attn_kernel(q_ref, k_ref, v_ref, kp_ref, vp_ref, km_ref, vm_ref, sink_ref, tri_ref, o_ref, acc_ref):
    first_step = pl.program_id(1) == 0
    wq = GQA_GROUP * WINDOW
    kj = lax.broadcasted_iota(jnp.int32, (WINDOW, wq), 0)
    qi = lax.broadcasted_iota(jnp.int32, (WINDOW, wq), 1) % WINDOW
    in_cur = kj <= qi
    units = [(blk, h) for blk in range(ATTN_BLOCKS) for h in range(N_KV_HEADS)]

    def kv_rows(ref, prev_ref, blk, h):
        rows = slice(h * HEAD_DIM, (h + 1) * HEAD_DIM)
        cur = ref[rows, blk * WINDOW:(blk + 1) * WINDOW]
        prev = prev_ref[rows, :] if blk == 0 else ref[rows, (blk - 1) * WINDOW:blk * WINDOW]
        return cur, prev

    scores = []
    for blk, h in units:
        q4 = jnp.concatenate(
            [q_ref[(h * GQA_GROUP + r) * HEAD_DIM:(h * GQA_GROUP + r + 1) * HEAD_DIM,
                   blk * WINDOW:(blk + 1) * WINDOW] for r in range(GQA_GROUP)], axis=1)
        k_cur, k_prev = kv_rows(k_ref, kp_ref, blk, h)
        k_meta = km_ref[h * HEAD_DIM:(h + 1) * HEAD_DIM, :]
        scores.append(tuple(lax.dot_general(k, q4, _TN, preferred_element_type=_F32)
                            for k in (k_cur, k_prev, k_meta)))
    for (blk, h), (s_cur, s_prev, s_meta) in zip(units, scores):
        if blk == 0:
            s_prev = jnp.where(first_step, -jnp.inf, s_prev)
        s_sel = jnp.where(in_cur, s_cur, s_prev)
        sink = sink_ref[h]
        m = jnp.maximum(jnp.maximum(jnp.max(s_sel, axis=0, keepdims=True),
                                    jnp.max(s_meta, axis=0, keepdims=True)), sink)
        e_sel = jnp.exp2(s_sel - m)
        e_meta = jnp.exp2(s_meta - m)
        den = (jnp.sum(e_sel, axis=0, keepdims=True) + jnp.sum(e_meta, axis=0, keepdims=True)
               + jnp.exp2(sink - m))
        e_bf = e_sel.astype(_BF16)
        p_cur = e_bf * tri_ref[...]
        p_all = jnp.concatenate([p_cur, e_bf - p_cur, e_meta.astype(_BF16)], axis=0)
        v_cur, v_prev = kv_rows(v_ref, vp_ref, blk, h)
        v_all = jnp.concatenate([v_cur, v_prev, vm_ref[h * HEAD_DIM:(h + 1) * HEAD_DIM, :]], axis=1)
        o = jnp.dot(v_all, p_all, preferred_element_type=_F32) * (1.0 / den)
        for r in range(GQA_GROUP):
            hq = h * GQA_GROUP + r
            acc_ref[hq * HEAD_DIM:(hq + 1) * HEAD_DIM, blk * WINDOW:(blk + 1) * WINDOW] = (
                o[:, r * WINDOW:(r + 1) * WINDOW])
    o_ref[...] = acc_ref[...].T.astype(o_ref.dtype)


def _attention(qkv_t, qkv_meta_t, sink_rows):
    b, _, s = qkv_t.shape
    kj = lax.broadcasted_iota(jnp.int32, (WINDOW, GQA_GROUP * WINDOW), 0)
    qi = lax.broadcasted_iota(jnp.int32, (WINDOW, GQA_GROUP * WINDOW), 1) % WINDOW
    tri = (kj <= qi).astype(_BF16)
    tq = ATTN_BLOCKS * WINDOW
    kblk = Q_W // KV_W
    prev_blk = lambda n: jnp.maximum(n * ATTN_BLOCKS - 1, 0)
    return pl.pallas_call(
        _attn_kernel,
        grid=(b, s // tq),
        in_specs=[
            pl.BlockSpec((None, Q_W, tq), lambda i, n: (i, 0, n)),
            pl.BlockSpec((None, KV_W, tq), lambda i, n: (i, kblk, n)),
            pl.BlockSpec((None, KV_W, tq), lambda i, n: (i, kblk + 1, n)),
            pl.BlockSpec((None, KV_W, WINDOW), lambda i, n: (i, kblk, prev_blk(n))),
            pl.BlockSpec((None, KV_W, WINDOW), lambda i, n: (i, kblk + 1, prev_blk(n))),
            pl.BlockSpec((None, KV_W, N_META), lambda i, n: (0, kblk, 0)),
            pl.BlockSpec((None, KV_W, N_META), lambda i, n: (0, kblk + 1, 0)),
            pl.BlockSpec(sink_rows.shape, lambda i, n: (0, 0, 0)),
            pl.BlockSpec(tri.shape, lambda i, n: (0, 0)),
        ],
        out_specs=pl.BlockSpec((None, tq, Q_W), lambda i, n: (i, n, 0)),
        out_shape=jax.ShapeDtypeStruct((b, s, Q_W), _F32),
        scratch_shapes=[pltpu.VMEM((Q_W, tq), _F32)],
        compiler_params=pltpu.CompilerParams(
            dimension_semantics=("arbitrary", "arbitrary"), vmem_limit_bytes=VMEM_LIMIT),
        name="swa_attention",
    )(qkv_t, qkv_t, qkv_t, qkv_t, qkv_t, qkv_meta_t, qkv_meta_t, sink_rows, tri)


def _tail_kernel(x_hbm, attn_hbm, y_ref, g_ref, wglu_ref, abn_ref, sbn_ref, wout_ref,
                 wfi_ref, wfo_ref, o_hbm, xbuf, abuf, obuf, hn_ref, part_ref, sem_x, sem_a, sem_o,
                 *, ntiles, nj):
    d = D_MODEL
    step = pl.program_id(0)
    slot = step % 2

    def fetch(t, sl):
        bb, jj = t // nj, t % nj
        return (pltpu.make_async_copy(*_row_set_copy(x_hbm, xbuf, sem_x, bb, jj, sl)),
                pltpu.make_async_copy(*_row_set_copy(attn_hbm, abuf, sem_a, bb, jj, sl)))

    def put(t, sl):
        dst, src, sem = _row_set_copy(o_hbm, obuf, sem_o, t // nj, t % nj, sl)
        return pltpu.make_async_copy(src, dst, sem)

    @pl.when(step == 0)
    def _():
        for c in fetch(0, 0):
            c.start()
        hn_ref[...] = jnp.zeros_like(hn_ref)
        part_ref[...] = jnp.zeros_like(part_ref)

    @pl.when(step + 1 < ntiles)
    def _():
        for c in fetch(step + 1, 1 - slot):
            c.start()

    @pl.when(step < ntiles)
    def _():
        for c in fetch(step, slot):
            c.wait()

    @pl.when(step >= 3)
    def _():
        put(step - 3, 1 - slot).wait()

    nrows = xbuf.shape[1]
    pieces = [slice(r, r + TAIL_PIECE) for r in range(0, nrows, TAIL_PIECE)]
    (lo0, hi0), (lo1, hi1) = zip(FFN_EDGES[:-1], FFN_EDGES[1:])

    def ffn_chunk(hn, lo, hi):
        gate = jnp.dot(hn, wfi_ref[:, lo:hi], preferred_element_type=_F32)
        up = jnp.dot(hn, wfi_ref[:, D_FF + lo:D_FF + hi], preferred_element_type=_F32)
        act = (jax.nn.silu(gate) * up).astype(_BF16)
        return jnp.dot(act, wfo_ref[lo:hi, :], preferred_element_type=_F32)

    front = []
    for p in pieces:
        z = jax.nn.gelu(y_ref[:, p]).astype(_BF16)
        front.append(lax.dot_general(z, wglu_ref[...], _TN, preferred_element_type=_F32))
    for p in pieces:
        obuf[1 - slot, p, :] = part_ref[p, :] + ffn_chunk(hn_ref[p, :], lo1, hi1)
    hs = []
    for p, zz in zip(pieces, front):
        ssm = zz[:, :d] * jax.nn.sigmoid(zz[:, d:])
        merged = (jax.nn.sigmoid(g_ref[p, :d].astype(_F32)) * _rms_rows(abuf[slot, p, :], abn_ref[...])
                  + jax.nn.sigmoid(g_ref[p, d:].astype(_F32)) * _rms_rows(ssm, sbn_ref[...]))
        h = xbuf[slot, p, :] + jnp.dot(merged.astype(_BF16), wout_ref[...], preferred_element_type=_F32)
        hs.append((h, _unit_rms(h).astype(_BF16)))
    for p, (h, hn) in zip(pieces, hs):
        part_ref[p, :] = h + ffn_chunk(hn, lo0, hi0)
        hn_ref[p, :] = hn

    @pl.when(step >= 1)
    def _():
        put(step - 1, 1 - slot).start()

    @pl.when(step == ntiles)
    def _():
        put(step - 1, 1 - slot).wait()
        put(step - 2, slot).wait()


def _tail(x, attn, y_t, gates, w_glu, abn, sbn, w_out, w_fi, w_fo):
    b, s, d = x.shape
    nchunks = s // CHUNK
    ntiles = b * CHUNK
    x4 = x.reshape(b, nchunks, CHUNK, d)
    a4 = attn.reshape(b, nchunks, CHUNK, d)
    const = lambda a: pl.BlockSpec(a.shape, lambda t: (0,) * a.ndim, pipeline_mode=pl.Buffered(1))
    hbm = pl.BlockSpec(memory_space=pl.ANY)
    tile = lambda t: jnp.minimum(t, ntiles - 1)
    out = pl.pallas_call(
        functools.partial(_tail_kernel, ntiles=ntiles, nj=CHUNK),
        grid=(ntiles + 1,),
        in_specs=[
            hbm, hbm,
            pl.BlockSpec((None, d, nchunks), lambda t: (tile(t) // CHUNK, 0, tile(t) % CHUNK)),
            pl.BlockSpec((None, nchunks, gates.shape[2]), lambda t: (tile(t) // CHUNK, tile(t) % CHUNK, 0)),
            const(w_glu), const(abn), const(sbn), const(w_out),
            const(w_fi), const(w_fo),
        ],
        out_specs=hbm,
        out_shape=jax.ShapeDtypeStruct(x4.shape, x.dtype),
        scratch_shapes=[pltpu.VMEM((2, nchunks, d), x.dtype), pltpu.VMEM((2, nchunks, d), attn.dtype),
                        pltpu.VMEM((2, nchunks, d), x.dtype),
                        pltpu.VMEM((nchunks, d), _BF16), pltpu.VMEM((nchunks, d), _F32),
                        pltpu.SemaphoreType.DMA((2,)), pltpu.SemaphoreType.DMA((2,)),
                        pltpu.SemaphoreType.DMA((2,))],
        compiler_params=pltpu.CompilerParams(
            dimension_semantics=("arbitrary",), vmem_limit_bytes=VMEM_LIMIT),
        name="tail",
    )(x4, a4, y_t, gates, w_glu, abn, sbn, w_out, w_fi, w_fo)
    return out.reshape(b, s, d)


def kernel(x, meta_tokens, norm_mix, w_in, q_norm, k_norm, attn_sinks, lam_re, lam_im, log_dt,
           ssm_b_re, ssm_b_im, ssm_c_re, ssm_c_im, ssm_d, w_glu, attn_branch_norm, ssm_branch_norm,
           w_out, norm_ffn, w_ffn_in, w_ffn_out):
    seq = x.shape[1]
    levels = (seq // CHUNK).bit_length() - 1
    w = w_in[0] * norm_mix[0].astype(_F32)[:, None]
    w_qkv_t = w[:, :QKV_W].T.astype(_BF16)
    w_u_t = w[:, QKV_W:QKV_W + D_MODEL].T.astype(_BF16)
    w_g = w[:, QKV_W + D_MODEL:].astype(_BF16)
    scale = HEAD_DIM ** -0.5 * LOG2_E
    qk_gain = jnp.concatenate([jnp.tile(q_norm[0] * scale, N_Q_HEADS),
                               jnp.tile(k_norm[0], N_KV_HEADS)])[:, None].astype(_F32)
    no_gain = jnp.zeros((8, 1), _F32)
    sink_rows = jnp.repeat((attn_sinks[0] * LOG2_E).reshape(N_KV_HEADS, 1, GQA_GROUP), WINDOW,
                           axis=2).astype(_F32)
    dt = jnp.exp(log_dt[0].astype(_F32))
    lr, li = lam_re[0].astype(_F32), lam_im[0].astype(_F32)
    dtb = jnp.broadcast_to(dt[:, None], lr.shape)
    colp = jnp.stack([lr, li, dtb, jnp.zeros_like(lr)], axis=2)
    d_grp = ssm_d[0].astype(_F32).reshape(SSM_GROUPS, SSM_GROUP_CH, 1)

    meta = meta_tokens.astype(_F32)[None]
    qkv_meta_t = _proj_t(meta, w_qkv_t, qk_gain, norm_heads=N_Q_HEADS + N_KV_HEADS,
                         tm=N_META, name="proj_qkv_meta")
    u_meta_t = _proj_t(meta, w_u_t, no_gain, norm_heads=0, tm=N_META, name="proj_u_meta")
    a_meta = (u_meta_t[0].astype(_F32).reshape(SSM_GROUPS, SSM_GROUP_CH, CHUNK)
              .transpose(0, 2, 1).reshape(SSM_GROUPS, 1, TOEP))

    u_t, xn, qkv_t, w1, w3, pwr, pwi, s_meta = _projections_and_ssm_prep(
        x, w_u_t, w_qkv_t, qk_gain, N_Q_HEADS + N_KV_HEADS, colp,
        ssm_b_re[0].astype(_F32), ssm_b_im[0].astype(_F32),
        ssm_c_re[0].astype(_F32), ssm_c_im[0].astype(_F32), a_meta, d_grp, levels)
    y_t, gates = _ssm_core_and_gates(u_t, w1, w3, pwr, pwi, s_meta, xn, w_g, levels)
    attn = _attention(qkv_t, qkv_meta_t, sink_rows)
    return _tail(x, attn, y_t, gates, w_glu[0].astype(_BF16), attn_branch_norm,
                 ssm_branch_norm, w_out[0].astype(_BF16),
                 (w_ffn_in[0] * norm_ffn[0].astype(_F32)[:, None]).astype(_BF16),
                 w_ffn_out[0].astype(_BF16))
```

```python
import functools

import jax
import jax.numpy as jnp
from jax import lax
from jax.experimental import pallas as pl
from jax.experimental.pallas import tpu as pltpu

D_MODEL = 1024
N_META = 16
HEAD_DIM = 64
N_Q_HEADS = 16
N_KV_HEADS = 4
GQA_GROUP = N_Q_HEADS // N_KV_HEADS
WINDOW = 128
SSM_GROUP_CH = 16
SSM_GROUPS = D_MODEL // SSM_GROUP_CH
SSM_STATE = 64
D_FF = 2816
Q_W = N_Q_HEADS * HEAD_DIM
KV_W = N_KV_HEADS * HEAD_DIM
QKV_W = Q_W + 2 * KV_W
EPS = 1e-6
LOG2_E = 1.4426950408889634

LANES = 128
CHUNK = 16
TOEP = CHUNK * SSM_GROUP_CH
GROUPS_PER_STEP = 8
MXU_DIM = 256
FFN_EDGES = (0, 6 * MXU_DIM, D_FF)
ATTN_BLOCKS = 8
PROJ_PIECE = 256
TAIL_PIECE = 256
PROJ_U_SETS = 2
VMEM_LIMIT = 56 * 1024 * 1024

_F32 = jnp.float32
_BF16 = jnp.bfloat16
_NT = (((1,), (1,)), ((), ()))
_TN = (((0,), (0,)), ((), ()))
_HI = lax.Precision.HIGHEST


def _rms_rows(x, gain):
    return x * lax.rsqrt(jnp.mean(x * x, axis=-1, keepdims=True) + EPS) * gain


def _unit_rms(x):
    return x * lax.rsqrt(jnp.mean(x * x, axis=-1, keepdims=True) + EPS)


def _proj_rows_stages(xn_ref, w_ref, o_ref):
    tm = min(xn_ref.shape[0], PROJ_PIECE)

    def piece(c):
        rows = slice(c * tm, (c + 1) * tm)
        o_ref[rows, :] = jnp.dot(xn_ref[rows, :], w_ref[...], preferred_element_type=_F32).astype(o_ref.dtype)

    return [functools.partial(piece, c) for c in range(xn_ref.shape[0] // tm)]


def _proj_t_stages(x_ref, w_ref, gain_ref, o_ref, norm_heads, xn_ref=None):
    tm_all = x_ref.shape[0]
    tm = min(tm_all, PROJ_PIECE)

    def piece(c):
        cols = slice(c * tm, (c + 1) * tm)
        xn = _unit_rms(x_ref[cols, :]).astype(_BF16)
        if xn_ref is not None:
            xn_ref[cols, :] = xn
        p = lax.dot_general(w_ref[...], xn, _NT, preferred_element_type=_F32)
        if norm_heads:
            rows = norm_heads * HEAD_DIM
            hd = p[:rows].reshape(norm_heads, HEAD_DIM, tm)
            ms = jnp.mean(hd * hd, axis=1, keepdims=True)
            hd = hd * lax.rsqrt(ms + EPS) * gain_ref[...].reshape(norm_heads, HEAD_DIM, 1)
            o_ref[:rows, cols] = hd.reshape(rows, tm).astype(o_ref.dtype)
            o_ref[rows:, cols] = p[rows:].astype(o_ref.dtype)
        else:
            o_ref[:, cols] = p.astype(o_ref.dtype)

    return [functools.partial(piece, c) for c in range(tm_all // tm)]


def _proj_t_kernel(x_ref, w_ref, gain_ref, o_ref, *, norm_heads):
    for stage in _proj_t_stages(x_ref, w_ref, gain_ref, o_ref, norm_heads):
        stage()


def _proj_t(x, w_t, gain, *, norm_heads, tm, name):
    b, s, d = x.shape
    n = w_t.shape[0]
    return pl.pallas_call(
        functools.partial(_proj_t_kernel, norm_heads=norm_heads),
        grid=(b, s // tm),
        in_specs=[
            pl.BlockSpec((None, tm, d), lambda i, j: (i, j, 0)),
            pl.BlockSpec((n, d), lambda i, j: (0, 0)),
            pl.BlockSpec(gain.shape, lambda i, j: (0, 0)),
        ],
        out_specs=pl.BlockSpec((None, n, tm), lambda i, j: (i, 0, j)),
        out_shape=jax.ShapeDtypeStruct((b, n, s), _BF16),
        compiler_params=pltpu.CompilerParams(
            dimension_semantics=("arbitrary", "arbitrary"), vmem_limit_bytes=VMEM_LIMIT),
        name=name,
    )(x, w_t, gain)


def _row_set_copy(hbm4, buf, sem, bb, jj, slot):
    return hbm4.at[bb, :, jj, :], buf.at[slot], sem.at[slot]


def _fetch_row_sets(x_hbm, xbuf, sem):
    nj = CHUNK // PROJ_U_SETS
    last = pl.num_programs(0) * pl.num_programs(1) - 1
    step = pl.program_id(0) * pl.num_programs(1) + pl.program_id(1)
    slot = step % 2
    nchunks = x_hbm.shape[1]

    def fetch(s, sl):
        return [pltpu.make_async_copy(x_hbm.at[s // nj, :, (s % nj) * PROJ_U_SETS + k, :],
                                      xbuf.at[sl, k * nchunks:(k + 1) * nchunks, :], sem.at[sl, k])
                for k in range(PROJ_U_SETS)]

    @pl.when(step == 0)
    def _():
        for c in fetch(step, slot):
            c.start()

    @pl.when(step < last)
    def _():
        for c in fetch(step + 1, 1 - slot):
            c.start()

    for c in fetch(step, slot):
        c.wait()
    return slot


def _proj_kernel(x_hbm, xnat_ref, wu_ref, wq_ref, gain_ref,
                 colp_ref, bre_ref, bim_ref, cre_ref, cim_ref, am_ref, d_ref,
                 u_ref, xn_ref, qkv_ref, w1_ref, w3_ref, pwr_ref, pwi_ref, sm_ref, xbuf, sem,
                 *, levels, norm_heads):
    slot = _fetch_row_sets(x_hbm, xbuf, sem)
    proj_u = _proj_t_stages(xbuf.at[slot], wu_ref, None, u_ref, 0, xn_ref)
    proj_q = _proj_t_stages(xnat_ref, wq_ref, gain_ref, qkv_ref, norm_heads)
    prep = _ssm_prep_stages(colp_ref, bre_ref, bim_ref, cre_ref, cim_ref, am_ref, d_ref,
                            w1_ref, w3_ref, pwr_ref, pwi_ref, sm_ref, levels)
    mixed = [st for pair in zip(proj_u, proj_q) for st in pair]
    for stage in mixed[:3] + prep[:1] + mixed[3:5] + prep[1:] + mixed[5:]:
        stage()


def _projections_and_ssm_prep(x, wu_t, wq_t, qk_gain, norm_heads, colp, b_re, b_im, c_re, c_im,
                              a_meta, d_grp, levels):
    b, s, d = x.shape
    nu, nq = wu_t.shape[0], wq_t.shape[0]
    nchunks = s // CHUNK
    x4 = x.reshape(b, nchunks, CHUNK, d)
    nj = CHUNK // PROJ_U_SETS
    tm = s // nj
    g, ns, gc = SSM_GROUPS, SSM_STATE, SSM_GROUP_CH
    gpp = g // (b * nj)
    assert gpp * b * nj == g
    gspec = lambda *tail: pl.BlockSpec((gpp,) + tail, lambda i, j: (i * nj + j,) + (0,) * len(tail))
    const = lambda a: pl.BlockSpec(a.shape, lambda i, j: (0,) * a.ndim)
    return pl.pallas_call(
        functools.partial(_proj_kernel, levels=levels, norm_heads=norm_heads),
        grid=(b, nj),
        in_specs=[
            pl.BlockSpec(memory_space=pl.ANY),
            pl.BlockSpec((None, tm, d), lambda i, j: (i, j, 0)),
            const(wu_t), const(wq_t), const(qk_gain),
            gspec(ns, 4), gspec(ns, gc), gspec(ns, gc), gspec(gc, ns), gspec(gc, ns), gspec(1, TOEP),
            gspec(gc, 1),
        ],
        out_specs=[
            pl.BlockSpec((None, nu, PROJ_U_SETS * nchunks), lambda i, j: (i, 0, j)),
            pl.BlockSpec((None, PROJ_U_SETS * nchunks, d), lambda i, j: (i, j, 0)),
            pl.BlockSpec((None, nq, tm), lambda i, j: (i, 0, j)),
            gspec(2 * ns, TOEP), gspec(TOEP, TOEP + 2 * ns), gspec(levels, ns, LANES),
            gspec(levels, ns, LANES), gspec(4, ns, LANES),
        ],
        out_shape=[jax.ShapeDtypeStruct((b, nu, s), _BF16),
                   jax.ShapeDtypeStruct((b, s, d), _BF16),
                   jax.ShapeDtypeStruct((b, nq, s), _BF16),
                   jax.ShapeDtypeStruct((g, 2 * ns, TOEP), _BF16),
                   jax.ShapeDtypeStruct((g, TOEP, TOEP + 2 * ns), _BF16),
                   jax.ShapeDtypeStruct((g, levels, ns, LANES), _F32),
                   jax.ShapeDtypeStruct((g, levels, ns, LANES), _F32),
                   jax.ShapeDtypeStruct((g, 4, ns, LANES), _F32)],
        scratch_shapes=[pltpu.VMEM((2, PROJ_U_SETS * nchunks, d), x.dtype),
                        pltpu.SemaphoreType.DMA((2, PROJ_U_SETS))],
        compiler_params=pltpu.CompilerParams(
            dimension_semantics=("arbitrary", "arbitrary"), vmem_limit_bytes=VMEM_LIMIT),
        name="projections_ssm_prep",
    )(x4, x, wu_t, wq_t, qk_gain, colp, b_re, b_im, c_re, c_im, a_meta, d_grp)


def _ssm_prep_stages(colp_ref, bre_ref, bim_ref, cre_ref, cim_ref, am_ref, d_ref,
                     w1_ref, w3_ref, pwr_ref, pwi_ref, sm_ref, levels):
    ns, gc = SSM_STATE, SSM_GROUP_CH

    def split3(a):
        hi = a.astype(_BF16)
        rest = a - hi.astype(_F32)
        mid = rest.astype(_BF16)
        return hi, mid, (rest - mid.astype(_F32)).astype(_BF16)

    pick_cols = lambda a, m3: jnp.dot(jnp.concatenate(split3(a), axis=1), m3, preferred_element_type=_F32)
    pick_rows = lambda m3, a: jnp.dot(m3, jnp.concatenate(split3(a), axis=0), preferred_element_type=_F32)
    l16 = lax.broadcasted_iota(jnp.int32, (gc, TOEP), 1)
    r16 = lax.broadcasted_iota(jnp.int32, (gc, TOEP), 0)
    rc = lax.broadcasted_iota(jnp.int32, (3 * gc, TOEP), 0)
    lc = lax.broadcasted_iota(jnp.int32, (3 * gc, TOEP), 1)
    tile_c3 = (lc % gc == rc % gc).astype(_BF16)
    tau_r = lax.broadcasted_iota(jnp.int32, (3 * LANES, TOEP), 0) % LANES
    jc_l = lax.broadcasted_iota(jnp.int32, (3 * LANES, TOEP), 1)
    pick_rev3 = (tau_r == CHUNK - 1 - jc_l // gc).astype(_BF16)
    tc_r = lax.broadcasted_iota(jnp.int32, (TOEP, 3 * LANES), 0)
    tau_l = lax.broadcasted_iota(jnp.int32, (TOEP, 3 * LANES), 1) % LANES
    pick_next3 = (tau_l == tc_r // gc + 1).astype(_BF16)
    lane = lax.broadcasted_iota(jnp.int32, (ns, LANES), 1)
    lvl = jnp.clip(lane - CHUNK, 0, levels - 1)
    expo = jnp.where(lane <= CHUNK, lane,
                     jnp.where(lane < CHUNK + levels, CHUNK * jnp.left_shift(1, lvl), 0)).astype(_F32)
    is_tau = lane <= CHUNK

    def group(gl):
        lr, li, dt = colp_ref[gl, :, 0:1], colp_ref[gl, :, 1:2], colp_ref[gl, :, 2:3]
        mag = jnp.exp(lr * dt * expo)
        ang = li * dt * expo
        pw_r, pw_i = mag * jnp.cos(ang), mag * jnp.sin(ang)
        ar, ai = pw_r[:, 1:2], pw_i[:, 1:2]
        den = lr * lr + li * li
        nr, ni = ar - 1.0, ai
        fr, fi = (nr * lr + ni * li) / den, (ni * lr - nr * li) / den
        b_re, b_im = bre_ref[gl], bim_ref[gl]
        bbr_t = pick_cols(fr * b_re - fi * b_im, tile_c3)
        bbi_t = pick_cols(fr * b_im + fi * b_re, tile_c3)
        tau_re, tau_im = jnp.where(is_tau, pw_r, 0.0), jnp.where(is_tau, pw_i, 0.0)
        pr, pi = pick_cols(tau_re, pick_rev3), pick_cols(tau_im, pick_rev3)
        xr = pr * bbr_t - pi * bbi_t
        xi = pr * bbi_t + pi * bbr_t
        krev = (jnp.dot(cre_ref[gl], xr, precision=_HI, preferred_element_type=_F32)
                - jnp.dot(cim_ref[gl], xi, precision=_HI, preferred_element_type=_F32))
        krev = krev + jnp.where(l16 - (CHUNK - 1) * gc == r16, d_ref[gl], 0.0)
        rz = jnp.concatenate([krev, jnp.zeros_like(krev)], axis=1)
        toep = jnp.concatenate(
            [rz[:, (CHUNK - 1 - t) * gc:(CHUNK - 1 - t) * gc + TOEP] for t in range(CHUNK)], axis=0)
        w1_ref[gl] = jnp.concatenate([xr, xi], axis=0).astype(w1_ref.dtype)
        gar = pick_rows(pick_next3, tau_re.T)
        gai = pick_rows(pick_next3, tau_im.T)
        crt = jnp.tile(cre_ref[gl], (CHUNK, 1))
        cit = jnp.tile(cim_ref[gl], (CHUNK, 1))
        w3_ref[gl] = jnp.concatenate([toep, crt * gar - cit * gai, -(crt * gai + cit * gar)],
                                     axis=1).astype(w3_ref.dtype)
        for l in range(levels):
            pwr_ref[gl, l] = jnp.broadcast_to(pw_r[:, CHUNK + l:CHUNK + l + 1], (ns, LANES))
            pwi_ref[gl, l] = jnp.broadcast_to(pw_i[:, CHUNK + l:CHUNK + l + 1], (ns, LANES))
        am = am_ref[gl]
        sr = jnp.sum(xr * am, axis=1, keepdims=True)
        si = jnp.sum(xi * am, axis=1, keepdims=True)
        a16r, a16i = pw_r[:, CHUNK:CHUNK + 1], pw_i[:, CHUNK:CHUNK + 1]
        sm_ref[gl, 0] = jnp.broadcast_to(sr, (ns, LANES))
        sm_ref[gl, 1] = jnp.broadcast_to(si, (ns, LANES))
        sm_ref[gl, 2] = jnp.broadcast_to(a16r * sr - a16i * si, (ns, LANES))
        sm_ref[gl, 3] = jnp.broadcast_to(a16r * si + a16i * sr, (ns, LANES))

    return [functools.partial(group, gl) for gl in range(colp_ref.shape[0])]


def _ssm_stages(u_ref, w1_ref, w3_ref, pwr_ref, pwi_ref, sm_ref, y_ref, bre_ref, bim_ref, nchunks, levels):
    ns, gc, gps = SSM_STATE, SSM_GROUP_CH, GROUPS_PER_STEP
    nblk = nchunks // LANES
    blocks = [slice(h * LANES, (h + 1) * LANES) for h in range(nblk)]
    group_rows = [slice(gl * ns, (gl + 1) * ns) for gl in range(gps)]

    def chunk_operand(gl):
        r0 = gl * gc
        return jnp.concatenate(
            [u_ref[r0:r0 + gc, j * nchunks:(j + 1) * nchunks] for j in range(CHUNK)], axis=0)

    def shifted(vals, sh, first):
        if sh % LANES == 0:
            k = sh // LANES
            return [None if h < k else vals[h - k] for h in range(nblk)]
        lane = lax.broadcasted_iota(jnp.int32, (ns, LANES), 1)
        rot = [pltpu.roll(v, sh, axis=1) for v in vals]
        keep = lane >= sh
        return [jnp.where(keep, rot[h], first if h == 0 else rot[h - 1]) for h in range(nblk)]

    def state_increments():
        lane0 = lax.broadcasted_iota(jnp.int32, (ns, LANES), 1) == 0
        for gl, rs in enumerate(group_rows):
            r1 = jnp.dot(w1_ref[gl], chunk_operand(gl), preferred_element_type=_F32)
            bre_ref[rs, :] = r1[:ns]
            bim_ref[rs, :] = r1[ns:]
            bre_ref[rs, blocks[0]] += jnp.where(lane0, sm_ref[gl, 2], 0.0)
            bim_ref[rs, blocks[0]] += jnp.where(lane0, sm_ref[gl, 3], 0.0)

    def scan_level(lvl):
        for gl, rs in enumerate(group_rows):
            p_re, p_im = pwr_ref[gl, lvl], pwi_ref[gl, lvl]
            s_re = [bre_ref[rs, blk] for blk in blocks]
            s_im = [bim_ref[rs, blk] for blk in blocks]
            t_re, t_im = shifted(s_re, 1 << lvl, 0.0), shifted(s_im, 1 << lvl, 0.0)
            for h in range(nblk):
                if t_re[h] is not None:
                    bre_ref[rs, blocks[h]] = s_re[h] + p_re * t_re[h] - p_im * t_im[h]
                    bim_ref[rs, blocks[h]] = s_im[h] + p_re * t_im[h] + p_im * t_re[h]

    def entering_states():
        for gl, rs in enumerate(group_rows):
            prev_re = shifted([bre_ref[rs, blk] for blk in blocks], 1, sm_ref[gl, 0])
            prev_im = shifted([bim_ref[rs, blk] for blk in blocks], 1, sm_ref[gl, 1])
            for h in range(nblk):
                bre_ref[rs, blocks[h]] = prev_re[h]
                bim_ref[rs, blocks[h]] = prev_im[h]

    def outputs(gl):
        r0, rs = gl * gc, group_rows[gl]
        rhs = jnp.concatenate([chunk_operand(gl), bre_ref[rs, :].astype(_BF16),
                               bim_ref[rs, :].astype(_BF16)], axis=0)
        y = jnp.dot(w3_ref[gl], rhs, preferred_element_type=_F32)
        for t in range(CHUNK):
            y_ref[r0:r0 + gc, t * nchunks:(t + 1) * nchunks] = y[t * gc:(t + 1) * gc, :].astype(y_ref.dtype)

    return ([state_increments] + [functools.partial(scan_level, lvl) for lvl in range(levels)]
            + [entering_states] + [functools.partial(outputs, gl) for gl in range(gps)])


def _ssm_gates_kernel(u_ref, w1_ref, w3_ref, pwr_ref, pwi_ref, sm_ref, xn_ref, wg_ref,
                      y_ref, g_ref, bre_ref, bim_ref, *, nchunks, levels):
    ssm = _ssm_stages(u_ref, w1_ref, w3_ref, pwr_ref, pwi_ref, sm_ref, y_ref, bre_ref, bim_ref,
                      nchunks, levels)
    proj = _proj_rows_stages(xn_ref, wg_ref, g_ref)
    every = -(-len(ssm) // len(proj))
    for i, stage in enumerate(ssm):
        if i % every == 0 and proj:
            proj.pop(0)()
        stage()
    for stage in proj:
        stage()


def _ssm_core_and_gates(u_t, w1, w3, pwr, pwi, s_meta, xn, w_gates, levels):
    b, n, s = u_t.shape
    gps, ns = GROUPS_PER_STEP, SSM_STATE
    rows = gps * SSM_GROUP_CH
    nchunks = s // CHUNK
    d = xn.shape[2]
    tm = s // (n // rows)
    tiles = s // tm
    wspec = lambda *tail: pl.BlockSpec((gps,) + tail, lambda gb, i: (gb,) + (0,) * len(tail))
    tile = lambda gb, i: ((gb * b + i) // tiles, (gb * b + i) % tiles, 0)
    return pl.pallas_call(
        functools.partial(_ssm_gates_kernel, nchunks=nchunks, levels=levels),
        grid=(n // rows, b),
        in_specs=[
            pl.BlockSpec((None, rows, s), lambda gb, i: (i, gb, 0)),
            wspec(2 * ns, TOEP), wspec(TOEP, TOEP + 2 * ns),
            wspec(levels, ns, LANES), wspec(levels, ns, LANES), wspec(4, ns, LANES),
            pl.BlockSpec((None, tm, d), tile),
            pl.BlockSpec(w_gates.shape, lambda gb, i: (0, 0)),
        ],
        out_specs=[
            pl.BlockSpec((None, rows, s), lambda gb, i: (i, gb, 0)),
            pl.BlockSpec((None, tm, w_gates.shape[1]), tile),
        ],
        out_shape=[jax.ShapeDtypeStruct((b, n, s), _BF16),
                   jax.ShapeDtypeStruct((b, s, w_gates.shape[1]), _BF16)],
        scratch_shapes=[pltpu.VMEM((gps * ns, nchunks), _F32), pltpu.VMEM((gps * ns, nchunks), _F32)],
        compiler_params=pltpu.CompilerParams(
            dimension_semantics=("arbitrary", "arbitrary"), vmem_limit_bytes=VMEM_LIMIT),
        name="ssm_core_gates",
    )(u_t, w1, w3, pwr, pwi, s_meta, xn, w_gates)


def _attn_kernel(q_ref, k_ref, v_ref, kp_ref, vp_ref, km_ref, vm_ref, sink_ref, tri_ref, o_ref, acc_ref):
    first_step = pl.program_id(1) == 0
    wq = GQA_GROUP * WINDOW
    kj = lax.broadcasted_iota(jnp.int32, (WINDOW, wq), 0)
    qi = lax.broadcasted_iota(jnp.int32, (WINDOW, wq), 1) % WINDOW
    in_cur = kj <= qi
    units = [(blk, h) for blk in range(ATTN_BLOCKS) for h in range(N_KV_HEADS)]

    def kv_rows(ref, prev_ref, blk, h):
        rows = slice(h * HEAD_DIM, (h + 1) * HEAD_DIM)
        cur = ref[rows, blk * WINDOW:(blk + 1) * WINDOW]
        prev = prev_ref[rows, :] if blk == 0 else ref[rows, (blk - 1) * WINDOW:blk * WINDOW]
        return cur, prev

    scores = []
    for blk, h in units:
        q4 = jnp.concatenate(
            [q_ref[(h * GQA_GROUP + r) * HEAD_DIM:(h * GQA_GROUP + r + 1) * HEAD_DIM,
                   blk * WINDOW:(blk + 1) * WINDOW] for r in range(GQA_GROUP)], axis=1)
        k_cur, k_prev = kv_rows(k_ref, kp_ref, blk, h)
        k_meta = km_ref[h * HEAD_DIM:(h + 1) * HEAD_DIM, :]
        scores.append(tuple(lax.dot_general(k, q4, _TN, preferred_element_type=_F32)
                            for k in (k_cur, k_prev, k_meta)))
    for (blk, h), (s_cur, s_prev, s_meta) in zip(units, scores):
        if blk == 0:
            s_prev = jnp.where(first_step, -jnp.inf, s_prev)
        s_sel = jnp.where(in_cur, s_cur, s_prev)
        sink = sink_ref[h]
        m = jnp.maximum(jnp.maximum(jnp.max(s_sel, axis=0, keepdims=True),
                                    jnp.max(s_meta, axis=0, keepdims=True)), sink)
        e_sel = jnp.exp2(s_sel - m)
        e_meta = jnp.exp2(s_meta - m)
        den = (jnp.sum(e_sel, axis=0, keepdims=True) + jnp.sum(e_meta, axis=0, keepdims=True)
               + jnp.exp2(sink - m))
        e_bf = e_sel.astype(_BF16)
        p_cur = e_bf * tri_ref[...]
        p_all = jnp.concatenate([p_cur, e_bf - p_cur, e_meta.astype(_BF16)], axis=0)
        v_cur, v_prev = kv_rows(v_ref, vp_ref, blk, h)
        v_all = jnp.concatenate([v_cur, v_prev, vm_ref[h * HEAD_DIM:(h + 1) * HEAD_DIM, :]], axis=1)
        o = jnp.dot(v_all, p_all, preferred_element_type=_F32) * (1.0 / den)
        for r in range(GQA_GROUP):
            hq = h * GQA_GROUP + r
            acc_ref[hq * HEAD_DIM:(hq + 1) * HEAD_DIM, blk * WINDOW:(blk + 1) * WINDOW] = (
                o[:, r * WINDOW:(r + 1) * WINDOW])
    o_ref[...] = acc_ref[...].T.astype(o_ref.dtype)


def _attention(qkv_t, qkv_meta_t, sink_rows):
    b, _, s = qkv_t.shape
    kj = lax.broadcasted_iota(jnp.int32, (WINDOW, GQA_GROUP * WINDOW), 0)
    qi = lax.broadcasted_iota(jnp.int32, (WINDOW, GQA_GROUP * WINDOW), 1) % WINDOW
    tri = (kj <= qi).astype(_BF16)
    tq = ATTN_BLOCKS * WINDOW
    kblk = Q_W // KV_W
    prev_blk = lambda n: jnp.maximum(n * ATTN_BLOCKS - 1, 0)
    return pl.pallas_call(
        _attn_kernel,
        grid=(b, s // tq),
        in_specs=[
            pl.BlockSpec((None, Q_W, tq), lambda i, n: (i, 0, n)),
            pl.BlockSpec((None, KV_W, tq), lambda i, n: (i, kblk, n)),
            pl.BlockSpec((None, KV_W, tq), lambda i, n: (i, kblk + 1, n)),
            pl.BlockSpec((None, KV_W, WINDOW), lambda i, n: (i, kblk, prev_blk(n))),
            pl.BlockSpec((None, KV_W, WINDOW), lambda i, n: (i, kblk + 1, prev_blk(n))),
            pl.BlockSpec((None, KV_W, N_META), lambda i, n: (0, kblk, 0)),
            pl.BlockSpec((None, KV_W, N_META), lambda i, n: (0, kblk + 1, 0)),
            pl.BlockSpec(sink_rows.shape, lambda i, n: (0, 0, 0)),
            pl.BlockSpec(tri.shape, lambda i, n: (0, 0)),
        ],
        out_specs=pl.BlockSpec((None, tq, Q_W), lambda i, n: (i, n, 0)),
        out_shape=jax.ShapeDtypeStruct((b, s, Q_W), _F32),
        scratch_shapes=[pltpu.VMEM((Q_W, tq), _F32)],
        compiler_params=pltpu.CompilerParams(
            dimension_semantics=("arbitrary", "arbitrary"), vmem_limit_bytes=VMEM_LIMIT),
        name="swa_attention",
    )(qkv_t, qkv_t, qkv_t, qkv_t, qkv_t, qkv_meta_t, qkv_meta_t, sink_rows, tri)


def _tail_kernel(x_hbm, attn_hbm, y_ref, g_ref, wglu_ref, abn_ref, sbn_ref, wout_ref,
                 wfi_ref, wfo_ref, o_hbm, xbuf, abuf, obuf, hn_ref, part_ref, sem_x, sem_a, sem_o,
                 *, ntiles, nj):
    d = D_MODEL
    step = pl.program_id(0)
    slot = step % 2

    def fetch(t, sl):
        bb, jj = t // nj, t % nj
        return (pltpu.make_async_copy(*_row_set_copy(x_hbm, xbuf, sem_x, bb, jj, sl)),
                pltpu.make_async_copy(*_row_set_copy(attn_hbm, abuf, sem_a, bb, jj, sl)))

    def put(t, sl):
        dst, src, sem = _row_set_copy(o_hbm, obuf, sem_o, t // nj, t % nj, sl)
        return pltpu.make_async_copy(src, dst, sem)

    @pl.when(step == 0)
    def _():
        for c in fetch(0, 0):
            c.start()
        hn_ref[...] = jnp.zeros_like(hn_ref)
        part_ref[...] = jnp.zeros_like(part_ref)

    @pl.when(step + 1 < ntiles)
    def _():
        for c in fetch(step + 1, 1 - slot):
            c.start()

    @pl.when(step < ntiles)
    def _():
        for c in fetch(step, slot):
            c.wait()

    @pl.when(step >= 3)
    def _():
        put(step - 3, 1 - slot).wait()

    nrows = xbuf.shape[1]
    pieces = [slice(r, r + TAIL_PIECE) for r in range(0, nrows, TAIL_PIECE)]
    (lo0, hi0), (lo1, hi1) = zip(FFN_EDGES[:-1], FFN_EDGES[1:])

    def ffn_chunk(hn, lo, hi):
        gate = jnp.dot(hn, wfi_ref[:, lo:hi], preferred_element_type=_F32)
        up = jnp.dot(hn, wfi_ref[:, D_FF + lo:D_FF + hi], preferred_element_type=_F32)
        act = (jax.nn.silu(gate) * up).astype(_BF16)
        return jnp.dot(act, wfo_ref[lo:hi, :], preferred_element_type=_F32)

    front = []
    for p in pieces:
        z = jax.nn.gelu(y_ref[:, p].astype(_F32)).astype(_BF16)
        front.append(lax.dot_general(z, wglu_ref[...], _TN, preferred_element_type=_F32))
    for p in pieces:
        obuf[1 - slot, p, :] = part_ref[p, :] + ffn_chunk(hn_ref[p, :], lo1, hi1)
    hs = []
    for p, zz in zip(pieces, front):
        ssm = zz[:, :d] * jax.nn.sigmoid(zz[:, d:])
        merged = (jax.nn.sigmoid(g_ref[p, :d].astype(_F32)) * _rms_rows(abuf[slot, p, :], abn_ref[...])
                  + jax.nn.sigmoid(g_ref[p, d:].astype(_F32)) * _rms_rows(ssm, sbn_ref[...]))
        h = xbuf[slot, p, :] + jnp.dot(merged.astype(_BF16), wout_ref[...], preferred_element_type=_F32)
        hs.append((h, _unit_rms(h).astype(_BF16)))
    for p, (h, hn) in zip(pieces, hs):
        part_ref[p, :] = h + ffn_chunk(hn, lo0, hi0)
        hn_ref[p, :] = hn

    @pl.when(step >= 1)
    def _():
        put(step - 1, 1 - slot).start()

    @pl.when(step == ntiles)
    def _():
        put(step - 1, 1 - slot).wait()
        put(step - 2, slot).wait()


def _tail(x, attn, y_t, gates, w_glu, abn, sbn, w_out, w_fi, w_fo):
    b, s, d = x.shape
    nchunks = s // CHUNK
    ntiles = b * CHUNK
    x4 = x.reshape(b, nchunks, CHUNK, d)
    a4 = attn.reshape(b, nchunks, CHUNK, d)
    const = lambda a: pl.BlockSpec(a.shape, lambda t: (0,) * a.ndim, pipeline_mode=pl.Buffered(1))
    hbm = pl.BlockSpec(memory_space=pl.ANY)
    tile = lambda t: jnp.minimum(t, ntiles - 1)
    out = pl.pallas_call(
        functools.partial(_tail_kernel, ntiles=ntiles, nj=CHUNK),
        grid=(ntiles + 1,),
        in_specs=[
            hbm, hbm,
            pl.BlockSpec((None, d, nchunks), lambda t: (tile(t) // CHUNK, 0, tile(t) % CHUNK)),
            pl.BlockSpec((None, nchunks, gates.shape[2]), lambda t: (tile(t) // CHUNK, tile(t) % CHUNK, 0)),
            const(w_glu), const(abn), const(sbn), const(w_out),
            const(w_fi), const(w_fo),
        ],
        out_specs=hbm,
        out_shape=jax.ShapeDtypeStruct(x4.shape, x.dtype),
        scratch_shapes=[pltpu.VMEM((2, nchunks, d), x.dtype), pltpu.VMEM((2, nchunks, d), attn.dtype),
                        pltpu.VMEM((2, nchunks, d), x.dtype),
                        pltpu.VMEM((nchunks, d), _BF16), pltpu.VMEM((nchunks, d), _F32),
                        pltpu.SemaphoreType.DMA((2,)), pltpu.SemaphoreType.DMA((2,)),
                        pltpu.SemaphoreType.DMA((2,))],
        compiler_params=pltpu.CompilerParams(
            dimension_semantics=("arbitrary",), vmem_limit_bytes=VMEM_LIMIT),
        name="tail",
    )(x4, a4, y_t, gates, w_glu, abn, sbn, w_out, w_fi, w_fo)
    return out.reshape(b, s, d)


def kernel(x, meta_tokens, norm_mix, w_in, q_norm, k_norm, attn_sinks, lam_re, lam_im, log_dt,
           ssm_b_re, ssm_b_im, ssm_c_re, ssm_c_im, ssm_d, w_glu, attn_branch_norm, ssm_branch_norm,
           w_out, norm_ffn, w_ffn_in, w_ffn_out):
    seq = x.shape[1]
    levels = (seq // CHUNK).bit_length() - 1
    w = w_in[0] * norm_mix[0].astype(_F32)[:, None]
    w_qkv_t = w[:, :QKV_W].T.astype(_BF16)
    w_u_t = w[:, QKV_W:QKV_W + D_MODEL].T.astype(_BF16)
    w_g = w[:, QKV_W + D_MODEL:].astype(_BF16)
    scale = HEAD_DIM ** -0.5 * LOG2_E
    qk_gain = jnp.concatenate([jnp.tile(q_norm[0] * scale, N_Q_HEADS),
                               jnp.tile(k_norm[0], N_KV_HEADS)])[:, None].astype(_F32)
    no_gain = jnp.zeros((8, 1), _F32)
    sink_rows = jnp.repeat((attn_sinks[0] * LOG2_E).reshape(N_KV_HEADS, 1, GQA_GROUP), WINDOW,
                           axis=2).astype(_F32)
    dt = jnp.exp(log_dt[0].astype(_F32))
    lr, li = lam_re[0].astype(_F32), lam_im[0].astype(_F32)
    dtb = jnp.broadcast_to(dt[:, None], lr.shape)
    colp = jnp.stack([lr, li, dtb, jnp.zeros_like(lr)], axis=2)
    d_grp = ssm_d[0].astype(_F32).reshape(SSM_GROUPS, SSM_GROUP_CH, 1)

    meta = meta_tokens.astype(_F32)[None]
    qkv_meta_t = _proj_t(meta, w_qkv_t, qk_gain, norm_heads=N_Q_HEADS + N_KV_HEADS,
                         tm=N_META, name="proj_qkv_meta")
    u_meta_t = _proj_t(meta, w_u_t, no_gain, norm_heads=0, tm=N_META, name="proj_u_meta")
    a_meta = (u_meta_t[0].astype(_F32).reshape(SSM_GROUPS, SSM_GROUP_CH, CHUNK)
              .transpose(0, 2, 1).reshape(SSM_GROUPS, 1, TOEP))

    u_t, xn, qkv_t, w1, w3, pwr, pwi, s_meta = _projections_and_ssm_prep(
        x, w_u_t, w_qkv_t, qk_gain, N_Q_HEADS + N_KV_HEADS, colp,
        ssm_b_re[0].astype(_F32), ssm_b_im[0].astype(_F32),
        ssm_c_re[0].astype(_F32), ssm_c_im[0].astype(_F32), a_meta, d_grp, levels)
    y_t, gates = _ssm_core_and_gates(u_t, w1, w3, pwr, pwi, s_meta, xn, w_g, levels)
    attn = _attention(qkv_t, qkv_meta_t, sink_rows)
    return _tail(x, attn, y_t, gates, w_glu[0].astype(_BF16), attn_branch_norm,
                 ssm_branch_norm, w_out[0].astype(_BF16),
                 (w_ffn_in[0] * norm_ffn[0].astype(_F32)[:, None]).astype(_BF16),
                 w_ffn_out[0].astype(_BF16))
```

```python
import functools

import jax
import jax.numpy as jnp
from jax import lax
from jax.experimental import pallas as pl
from jax.experimental.pallas import tpu as pltpu

D_MODEL = 1024
N_META = 16
HEAD_DIM = 64
N_Q_HEADS = 16
N_KV_HEADS = 4
GQA_GROUP = N_Q_HEADS // N_KV_HEADS
WINDOW = 128
SSM_GROUP_CH = 16
SSM_GROUPS = D_MODEL // SSM_GROUP_CH
SSM_STATE = 64
D_FF = 2816
Q_W = N_Q_HEADS * HEAD_DIM
KV_W = N_KV_HEADS * HEAD_DIM
QKV_W = Q_W + 2 * KV_W
EPS = 1e-6
LOG2_E = 1.4426950408889634

LANES = 128
CHUNK = 16
TOEP = CHUNK * SSM_GROUP_CH
GROUPS_PER_STEP = 8
MXU_DIM = 256
FFN_EDGES = (0, 6 * MXU_DIM, D_FF)
ATTN_BLOCKS = 16
PROJ_PIECE = 256
TAIL_PIECE = 256
PROJ_U_SETS = 2
VMEM_LIMIT = 56 * 1024 * 1024

_F32 = jnp.float32
_BF16 = jnp.bfloat16
_NT = (((1,), (1,)), ((), ()))
_TN = (((0,), (0,)), ((), ()))
_HI = lax.Precision.HIGHEST


def _rms_rows(x, gain):
    return x * lax.rsqrt(jnp.mean(x * x, axis=-1, keepdims=True) + EPS) * gain


def _unit_rms(x):
    return x * lax.rsqrt(jnp.mean(x * x, axis=-1, keepdims=True) + EPS)


def _proj_rows_stages(xn_ref, w_ref, o_ref):
    tm = min(xn_ref.shape[0], PROJ_PIECE)

    def piece(c):
        rows = slice(c * tm, (c + 1) * tm)
        o_ref[rows, :] = jnp.dot(xn_ref[rows, :], w_ref[...], preferred_element_type=_F32).astype(o_ref.dtype)

    return [functools.partial(piece, c) for c in range(xn_ref.shape[0] // tm)]


def _proj_t_stages(x_ref, w_ref, gain_ref, o_ref, norm_heads, xn_ref=None):
    tm_all = x_ref.shape[0]
    tm = min(tm_all, PROJ_PIECE)

    def piece(c):
        cols = slice(c * tm, (c + 1) * tm)
        xn = _unit_rms(x_ref[cols, :]).astype(_BF16)
        if xn_ref is not None:
            xn_ref[cols, :] = xn
        p = lax.dot_general(w_ref[...], xn, _NT, preferred_element_type=_F32)
        if norm_heads:
            rows = norm_heads * HEAD_DIM
            hd = p[:rows].reshape(norm_heads, HEAD_DIM, tm)
            ms = jnp.mean(hd * hd, axis=1, keepdims=True)
            hd = hd * lax.rsqrt(ms + EPS) * gain_ref[...].reshape(norm_heads, HEAD_DIM, 1)
            o_ref[:rows, cols] = hd.reshape(rows, tm).astype(o_ref.dtype)
            o_ref[rows:, cols] = p[rows:].astype(o_ref.dtype)
        else:
            o_ref[:, cols] = p.astype(o_ref.dtype)

    return [functools.partial(piece, c) for c in range(tm_all // tm)]


def _proj_t_kernel(x_ref, w_ref, gain_ref, o_ref, *, norm_heads):
    for stage in _proj_t_stages(x_ref, w_ref, gain_ref, o_ref, norm_heads):
        stage()


def _proj_t(x, w_t, gain, *, norm_heads, tm, name):
    b, s, d = x.shape
    n = w_t.shape[0]
    return pl.pallas_call(
        functools.partial(_proj_t_kernel, norm_heads=norm_heads),
        grid=(b, s // tm),
        in_specs=[
            pl.BlockSpec((None, tm, d), lambda i, j: (i, j, 0)),
            pl.BlockSpec((n, d), lambda i, j: (0, 0)),
            pl.BlockSpec(gain.shape, lambda i, j: (0, 0)),
        ],
        out_specs=pl.BlockSpec((None, n, tm), lambda i, j: (i, 0, j)),
        out_shape=jax.ShapeDtypeStruct((b, n, s), _BF16),
        compiler_params=pltpu.CompilerParams(
            dimension_semantics=("arbitrary", "arbitrary"), vmem_limit_bytes=VMEM_LIMIT),
        name=name,
    )(x, w_t, gain)


def _row_set_copy(hbm4, buf, sem, bb, jj, slot):
    return hbm4.at[bb, :, jj, :], buf.at[slot], sem.at[slot]


def _fetch_row_sets(x_hbm, xbuf, sem):
    nj = CHUNK // PROJ_U_SETS
    last = pl.num_programs(0) * pl.num_programs(1) - 1
    step = pl.program_id(0) * pl.num_programs(1) + pl.program_id(1)
    slot = step % 2
    nchunks = x_hbm.shape[1]

    def fetch(s, sl):
        return [pltpu.make_async_copy(x_hbm.at[s // nj, :, (s % nj) * PROJ_U_SETS + k, :],
                                      xbuf.at[sl, k * nchunks:(k + 1) * nchunks, :], sem.at[sl, k])
                for k in range(PROJ_U_SETS)]

    @pl.when(step == 0)
    def _():
        for c in fetch(step, slot):
            c.start()

    @pl.when(step < last)
    def _():
        for c in fetch(step + 1, 1 - slot):
            c.start()

    for c in fetch(step, slot):
        c.wait()
    return slot


def _proj_kernel(x_hbm, xnat_ref, wu_ref, wq_ref, gain_ref,
                 colp_ref, bre_ref, bim_ref, cre_ref, cim_ref, am_ref, d_ref,
                 u_ref, xn_ref, qkv_ref, w1_ref, w3_ref, pwr_ref, pwi_ref, sm_ref, xbuf, sem,
                 *, levels, norm_heads):
    slot = _fetch_row_sets(x_hbm, xbuf, sem)
    proj_u = _proj_t_stages(xbuf.at[slot], wu_ref, None, u_ref, 0, xn_ref)
    proj_q = _proj_t_stages(xnat_ref, wq_ref, gain_ref, qkv_ref, norm_heads)
    prep = _ssm_prep_stages(colp_ref, bre_ref, bim_ref, cre_ref, cim_ref, am_ref, d_ref,
                            w1_ref, w3_ref, pwr_ref, pwi_ref, sm_ref, levels)
    mixed = [st for pair in zip(proj_u, proj_q) for st in pair]
    for stage in mixed[:3] + prep[:1] + mixed[3:5] + prep[1:] + mixed[5:]:
        stage()


def _projections_and_ssm_prep(x, wu_t, wq_t, qk_gain, norm_heads, colp, b_re, b_im, c_re, c_im,
                              a_meta, d_grp, levels):
    b, s, d = x.shape
    nu, nq = wu_t.shape[0], wq_t.shape[0]
    nchunks = s // CHUNK
    x4 = x.reshape(b, nchunks, CHUNK, d)
    nj = CHUNK // PROJ_U_SETS
    tm = s // nj
    g, ns, gc = SSM_GROUPS, SSM_STATE, SSM_GROUP_CH
    gpp = g // (b * nj)
    assert gpp * b * nj == g
    gspec = lambda *tail: pl.BlockSpec((gpp,) + tail, lambda i, j: (i * nj + j,) + (0,) * len(tail))
    const = lambda a: pl.BlockSpec(a.shape, lambda i, j: (0,) * a.ndim)
    return pl.pallas_call(
        functools.partial(_proj_kernel, levels=levels, norm_heads=norm_heads),
        grid=(b, nj),
        in_specs=[
            pl.BlockSpec(memory_space=pl.ANY),
            pl.BlockSpec((None, tm, d), lambda i, j: (i, j, 0)),
            const(wu_t), const(wq_t), const(qk_gain),
            gspec(ns, 4), gspec(ns, gc), gspec(ns, gc), gspec(gc, ns), gspec(gc, ns), gspec(1, TOEP),
            gspec(gc, 1),
        ],
        out_specs=[
            pl.BlockSpec((None, nu, PROJ_U_SETS * nchunks), lambda i, j: (i, 0, j)),
            pl.BlockSpec((None, PROJ_U_SETS * nchunks, d), lambda i, j: (i, j, 0)),
            pl.BlockSpec((None, nq, tm), lambda i, j: (i, 0, j)),
            gspec(2 * ns, TOEP), gspec(TOEP, TOEP + 2 * ns), gspec(levels, ns, LANES),
            gspec(levels, ns, LANES), gspec(4, ns, LANES),
        ],
        out_shape=[jax.ShapeDtypeStruct((b, nu, s), _BF16),
                   jax.ShapeDtypeStruct((b, s, d), _BF16),
                   jax.ShapeDtypeStruct((b, nq, s), _BF16),
                   jax.ShapeDtypeStruct((g, 2 * ns, TOEP), _BF16),
                   jax.ShapeDtypeStruct((g, TOEP, TOEP + 2 * ns), _BF16),
                   jax.ShapeDtypeStruct((g, levels, ns, LANES), _F32),
                   jax.ShapeDtypeStruct((g, levels, ns, LANES), _F32),
                   jax.ShapeDtypeStruct((g, 4, ns, LANES), _F32)],
        scratch_shapes=[pltpu.VMEM((2, PROJ_U_SETS * nchunks, d), x.dtype),
                        pltpu.SemaphoreType.DMA((2, PROJ_U_SETS))],
        compiler_params=pltpu.CompilerParams(
            dimension_semantics=("arbitrary", "arbitrary"), vmem_limit_bytes=VMEM_LIMIT),
        name="projections_ssm_prep",
    )(x4, x, wu_t, wq_t, qk_gain, colp, b_re, b_im, c_re, c_im, a_meta, d_grp)


def _ssm_prep_stages(colp_ref, bre_ref, bim_ref, cre_ref, cim_ref, am_ref, d_ref,
                     w1_ref, w3_ref, pwr_ref, pwi_ref, sm_ref, levels):
    ns, gc = SSM_STATE, SSM_GROUP_CH

    def split3(a):
        hi = a.astype(_BF16)
        rest = a - hi.astype(_F32)
        mid = rest.astype(_BF16)
        return hi, mid, (rest - mid.astype(_F32)).astype(_BF16)

    pick_cols = lambda a, m3: jnp.dot(jnp.concatenate(split3(a), axis=1), m3, preferred_element_type=_F32)
    pick_rows = lambda m3, a: jnp.dot(m3, jnp.concatenate(split3(a), axis=0), preferred_element_type=_F32)
    l16 = lax.broadcasted_iota(jnp.int32, (gc, TOEP), 1)
    r16 = lax.broadcasted_iota(jnp.int32, (gc, TOEP), 0)
    rc = lax.broadcasted_iota(jnp.int32, (3 * gc, TOEP), 0)
    lc = lax.broadcasted_iota(jnp.int32, (3 * gc, TOEP), 1)
    tile_c3 = (lc % gc == rc % gc).astype(_BF16)
    tau_r = lax.broadcasted_iota(jnp.int32, (3 * LANES, TOEP), 0) % LANES
    jc_l = lax.broadcasted_iota(jnp.int32, (3 * LANES, TOEP), 1)
    pick_rev3 = (tau_r == CHUNK - 1 - jc_l // gc).astype(_BF16)
    tc_r = lax.broadcasted_iota(jnp.int32, (TOEP, 3 * LANES), 0)
    tau_l = lax.broadcasted_iota(jnp.int32, (TOEP, 3 * LANES), 1) % LANES
    pick_next3 = (tau_l == tc_r // gc + 1).astype(_BF16)
    lane = lax.broadcasted_iota(jnp.int32, (ns, LANES), 1)
    lvl = jnp.clip(lane - CHUNK, 0, levels - 1)
    expo = jnp.where(lane <= CHUNK, lane,
                     jnp.where(lane < CHUNK + levels, CHUNK * jnp.left_shift(1, lvl), 0)).astype(_F32)
    is_tau = lane <= CHUNK

    def group(gl):
        lr, li, dt = colp_ref[gl, :, 0:1], colp_ref[gl, :, 1:2], colp_ref[gl, :, 2:3]
        mag = jnp.exp(lr * dt * expo)
        ang = li * dt * expo
        pw_r, pw_i = mag * jnp.cos(ang), mag * jnp.sin(ang)
        ar, ai = pw_r[:, 1:2], pw_i[:, 1:2]
        den = lr * lr + li * li
        nr, ni = ar - 1.0, ai
        fr, fi = (nr * lr + ni * li) / den, (ni * lr - nr * li) / den
        b_re, b_im = bre_ref[gl], bim_ref[gl]
        bbr_t = pick_cols(fr * b_re - fi * b_im, tile_c3)
        bbi_t = pick_cols(fr * b_im + fi * b_re, tile_c3)
        tau_re, tau_im = jnp.where(is_tau, pw_r, 0.0), jnp.where(is_tau, pw_i, 0.0)
        pr, pi = pick_cols(tau_re, pick_rev3), pick_cols(tau_im, pick_rev3)
        xr = pr * bbr_t - pi * bbi_t
        xi = pr * bbi_t + pi * bbr_t
        krev = (jnp.dot(cre_ref[gl], xr, precision=_HI, preferred_element_type=_F32)
                - jnp.dot(cim_ref[gl], xi, precision=_HI, preferred_element_type=_F32))
        krev = krev + jnp.where(l16 - (CHUNK - 1) * gc == r16, d_ref[gl], 0.0)
        rz = jnp.concatenate([krev, jnp.zeros_like(krev)], axis=1)
        toep = jnp.concatenate(
            [rz[:, (CHUNK - 1 - t) * gc:(CHUNK - 1 - t) * gc + TOEP] for t in range(CHUNK)], axis=0)
        w1_ref[gl] = jnp.concatenate([xr, xi], axis=0).astype(w1_ref.dtype)
        gar = pick_rows(pick_next3, tau_re.T)
        gai = pick_rows(pick_next3, tau_im.T)
        crt = jnp.tile(cre_ref[gl], (CHUNK, 1))
        cit = jnp.tile(cim_ref[gl], (CHUNK, 1))
        w3_ref[gl] = jnp.concatenate([toep, crt * gar - cit * gai, -(crt * gai + cit * gar)],
                                     axis=1).astype(w3_ref.dtype)
        for l in range(levels):
            pwr_ref[gl, l] = jnp.broadcast_to(pw_r[:, CHUNK + l:CHUNK + l + 1], (ns, LANES))
            pwi_ref[gl, l] = jnp.broadcast_to(pw_i[:, CHUNK + l:CHUNK + l + 1], (ns, LANES))
        am = am_ref[gl]
        sr = jnp.sum(xr * am, axis=1, keepdims=True)
        si = jnp.sum(xi * am, axis=1, keepdims=True)
        a16r, a16i = pw_r[:, CHUNK:CHUNK + 1], pw_i[:, CHUNK:CHUNK + 1]
        sm_ref[gl, 0] = jnp.broadcast_to(sr, (ns, LANES))
        sm_ref[gl, 1] = jnp.broadcast_to(si, (ns, LANES))
        sm_ref[gl, 2] = jnp.broadcast_to(a16r * sr - a16i * si, (ns, LANES))
        sm_ref[gl, 3] = jnp.broadcast_to(a16r * si + a16i * sr, (ns, LANES))

    return [functools.partial(group, gl) for gl in range(colp_ref.shape[0])]


def _ssm_stages(u_ref, w1_ref, w3_ref, pwr_ref, pwi_ref, sm_ref, y_ref, bre_ref, bim_ref, nchunks, levels):
    ns, gc, gps = SSM_STATE, SSM_GROUP_CH, GROUPS_PER_STEP
    nblk = nchunks // LANES
    blocks = [slice(h * LANES, (h + 1) * LANES) for h in range(nblk)]
    group_rows = [slice(gl * ns, (gl + 1) * ns) for gl in range(gps)]

    def chunk_operand(gl):
        r0 = gl * gc
        return jnp.concatenate(
            [u_ref[r0:r0 + gc, j * nchunks:(j + 1) * nchunks] for j in range(CHUNK)], axis=0)

    def shifted(vals, sh, first):
        if sh % LANES == 0:
            k = sh // LANES
            return [None if h < k else vals[h - k] for h in range(nblk)]
        lane = lax.broadcasted_iota(jnp.int32, (ns, LANES), 1)
        rot = [pltpu.roll(v, sh, axis=1) for v in vals]
        keep = lane >= sh
        return [jnp.where(keep, rot[h], first if h == 0 else rot[h - 1]) for h in range(nblk)]

    def state_increments():
        lane0 = lax.broadcasted_iota(jnp.int32, (ns, LANES), 1) == 0
        for gl, rs in enumerate(group_rows):
            r1 = jnp.dot(w1_ref[gl], chunk_operand(gl), preferred_element_type=_F32)
            bre_ref[rs, :] = r1[:ns]
            bim_ref[rs, :] = r1[ns:]
            bre_ref[rs, blocks[0]] += jnp.where(lane0, sm_ref[gl, 2], 0.0)
            bim_ref[rs, blocks[0]] += jnp.where(lane0, sm_ref[gl, 3], 0.0)

    def scan_level(lvl):
        for gl, rs in enumerate(group_rows):
            p_re, p_im = pwr_ref[gl, lvl], pwi_ref[gl, lvl]
            s_re = [bre_ref[rs, blk] for blk in blocks]
            s_im = [bim_ref[rs, blk] for blk in blocks]
            t_re, t_im = shifted(s_re, 1 << lvl, 0.0), shifted(s_im, 1 << lvl, 0.0)
            for h in range(nblk):
                if t_re[h] is not None:
                    bre_ref[rs, blocks[h]] = s_re[h] + p_re * t_re[h] - p_im * t_im[h]
                    bim_ref[rs, blocks[h]] = s_im[h] + p_re * t_im[h] + p_im * t_re[h]

    def entering_states():
        for gl, rs in enumerate(group_rows):
            prev_re = shifted([bre_ref[rs, blk] for blk in blocks], 1, sm_ref[gl, 0])
            prev_im = shifted([bim_ref[rs, blk] for blk in blocks], 1, sm_ref[gl, 1])
            for h in range(nblk):
                bre_ref[rs, blocks[h]] = prev_re[h]
                bim_ref[rs, blocks[h]] = prev_im[h]

    def outputs(gl):
        r0, rs = gl * gc, group_rows[gl]
        rhs = jnp.concatenate([chunk_operand(gl), bre_ref[rs, :].astype(_BF16),
                               bim_ref[rs, :].astype(_BF16)], axis=0)
        y = jnp.dot(w3_ref[gl], rhs, preferred_element_type=_F32)
        for t in range(CHUNK):
            y_ref[r0:r0 + gc, t * nchunks:(t + 1) * nchunks] = y[t * gc:(t + 1) * gc, :]

    return ([state_increments] + [functools.partial(scan_level, lvl) for lvl in range(levels)]
            + [entering_states] + [functools.partial(outputs, gl) for gl in range(gps)])


def _ssm_gates_kernel(u_ref, w1_ref, w3_ref, pwr_ref, pwi_ref, sm_ref, xn_ref, wg_ref,
                      y_ref, g_ref, bre_ref, bim_ref, *, nchunks, levels):
    ssm = _ssm_stages(u_ref, w1_ref, w3_ref, pwr_ref, pwi_ref, sm_ref, y_ref, bre_ref, bim_ref,
                      nchunks, levels)
    proj = _proj_rows_stages(xn_ref, wg_ref, g_ref)
    every = -(-len(ssm) // len(proj))
    for i, stage in enumerate(ssm):
        if i % every == 0 and proj:
            proj.pop(0)()
        stage()
    for stage in proj:
        stage()


def _ssm_core_and_gates(u_t, w1, w3, pwr, pwi, s_meta, xn, w_gates, levels):
    b, n, s = u_t.shape
    gps, ns = GROUPS_PER_STEP, SSM_STATE
    rows = gps * SSM_GROUP_CH
    nchunks = s // CHUNK
    d = xn.shape[2]
    tm = s // (n // rows)
    tiles = s // tm
    wspec = lambda *tail: pl.BlockSpec((gps,) + tail, lambda gb, i: (gb,) + (0,) * len(tail))
    tile = lambda gb, i: ((gb * b + i) // tiles, (gb * b + i) % tiles, 0)
    return pl.pallas_call(
        functools.partial(_ssm_gates_kernel, nchunks=nchunks, levels=levels),
        grid=(n // rows, b),
        in_specs=[
            pl.BlockSpec((None, rows, s), lambda gb, i: (i, gb, 0)),
            wspec(2 * ns, TOEP), wspec(TOEP, TOEP + 2 * ns),
            wspec(levels, ns, LANES), wspec(levels, ns, LANES), wspec(4, ns, LANES),
            pl.BlockSpec((None, tm, d), tile),
            pl.BlockSpec(w_gates.shape, lambda gb, i: (0, 0)),
        ],
        out_specs=[
            pl.BlockSpec((None, rows, s), lambda gb, i: (i, gb, 0)),
            pl.BlockSpec((None, tm, w_gates.shape[1]), tile),
        ],
        out_shape=[jax.ShapeDtypeStruct((b, n, s), _F32),
                   jax.ShapeDtypeStruct((b, s, w_gates.shape[1]), _BF16)],
        scratch_shapes=[pltpu.VMEM((gps * ns, nchunks), _F32), pltpu.VMEM((gps * ns, nchunks), _F32)],
        compiler_params=pltpu.CompilerParams(
            dimension_semantics=("arbitrary", "arbitrary"), vmem_limit_bytes=VMEM_LIMIT),
        name="ssm_core_gates",
    )(u_t, w1, w3, pwr, pwi, s_meta, xn, w_gates)


def _attn_kernel(q_ref, k_ref, v_ref, kp_ref, vp_ref, km_ref, vm_ref, sink_ref, tri_ref, o_ref, acc_ref):
    first_step = pl.program_id(1) == 0
    wq = GQA_GROUP * WINDOW
    kj = lax.broadcasted_iota(jnp.int32, (WINDOW, wq), 0)
    qi = lax.broadcasted_iota(jnp.int32, (WINDOW, wq), 1) % WINDOW
    in_cur = kj <= qi
    units = [(blk, h) for blk in range(ATTN_BLOCKS) for h in range(N_KV_HEADS)]

    def kv_rows(ref, prev_ref, blk, h):
        rows = slice(h * HEAD_DIM, (h + 1) * HEAD_DIM)
        cur = ref[rows, blk * WINDOW:(blk + 1) * WINDOW]
        prev = prev_ref[rows, :] if blk == 0 else ref[rows, (blk - 1) * WINDOW:blk * WINDOW]
        return cur, prev

    scores = []
    for blk, h in units:
        q4 = jnp.concatenate(
            [q_ref[(h * GQA_GROUP + r) * HEAD_DIM:(h * GQA_GROUP + r + 1) * HEAD_DIM,
                   blk * WINDOW:(blk + 1) * WINDOW] for r in range(GQA_GROUP)], axis=1)
        k_cur, k_prev = kv_rows(k_ref, kp_ref, blk, h)
        k_meta = km_ref[h * HEAD_DIM:(h + 1) * HEAD_DIM, :]
        scores.append(tuple(lax.dot_general(k, q4, _TN, preferred_element_type=_F32)
                            for k in (k_cur, k_prev, k_meta)))
    for (blk, h), (s_cur, s_prev, s_meta) in zip(units, scores):
        if blk == 0:
            s_prev = jnp.where(first_step, -jnp.inf, s_prev)
        s_sel = jnp.where(in_cur, s_cur, s_prev)
        sink = sink_ref[h]
        m = jnp.maximum(jnp.maximum(jnp.max(s_sel, axis=0, keepdims=True),
                                    jnp.max(s_meta, axis=0, keepdims=True)), sink)
        e_sel = jnp.exp2(s_sel - m)
        e_meta = jnp.exp2(s_meta - m)
        den = (jnp.sum(e_sel, axis=0, keepdims=True) + jnp.sum(e_meta, axis=0, keepdims=True)
               + jnp.exp2(sink - m))
        e_bf = e_sel.astype(_BF16)
        p_cur = e_bf * tri_ref[...]
        p_all = jnp.concatenate([p_cur, e_bf - p_cur, e_meta.astype(_BF16)], axis=0)
        v_cur, v_prev = kv_rows(v_ref, vp_ref, blk, h)
        v_all = jnp.concatenate([v_cur, v_prev, vm_ref[h * HEAD_DIM:(h + 1) * HEAD_DIM, :]], axis=1)
        o = jnp.dot(v_all, p_all, preferred_element_type=_F32) * (1.0 / den)
        for r in range(GQA_GROUP):
            hq = h * GQA_GROUP + r
            acc_ref[hq * HEAD_DIM:(hq + 1) * HEAD_DIM, blk * WINDOW:(blk + 1) * WINDOW] = (
                o[:, r * WINDOW:(r + 1) * WINDOW])
    o_ref[...] = acc_ref[...].T.astype(o_ref.dtype)


def _attention(qkv_t, qkv_meta_t, sink_rows):
    b, _, s = qkv_t.shape
    kj = lax.broadcasted_iota(jnp.int32, (WINDOW, GQA_GROUP * WINDOW), 0)
    qi = lax.broadcasted_iota(jnp.int32, (WINDOW, GQA_GROUP * WINDOW), 1) % WINDOW
    tri = (kj <= qi).astype(_BF16)
    tq = ATTN_BLOCKS * WINDOW
    kblk = Q_W // KV_W
    prev_blk = lambda n: jnp.maximum(n * ATTN_BLOCKS - 1, 0)
    return pl.pallas_call(
        _attn_kernel,
        grid=(b, s // tq),
        in_specs=[
            pl.BlockSpec((None, Q_W, tq), lambda i, n: (i, 0, n)),
            pl.BlockSpec((None, KV_W, tq), lambda i, n: (i, kblk, n)),
            pl.BlockSpec((None, KV_W, tq), lambda i, n: (i, kblk + 1, n)),
            pl.BlockSpec((None, KV_W, WINDOW), lambda i, n: (i, kblk, prev_blk(n))),
            pl.BlockSpec((None, KV_W, WINDOW), lambda i, n: (i, kblk + 1, prev_blk(n))),
            pl.BlockSpec((None, KV_W, N_META), lambda i, n: (0, kblk, 0)),
            pl.BlockSpec((None, KV_W, N_META), lambda i, n: (0, kblk + 1, 0)),
            pl.BlockSpec(sink_rows.shape, lambda i, n: (0, 0, 0)),
            pl.BlockSpec(tri.shape, lambda i, n: (0, 0)),
        ],
        out_specs=pl.BlockSpec((None, tq, Q_W), lambda i, n: (i, n, 0)),
        out_shape=jax.ShapeDtypeStruct((b, s, Q_W), _F32),
        scratch_shapes=[pltpu.VMEM((Q_W, tq), _F32)],
        compiler_params=pltpu.CompilerParams(
            dimension_semantics=("arbitrary", "arbitrary"), vmem_limit_bytes=VMEM_LIMIT),
        name="swa_attention",
    )(qkv_t, qkv_t, qkv_t, qkv_t, qkv_t, qkv_meta_t, qkv_meta_t, sink_rows, tri)


def _tail_kernel(x_hbm, attn_hbm, y_ref, g_ref, wglu_ref, abn_ref, sbn_ref, wout_ref,
                 wfi_ref, wfo_ref, o_hbm, xbuf, abuf, obuf, hn_ref, part_ref, sem_x, sem_a, sem_o,
                 *, ntiles, nj):
    d = D_MODEL
    step = pl.program_id(0)
    slot = step % 2

    def fetch(t, sl):
        bb, jj = t // nj, t % nj
        return (pltpu.make_async_copy(*_row_set_copy(x_hbm, xbuf, sem_x, bb, jj, sl)),
                pltpu.make_async_copy(*_row_set_copy(attn_hbm, abuf, sem_a, bb, jj, sl)))

    def put(t, sl):
        dst, src, sem = _row_set_copy(o_hbm, obuf, sem_o, t // nj, t % nj, sl)
        return pltpu.make_async_copy(src, dst, sem)

    @pl.when(step == 0)
    def _():
        for c in fetch(0, 0):
            c.start()
        hn_ref[...] = jnp.zeros_like(hn_ref)
        part_ref[...] = jnp.zeros_like(part_ref)

    @pl.when(step + 1 < ntiles)
    def _():
        for c in fetch(step + 1, 1 - slot):
            c.start()

    @pl.when(step < ntiles)
    def _():
        for c in fetch(step, slot):
            c.wait()

    @pl.when(step >= 3)
    def _():
        put(step - 3, 1 - slot).wait()

    nrows = xbuf.shape[1]
    pieces = [slice(r, r + TAIL_PIECE) for r in range(0, nrows, TAIL_PIECE)]
    (lo0, hi0), (lo1, hi1) = zip(FFN_EDGES[:-1], FFN_EDGES[1:])

    def ffn_chunk(hn, lo, hi):
        gate = jnp.dot(hn, wfi_ref[:, lo:hi], preferred_element_type=_F32)
        up = jnp.dot(hn, wfi_ref[:, D_FF + lo:D_FF + hi], preferred_element_type=_F32)
        act = (jax.nn.silu(gate) * up).astype(_BF16)
        return jnp.dot(act, wfo_ref[lo:hi, :], preferred_element_type=_F32)

    front = []
    for p in pieces:
        z = jax.nn.gelu(y_ref[:, p]).astype(_BF16)
        front.append(lax.dot_general(z, wglu_ref[...], _TN, preferred_element_type=_F32))
    for p in pieces:
        obuf[1 - slot, p, :] = part_ref[p, :] + ffn_chunk(hn_ref[p, :], lo1, hi1)
    hs = []
    for p, zz in zip(pieces, front):
        ssm = zz[:, :d] * jax.nn.sigmoid(zz[:, d:])
        merged = (jax.nn.sigmoid(g_ref[p, :d].astype(_F32)) * _rms_rows(abuf[slot, p, :], abn_ref[...])
                  + jax.nn.sigmoid(g_ref[p, d:].astype(_F32)) * _rms_rows(ssm, sbn_ref[...]))
        h = xbuf[slot, p, :] + jnp.dot(merged.astype(_BF16), wout_ref[...], preferred_element_type=_F32)
        hs.append((h, _unit_rms(h).astype(_BF16)))
    for p, (h, hn) in zip(pieces, hs):
        part_ref[p, :] = h + ffn_chunk(hn, lo0, hi0)
        hn_ref[p, :] = hn

    @pl.when(step >= 1)
    def _():
        put(step - 1, 1 - slot).start()

    @pl.when(step == ntiles)
    def _():
        put(step - 1, 1 - slot).wait()
        put(step - 2, slot).wait()


def _tail(x, attn, y_t, gates, w_glu, abn, sbn, w_out, w_fi, w_fo):
    b, s, d = x.shape
    nchunks = s // CHUNK
    ntiles = b * CHUNK
    x4 = x.reshape(b, nchunks, CHUNK, d)
    a4 = attn.reshape(b, nchunks, CHUNK, d)
    const = lambda a: pl.BlockSpec(a.shape, lambda t: (0,) * a.ndim, pipeline_mode=pl.Buffered(1))
    hbm = pl.BlockSpec(memory_space=pl.ANY)
    tile = lambda t: jnp.minimum(t, ntiles - 1)
    out = pl.pallas_call(
        functools.partial(_tail_kernel, ntiles=ntiles, nj=CHUNK),
        grid=(ntiles + 1,),
        in_specs=[
            hbm, hbm,
            pl.BlockSpec((None, d, nchunks), lambda t: (tile(t) // CHUNK, 0, tile(t) % CHUNK)),
            pl.BlockSpec((None, nchunks, gates.shape[2]), lambda t: (tile(t) // CHUNK, tile(t) % CHUNK, 0)),
            const(w_glu), const(abn), const(sbn), const(w_out),
            const(w_fi), const(w_fo),
        ],
        out_specs=hbm,
        out_shape=jax.ShapeDtypeStruct(x4.shape, x.dtype),
        scratch_shapes=[pltpu.VMEM((2, nchunks, d), x.dtype), pltpu.VMEM((2, nchunks, d), attn.dtype),
                        pltpu.VMEM((2, nchunks, d), x.dtype),
                        pltpu.VMEM((nchunks, d), _BF16), pltpu.VMEM((nchunks, d), _F32),
                        pltpu.SemaphoreType.DMA((2,)), pltpu.SemaphoreType.DMA((2,)),
                        pltpu.SemaphoreType.DMA((2,))],
        compiler_params=pltpu.CompilerParams(
            dimension_semantics=("arbitrary",), vmem_limit_bytes=VMEM_LIMIT),
        name="tail",
    )(x4, a4, y_t, gates, w_glu, abn, sbn, w_out, w_fi, w_fo)
    return out.reshape(b, s, d)


def kernel(x, meta_tokens, norm_mix, w_in, q_norm, k_norm, attn_sinks, lam_re, lam_im, log_dt,
           ssm_b_re, ssm_b_im, ssm_c_re, ssm_c_im, ssm_d, w_glu, attn_branch_norm, ssm_branch_norm,
           w_out, norm_ffn, w_ffn_in, w_ffn_out):
    seq = x.shape[1]
    levels = (seq // CHUNK).bit_length() - 1
    w = w_in[0] * norm_mix[0].astype(_F32)[:, None]
    w_qkv_t = w[:, :QKV_W].T.astype(_BF16)
    w_u_t = w[:, QKV_W:QKV_W + D_MODEL].T.astype(_BF16)
    w_g = w[:, QKV_W + D_MODEL:].astype(_BF16)
    scale = HEAD_DIM ** -0.5 * LOG2_E
    qk_gain = jnp.concatenate([jnp.tile(q_norm[0] * scale, N_Q_HEADS),
                               jnp.tile(k_norm[0], N_KV_HEADS)])[:, None].astype(_F32)
    no_gain = jnp.zeros((8, 1), _F32)
    sink_rows = jnp.repeat((attn_sinks[0] * LOG2_E).reshape(N_KV_HEADS, 1, GQA_GROUP), WINDOW,
                           axis=2).astype(_F32)
    dt = jnp.exp(log_dt[0].astype(_F32))
    lr, li = lam_re[0].astype(_F32), lam_im[0].astype(_F32)
    dtb = jnp.broadcast_to(dt[:, None], lr.shape)
    colp = jnp.stack([lr, li, dtb, jnp.zeros_like(lr)], axis=2)
    d_grp = ssm_d[0].astype(_F32).reshape(SSM_GROUPS, SSM_GROUP_CH, 1)

    meta = meta_tokens.astype(_F32)[None]
    qkv_meta_t = _proj_t(meta, w_qkv_t, qk_gain, norm_heads=N_Q_HEADS + N_KV_HEADS,
                         tm=N_META, name="proj_qkv_meta")
    u_meta_t = _proj_t(meta, w_u_t, no_gain, norm_heads=0, tm=N_META, name="proj_u_meta")
    a_meta = (u_meta_t[0].astype(_F32).reshape(SSM_GROUPS, SSM_GROUP_CH, CHUNK)
              .transpose(0, 2, 1).reshape(SSM_GROUPS, 1, TOEP))

    u_t, xn, qkv_t, w1, w3, pwr, pwi, s_meta = _projections_and_ssm_prep(
        x, w_u_t, w_qkv_t, qk_gain, N_Q_HEADS + N_KV_HEADS, colp,
        ssm_b_re[0].astype(_F32), ssm_b_im[0].astype(_F32),
        ssm_c_re[0].astype(_F32), ssm_c_im[0].astype(_F32), a_meta, d_grp, levels)
    y_t, gates = _ssm_core_and_gates(u_t, w1, w3, pwr, pwi, s_meta, xn, w_g, levels)
    attn = _attention(qkv_t, qkv_meta_t, sink_rows)
    return _tail(x, attn, y_t, gates, w_glu[0].astype(_BF16), attn_branch_norm,
                 ssm_branch_norm, w_out[0].astype(_BF16),
                 (w_ffn_in[0] * norm_ffn[0].astype(_F32)[:, None]).astype(_BF16),
                 w_ffn_out[0].astype(_BF16))
```

```python
import functools

import jax
import jax.numpy as jnp
from jax import lax
from jax.experimental import pallas as pl
from jax.experimental.pallas import tpu as pltpu

D_MODEL = 1024
N_META = 16
HEAD_DIM = 64
N_Q_HEADS = 16
N_KV_HEADS = 4
GQA_GROUP = N_Q_HEADS // N_KV_HEADS
WINDOW = 128
SSM_GROUP_CH = 16
SSM_GROUPS = D_MODEL // SSM_GROUP_CH
SSM_STATE = 64
D_FF = 2816
Q_W = N_Q_HEADS * HEAD_DIM
KV_W = N_KV_HEADS * HEAD_DIM
QKV_W = Q_W + 2 * KV_W
EPS = 1e-6
LOG2_E = 1.4426950408889634

LANES = 128
CHUNK = 16
TOEP = CHUNK * SSM_GROUP_CH
GROUPS_PER_STEP = 8
MXU_DIM = 256
FFN_EDGES = (0, 6 * MXU_DIM, D_FF)
ATTN_BLOCKS = 16
PROJ_PIECE = 256
TAIL_PIECE = 256
PROJ_U_SETS = 2
VMEM_LIMIT = 56 * 1024 * 1024

_F32 = jnp.float32
_BF16 = jnp.bfloat16
_NT = (((1,), (1,)), ((), ()))
_TN = (((0,), (0,)), ((), ()))
_HI = lax.Precision.HIGHEST


def _rms_rows(x, gain):
    return x * lax.rsqrt(jnp.mean(x * x, axis=-1, keepdims=True) + EPS) * gain


def _unit_rms(x):
    return x * lax.rsqrt(jnp.mean(x * x, axis=-1, keepdims=True) + EPS)


def _proj_rows_stages(xn_ref, w_ref, o_ref):
    tm = min(xn_ref.shape[0], PROJ_PIECE)

    def piece(c):
        rows = slice(c * tm, (c + 1) * tm)
        o_ref[rows, :] = jnp.dot(xn_ref[rows, :], w_ref[...], preferred_element_type=_F32).astype(o_ref.dtype)

    return [functools.partial(piece, c) for c in range(xn_ref.shape[0] // tm)]


def _proj_t_stages(x_ref, w_ref, gain_ref, o_ref, norm_heads, xn_ref=None):
    tm_all = x_ref.shape[0]
    tm = min(tm_all, PROJ_PIECE)

    def piece(c):
        cols = slice(c * tm, (c + 1) * tm)
        xn = _unit_rms(x_ref[cols, :]).astype(_BF16)
        if xn_ref is not None:
            xn_ref[cols, :] = xn
        p = lax.dot_general(w_ref[...], xn, _NT, preferred_element_type=_F32)
        if norm_heads:
            rows = norm_heads * HEAD_DIM
            hd = p[:rows].reshape(norm_heads, HEAD_DIM, tm)
            ms = jnp.mean(hd * hd, axis=1, keepdims=True)
            hd = hd * lax.rsqrt(ms + EPS) * gain_ref[...].reshape(norm_heads, HEAD_DIM, 1)
            o_ref[:rows, cols] = hd.reshape(rows, tm).astype(o_ref.dtype)
            o_ref[rows:, cols] = p[rows:].astype(o_ref.dtype)
        else:
            o_ref[:, cols] = p.astype(o_ref.dtype)

    return [functools.partial(piece, c) for c in range(tm_all // tm)]


def _proj_t_kernel(x_ref, w_ref, gain_ref, o_ref, *, norm_heads):
    for stage in _proj_t_stages(x_ref, w_ref, gain_ref, o_ref, norm_heads):
        stage()


def _proj_t(x, w_t, gain, *, norm_heads, tm, name):
    b, s, d = x.shape
    n = w_t.shape[0]
    return pl.pallas_call(
        functools.partial(_proj_t_kernel, norm_heads=norm_heads),
        grid=(b, s // tm),
        in_specs=[
            pl.BlockSpec((None, tm, d), lambda i, j: (i, j, 0)),
            pl.BlockSpec((n, d), lambda i, j: (0, 0)),
            pl.BlockSpec(gain.shape, lambda i, j: (0, 0)),
        ],
        out_specs=pl.BlockSpec((None, n, tm), lambda i, j: (i, 0, j)),
        out_shape=jax.ShapeDtypeStruct((b, n, s), _BF16),
        compiler_params=pltpu.CompilerParams(
            dimension_semantics=("arbitrary", "arbitrary"), vmem_limit_bytes=VMEM_LIMIT),
        name=name,
    )(x, w_t, gain)


def _row_set_copy(hbm4, buf, sem, bb, jj, slot):
    return hbm4.at[bb, :, jj, :], buf.at[slot], sem.at[slot]


def _fetch_row_sets(x_hbm, xbuf, sem):
    nj = CHUNK // PROJ_U_SETS
    last = pl.num_programs(0) * pl.num_programs(1) - 1
    step = pl.program_id(0) * pl.num_programs(1) + pl.program_id(1)
    slot = step % 2
    nchunks = x_hbm.shape[1]

    def fetch(s, sl):
        return [pltpu.make_async_copy(x_hbm.at[s // nj, :, (s % nj) * PROJ_U_SETS + k, :],
                                      xbuf.at[sl, k * nchunks:(k + 1) * nchunks, :], sem.at[sl, k])
                for k in range(PROJ_U_SETS)]

    @pl.when(step == 0)
    def _():
        for c in fetch(step, slot):
            c.start()

    @pl.when(step < last)
    def _():
        for c in fetch(step + 1, 1 - slot):
            c.start()

    for c in fetch(step, slot):
        c.wait()
    return slot


def _proj_kernel(x_hbm, xnat_ref, wu_ref, wq_ref, gain_ref,
                 colp_ref, bre_ref, bim_ref, cre_ref, cim_ref, am_ref, d_ref,
                 u_ref, xn_ref, qkv_ref, w1_ref, w3_ref, pwr_ref, pwi_ref, sm_ref, xbuf, sem,
                 *, levels, norm_heads):
    slot = _fetch_row_sets(x_hbm, xbuf, sem)
    proj_u = _proj_t_stages(xbuf.at[slot], wu_ref, None, u_ref, 0, xn_ref)
    proj_q = _proj_t_stages(xnat_ref, wq_ref, gain_ref, qkv_ref, norm_heads)
    prep = _ssm_prep_stages(colp_ref, bre_ref, bim_ref, cre_ref, cim_ref, am_ref, d_ref,
                            w1_ref, w3_ref, pwr_ref, pwi_ref, sm_ref, levels)
    mixed = [st for pair in zip(proj_q, proj_u) for st in pair]
    for stage in mixed[:3] + prep[:1] + mixed[3:5] + prep[1:] + mixed[5:]:
        stage()


def _projections_and_ssm_prep(x, wu_t, wq_t, qk_gain, norm_heads, colp, b_re, b_im, c_re, c_im,
                              a_meta, d_grp, levels):
    b, s, d = x.shape
    nu, nq = wu_t.shape[0], wq_t.shape[0]
    nchunks = s // CHUNK
    x4 = x.reshape(b, nchunks, CHUNK, d)
    nj = CHUNK // PROJ_U_SETS
    tm = s // nj
    g, ns, gc = SSM_GROUPS, SSM_STATE, SSM_GROUP_CH
    gpp = g // (b * nj)
    assert gpp * b * nj == g
    gspec = lambda *tail: pl.BlockSpec((gpp,) + tail, lambda i, j: (i * nj + j,) + (0,) * len(tail))
    const = lambda a: pl.BlockSpec(a.shape, lambda i, j: (0,) * a.ndim)
    return pl.pallas_call(
        functools.partial(_proj_kernel, levels=levels, norm_heads=norm_heads),
        grid=(b, nj),
        in_specs=[
            pl.BlockSpec(memory_space=pl.ANY),
            pl.BlockSpec((None, tm, d), lambda i, j: (i, j, 0)),
            const(wu_t), const(wq_t), const(qk_gain),
            gspec(ns, 4), gspec(ns, gc), gspec(ns, gc), gspec(gc, ns), gspec(gc, ns), gspec(1, TOEP),
            gspec(gc, 1),
        ],
        out_specs=[
            pl.BlockSpec((None, nu, PROJ_U_SETS * nchunks), lambda i, j: (i, 0, j)),
            pl.BlockSpec((None, PROJ_U_SETS * nchunks, d), lambda i, j: (i, j, 0)),
            pl.BlockSpec((None, nq, tm), lambda i, j: (i, 0, j)),
            gspec(2 * ns, TOEP), gspec(TOEP, TOEP + 2 * ns), gspec(levels, ns, LANES),
            gspec(levels, ns, LANES), gspec(4, ns, LANES),
        ],
        out_shape=[jax.ShapeDtypeStruct((b, nu, s), _BF16),
                   jax.ShapeDtypeStruct((b, s, d), _BF16),
                   jax.ShapeDtypeStruct((b, nq, s), _BF16),
                   jax.ShapeDtypeStruct((g, 2 * ns, TOEP), _BF16),
                   jax.ShapeDtypeStruct((g, TOEP, TOEP + 2 * ns), _BF16),
                   jax.ShapeDtypeStruct((g, levels, ns, LANES), _F32),
                   jax.ShapeDtypeStruct((g, levels, ns, LANES), _F32),
                   jax.ShapeDtypeStruct((g, 4, ns, LANES), _F32)],
        scratch_shapes=[pltpu.VMEM((2, PROJ_U_SETS * nchunks, d), x.dtype),
                        pltpu.SemaphoreType.DMA((2, PROJ_U_SETS))],
        compiler_params=pltpu.CompilerParams(
            dimension_semantics=("arbitrary", "arbitrary"), vmem_limit_bytes=VMEM_LIMIT),
        name="projections_ssm_prep",
    )(x4, x, wu_t, wq_t, qk_gain, colp, b_re, b_im, c_re, c_im, a_meta, d_grp)


def _ssm_prep_stages(colp_ref, bre_ref, bim_ref, cre_ref, cim_ref, am_ref, d_ref,
                     w1_ref, w3_ref, pwr_ref, pwi_ref, sm_ref, levels):
    ns, gc = SSM_STATE, SSM_GROUP_CH

    def split3(a):
        hi = a.astype(_BF16)
        rest = a - hi.astype(_F32)
        mid = rest.astype(_BF16)
        return hi, mid, (rest - mid.astype(_F32)).astype(_BF16)

    pick_cols = lambda a, m3: jnp.dot(jnp.concatenate(split3(a), axis=1), m3, preferred_element_type=_F32)
    pick_rows = lambda m3, a: jnp.dot(m3, jnp.concatenate(split3(a), axis=0), preferred_element_type=_F32)
    l16 = lax.broadcasted_iota(jnp.int32, (gc, TOEP), 1)
    r16 = lax.broadcasted_iota(jnp.int32, (gc, TOEP), 0)
    rc = lax.broadcasted_iota(jnp.int32, (3 * gc, TOEP), 0)
    lc = lax.broadcasted_iota(jnp.int32, (3 * gc, TOEP), 1)
    tile_c3 = (lc % gc == rc % gc).astype(_BF16)
    tau_r = lax.broadcasted_iota(jnp.int32, (3 * LANES, TOEP), 0) % LANES
    jc_l = lax.broadcasted_iota(jnp.int32, (3 * LANES, TOEP), 1)
    pick_rev3 = (tau_r == CHUNK - 1 - jc_l // gc).astype(_BF16)
    tc_r = lax.broadcasted_iota(jnp.int32, (TOEP, 3 * LANES), 0)
    tau_l = lax.broadcasted_iota(jnp.int32, (TOEP, 3 * LANES), 1) % LANES
    pick_next3 = (tau_l == tc_r // gc + 1).astype(_BF16)
    lane = lax.broadcasted_iota(jnp.int32, (ns, LANES), 1)
    lvl = jnp.clip(lane - CHUNK, 0, levels - 1)
    expo = jnp.where(lane <= CHUNK, lane,
                     jnp.where(lane < CHUNK + levels, CHUNK * jnp.left_shift(1, lvl), 0)).astype(_F32)
    is_tau = lane <= CHUNK

    def group(gl):
        lr, li, dt = colp_ref[gl, :, 0:1], colp_ref[gl, :, 1:2], colp_ref[gl, :, 2:3]
        mag = jnp.exp(lr * dt * expo)
        ang = li * dt * expo
        pw_r, pw_i = mag * jnp.cos(ang), mag * jnp.sin(ang)
        ar, ai = pw_r[:, 1:2], pw_i[:, 1:2]
        den = lr * lr + li * li
        nr, ni = ar - 1.0, ai
        fr, fi = (nr * lr + ni * li) / den, (ni * lr - nr * li) / den
        b_re, b_im = bre_ref[gl], bim_ref[gl]
        bbr_t = pick_cols(fr * b_re - fi * b_im, tile_c3)
        bbi_t = pick_cols(fr * b_im + fi * b_re, tile_c3)
        tau_re, tau_im = jnp.where(is_tau, pw_r, 0.0), jnp.where(is_tau, pw_i, 0.0)
        pr, pi = pick_cols(tau_re, pick_rev3), pick_cols(tau_im, pick_rev3)
        xr = pr * bbr_t - pi * bbi_t
        xi = pr * bbi_t + pi * bbr_t
        krev = (jnp.dot(cre_ref[gl], xr, precision=_HI, preferred_element_type=_F32)
                - jnp.dot(cim_ref[gl], xi, precision=_HI, preferred_element_type=_F32))
        krev = krev + jnp.where(l16 - (CHUNK - 1) * gc == r16, d_ref[gl], 0.0)
        rz = jnp.concatenate([krev, jnp.zeros_like(krev)], axis=1)
        toep = jnp.concatenate(
            [rz[:, (CHUNK - 1 - t) * gc:(CHUNK - 1 - t) * gc + TOEP] for t in range(CHUNK)], axis=0)
        w1_ref[gl] = jnp.concatenate([xr, xi], axis=0).astype(w1_ref.dtype)
        gar = pick_rows(pick_next3, tau_re.T)
        gai = pick_rows(pick_next3, tau_im.T)
        crt = jnp.tile(cre_ref[gl], (CHUNK, 1))
        cit = jnp.tile(cim_ref[gl], (CHUNK, 1))
        w3_ref[gl] = jnp.concatenate([toep, crt * gar - cit * gai, -(crt * gai + cit * gar)],
                                     axis=1).astype(w3_ref.dtype)
        for l in range(levels):
            pwr_ref[gl, l] = jnp.broadcast_to(pw_r[:, CHUNK + l:CHUNK + l + 1], (ns, LANES))
            pwi_ref[gl, l] = jnp.broadcast_to(pw_i[:, CHUNK + l:CHUNK + l + 1], (ns, LANES))
        am = am_ref[gl]
        sr = jnp.sum(xr * am, axis=1, keepdims=True)
        si = jnp.sum(xi * am, axis=1, keepdims=True)
        a16r, a16i = pw_r[:, CHUNK:CHUNK + 1], pw_i[:, CHUNK:CHUNK + 1]
        sm_ref[gl, 0] = jnp.broadcast_to(sr, (ns, LANES))
        sm_ref[gl, 1] = jnp.broadcast_to(si, (ns, LANES))
        sm_ref[gl, 2] = jnp.broadcast_to(a16r * sr - a16i * si, (ns, LANES))
        sm_ref[gl, 3] = jnp.broadcast_to(a16r * si + a16i * sr, (ns, LANES))

    return [functools.partial(group, gl) for gl in range(colp_ref.shape[0])]


def _ssm_stages(u_ref, w1_ref, w3_ref, pwr_ref, pwi_ref, sm_ref, y_ref, bre_ref, bim_ref, nchunks, levels):
    ns, gc, gps = SSM_STATE, SSM_GROUP_CH, GROUPS_PER_STEP
    nblk = nchunks // LANES
    blocks = [slice(h * LANES, (h + 1) * LANES) for h in range(nblk)]
    group_rows = [slice(gl * ns, (gl + 1) * ns) for gl in range(gps)]

    def chunk_operand(gl):
        r0 = gl * gc
        return jnp.concatenate(
            [u_ref[r0:r0 + gc, j * nchunks:(j + 1) * nchunks] for j in range(CHUNK)], axis=0)

    def shifted(vals, sh, first):
        if sh % LANES == 0:
            k = sh // LANES
            return [None if h < k else vals[h - k] for h in range(nblk)]
        lane = lax.broadcasted_iota(jnp.int32, (ns, LANES), 1)
        rot = [pltpu.roll(v, sh, axis=1) for v in vals]
        keep = lane >= sh
        return [jnp.where(keep, rot[h], first if h == 0 else rot[h - 1]) for h in range(nblk)]

    def state_increments():
        lane0 = lax.broadcasted_iota(jnp.int32, (ns, LANES), 1) == 0
        for gl, rs in enumerate(group_rows):
            r1 = jnp.dot(w1_ref[gl], chunk_operand(gl), preferred_element_type=_F32)
            bre_ref[rs, :] = r1[:ns]
            bim_ref[rs, :] = r1[ns:]
            bre_ref[rs, blocks[0]] += jnp.where(lane0, sm_ref[gl, 2], 0.0)
            bim_ref[rs, blocks[0]] += jnp.where(lane0, sm_ref[gl, 3], 0.0)

    def scan_level(lvl):
        for gl, rs in enumerate(group_rows):
            p_re, p_im = pwr_ref[gl, lvl], pwi_ref[gl, lvl]
            s_re = [bre_ref[rs, blk] for blk in blocks]
            s_im = [bim_ref[rs, blk] for blk in blocks]
            t_re, t_im = shifted(s_re, 1 << lvl, 0.0), shifted(s_im, 1 << lvl, 0.0)
            for h in range(nblk):
                if t_re[h] is not None:
                    bre_ref[rs, blocks[h]] = s_re[h] + p_re * t_re[h] - p_im * t_im[h]
                    bim_ref[rs, blocks[h]] = s_im[h] + p_re * t_im[h] + p_im * t_re[h]

    def entering_states():
        for gl, rs in enumerate(group_rows):
            prev_re = shifted([bre_ref[rs, blk] for blk in blocks], 1, sm_ref[gl, 0])
            prev_im = shifted([bim_ref[rs, blk] for blk in blocks], 1, sm_ref[gl, 1])
            for h in range(nblk):
                bre_ref[rs, blocks[h]] = prev_re[h]
                bim_ref[rs, blocks[h]] = prev_im[h]

    def outputs(gl):
        r0, rs = gl * gc, group_rows[gl]
        rhs = jnp.concatenate([chunk_operand(gl), bre_ref[rs, :].astype(_BF16),
                               bim_ref[rs, :].astype(_BF16)], axis=0)
        y = jnp.dot(w3_ref[gl], rhs, preferred_element_type=_F32)
        for t in range(CHUNK):
            y_ref[r0:r0 + gc, t * nchunks:(t + 1) * nchunks] = y[t * gc:(t + 1) * gc, :]

    return ([state_increments] + [functools.partial(scan_level, lvl) for lvl in range(levels)]
            + [entering_states] + [functools.partial(outputs, gl) for gl in range(gps)])


def _ssm_gates_kernel(u_ref, w1_ref, w3_ref, pwr_ref, pwi_ref, sm_ref, xn_ref, wg_ref,
                      y_ref, g_ref, bre_ref, bim_ref, *, nchunks, levels):
    ssm = _ssm_stages(u_ref, w1_ref, w3_ref, pwr_ref, pwi_ref, sm_ref, y_ref, bre_ref, bim_ref,
                      nchunks, levels)
    proj = _proj_rows_stages(xn_ref, wg_ref, g_ref)
    every = -(-len(ssm) // len(proj))
    for i, stage in enumerate(ssm):
        if i % every == 0 and proj:
            proj.pop(0)()
        stage()
    for stage in proj:
        stage()


def _ssm_core_and_gates(u_t, w1, w3, pwr, pwi, s_meta, xn, w_gates, levels):
    b, n, s = u_t.shape
    gps, ns = GROUPS_PER_STEP, SSM_STATE
    rows = gps * SSM_GROUP_CH
    nchunks = s // CHUNK
    d = xn.shape[2]
    tm = s // (n // rows)
    tiles = s // tm
    wspec = lambda *tail: pl.BlockSpec((gps,) + tail, lambda gb, i: (gb,) + (0,) * len(tail))
    tile = lambda gb, i: ((gb * b + i) // tiles, (gb * b + i) % tiles, 0)
    return pl.pallas_call(
        functools.partial(_ssm_gates_kernel, nchunks=nchunks, levels=levels),
        grid=(n // rows, b),
        in_specs=[
            pl.BlockSpec((None, rows, s), lambda gb, i: (i, gb, 0)),
            wspec(2 * ns, TOEP), wspec(TOEP, TOEP + 2 * ns),
            wspec(levels, ns, LANES), wspec(levels, ns, LANES), wspec(4, ns, LANES),
            pl.BlockSpec((None, tm, d), tile),
            pl.BlockSpec(w_gates.shape, lambda gb, i: (0, 0)),
        ],
        out_specs=[
            pl.BlockSpec((None, rows, s), lambda gb, i: (i, gb, 0)),
            pl.BlockSpec((None, tm, w_gates.shape[1]), tile),
        ],
        out_shape=[jax.ShapeDtypeStruct((b, n, s), _F32),
                   jax.ShapeDtypeStruct((b, s, w_gates.shape[1]), _BF16)],
        scratch_shapes=[pltpu.VMEM((gps * ns, nchunks), _F32), pltpu.VMEM((gps * ns, nchunks), _F32)],
        compiler_params=pltpu.CompilerParams(
            dimension_semantics=("arbitrary", "arbitrary"), vmem_limit_bytes=VMEM_LIMIT),
        name="ssm_core_gates",
    )(u_t, w1, w3, pwr, pwi, s_meta, xn, w_gates)


def _attn_kernel(q_ref, k_ref, v_ref, kp_ref, vp_ref, km_ref, vm_ref, sink_ref, tri_ref, o_ref, acc_ref):
    first_step = pl.program_id(1) == 0
    wq = GQA_GROUP * WINDOW
    kj = lax.broadcasted_iota(jnp.int32, (WINDOW, wq), 0)
    qi = lax.broadcasted_iota(jnp.int32, (WINDOW, wq), 1) % WINDOW
    in_cur = kj <= qi
    units = [(blk, h) for blk in range(ATTN_BLOCKS) for h in range(N_KV_HEADS)]

    def kv_rows(ref, prev_ref, blk, h):
        rows = slice(h * HEAD_DIM, (h + 1) * HEAD_DIM)
        cur = ref[rows, blk * WINDOW:(blk + 1) * WINDOW]
        prev = prev_ref[rows, :] if blk == 0 else ref[rows, (blk - 1) * WINDOW:blk * WINDOW]
        return cur, prev

    scores = []
    for blk, h in units:
        q4 = jnp.concatenate(
            [q_ref[(h * GQA_GROUP + r) * HEAD_DIM:(h * GQA_GROUP + r + 1) * HEAD_DIM,
                   blk * WINDOW:(blk + 1) * WINDOW] for r in range(GQA_GROUP)], axis=1)
        k_cur, k_prev = kv_rows(k_ref, kp_ref, blk, h)
        k_meta = km_ref[h * HEAD_DIM:(h + 1) * HEAD_DIM, :]
        scores.append(tuple(lax.dot_general(k, q4, _TN, preferred_element_type=_F32)
                            for k in (k_cur, k_prev, k_meta)))
    for (blk, h), (s_cur, s_prev, s_meta) in zip(units, scores):
        if blk == 0:
            s_prev = jnp.where(first_step, -jnp.inf, s_prev)
        s_sel = jnp.where(in_cur, s_cur, s_prev)
        sink = sink_ref[h]
        m = jnp.maximum(jnp.maximum(jnp.max(s_sel, axis=0, keepdims=True),
                                    jnp.max(s_meta, axis=0, keepdims=True)), sink)
        e_sel = jnp.exp2(s_sel - m)
        e_meta = jnp.exp2(s_meta - m)
        den = (jnp.sum(e_sel, axis=0, keepdims=True) + jnp.sum(e_meta, axis=0, keepdims=True)
               + jnp.exp2(sink - m))
        e_bf = e_sel.astype(_BF16)
        p_cur = e_bf * tri_ref[...]
        p_all = jnp.concatenate([p_cur, e_bf - p_cur, e_meta.astype(_BF16)], axis=0)
        v_cur, v_prev = kv_rows(v_ref, vp_ref, blk, h)
        v_all = jnp.concatenate([v_cur, v_prev, vm_ref[h * HEAD_DIM:(h + 1) * HEAD_DIM, :]], axis=1)
        o = jnp.dot(v_all, p_all, preferred_element_type=_F32) * (1.0 / den)
        for r in range(GQA_GROUP):
            hq = h * GQA_GROUP + r
            acc_ref[hq * HEAD_DIM:(hq + 1) * HEAD_DIM, blk * WINDOW:(blk + 1) * WINDOW] = (
                o[:, r * WINDOW:(r + 1) * WINDOW])
    o_ref[...] = acc_ref[...].T.astype(o_ref.dtype)


def _attention(qkv_t, qkv_meta_t, sink_rows):
    b, _, s = qkv_t.shape
    kj = lax.broadcasted_iota(jnp.int32, (WINDOW, GQA_GROUP * WINDOW), 0)
    qi = lax.broadcasted_iota(jnp.int32, (WINDOW, GQA_GROUP * WINDOW), 1) % WINDOW
    tri = (kj <= qi).astype(_BF16)
    tq = ATTN_BLOCKS * WINDOW
    kblk = Q_W // KV_W
    prev_blk = lambda n: jnp.maximum(n * ATTN_BLOCKS - 1, 0)
    return pl.pallas_call(
        _attn_kernel,
        grid=(b, s // tq),
        in_specs=[
            pl.BlockSpec((None, Q_W, tq), lambda i, n: (i, 0, n)),
            pl.BlockSpec((None, KV_W, tq), lambda i, n: (i, kblk, n)),
            pl.BlockSpec((None, KV_W, tq), lambda i, n: (i, kblk + 1, n)),
            pl.BlockSpec((None, KV_W, WINDOW), lambda i, n: (i, kblk, prev_blk(n))),
            pl.BlockSpec((None, KV_W, WINDOW), lambda i, n: (i, kblk + 1, prev_blk(n))),
            pl.BlockSpec((None, KV_W, N_META), lambda i, n: (0, kblk, 0)),
            pl.BlockSpec((None, KV_W, N_META), lambda i, n: (0, kblk + 1, 0)),
            pl.BlockSpec(sink_rows.shape, lambda i, n: (0, 0, 0)),
            pl.BlockSpec(tri.shape, lambda i, n: (0, 0)),
        ],
        out_specs=pl.BlockSpec((None, tq, Q_W), lambda i, n: (i, n, 0)),
        out_shape=jax.ShapeDtypeStruct((b, s, Q_W), _F32),
        scratch_shapes=[pltpu.VMEM((Q_W, tq), _F32)],
        compiler_params=pltpu.CompilerParams(
            dimension_semantics=("arbitrary", "arbitrary"), vmem_limit_bytes=VMEM_LIMIT),
        name="swa_attention",
    )(qkv_t, qkv_t, qkv_t, qkv_t, qkv_t, qkv_meta_t, qkv_meta_t, sink_rows, tri)


def _tail_kernel(x_hbm, attn_hbm, y_ref, g_ref, wglu_ref, abn_ref, sbn_ref, wout_ref,
                 wfi_ref, wfo_ref, o_hbm, xbuf, abuf, obuf, hn_ref, part_ref, sem_x, sem_a, sem_o,
                 *, ntiles, nj):
    d = D_MODEL
    step = pl.program_id(0)
    slot = step % 2

    def fetch(t, sl):
        bb, jj = t // nj, t % nj
        return (pltpu.make_async_copy(*_row_set_copy(x_hbm, xbuf, sem_x, bb, jj, sl)),
                pltpu.make_async_copy(*_row_set_copy(attn_hbm, abuf, sem_a, bb, jj, sl)))

    def put(t, sl):
        dst, src, sem = _row_set_copy(o_hbm, obuf, sem_o, t // nj, t % nj, sl)
        return pltpu.make_async_copy(src, dst, sem)

    @pl.when(step == 0)
    def _():
        for c in fetch(0, 0):
            c.start()
        hn_ref[...] = jnp.zeros_like(hn_ref)
        part_ref[...] = jnp.zeros_like(part_ref)

    @pl.when(step + 1 < ntiles)
    def _():
        for c in fetch(step + 1, 1 - slot):
            c.start()

    @pl.when(step < ntiles)
    def _():
        for c in fetch(step, slot):
            c.wait()

    @pl.when(step >= 3)
    def _():
        put(step - 3, 1 - slot).wait()

    nrows = xbuf.shape[1]
    pieces = [slice(r, r + TAIL_PIECE) for r in range(0, nrows, TAIL_PIECE)]
    (lo0, hi0), (lo1, hi1) = zip(FFN_EDGES[:-1], FFN_EDGES[1:])

    def ffn_chunk(hn, lo, hi):
        gate = jnp.dot(hn, wfi_ref[:, lo:hi], preferred_element_type=_F32)
        up = jnp.dot(hn, wfi_ref[:, D_FF + lo:D_FF + hi], preferred_element_type=_F32)
        act = (jax.nn.silu(gate) * up).astype(_BF16)
        return jnp.dot(act, wfo_ref[lo:hi, :], preferred_element_type=_F32)

    front = []
    for p in pieces:
        z = jax.nn.gelu(y_ref[:, p]).astype(_BF16)
        front.append(lax.dot_general(z, wglu_ref[...], _TN, preferred_element_type=_F32))
    for p in pieces:
        obuf[1 - slot, p, :] = part_ref[p, :] + ffn_chunk(hn_ref[p, :], lo1, hi1)
    hs = []
    for p, zz in zip(pieces, front):
        ssm = zz[:, :d] * jax.nn.sigmoid(zz[:, d:])
        merged = (jax.nn.sigmoid(g_ref[p, :d].astype(_F32)) * _rms_rows(abuf[slot, p, :], abn_ref[...])
                  + jax.nn.sigmoid(g_ref[p, d:].astype(_F32)) * _rms_rows(ssm, sbn_ref[...]))
        h = xbuf[slot, p, :] + jnp.dot(merged.astype(_BF16), wout_ref[...], preferred_element_type=_F32)
        hs.append((h, _unit_rms(h).astype(_BF16)))
    for p, (h, hn) in zip(pieces, hs):
        part_ref[p, :] = h + ffn_chunk(hn, lo0, hi0)
        hn_ref[p, :] = hn

    @pl.when(step >= 1)
    def _():
        put(step - 1, 1 - slot).start()

    @pl.when(step == ntiles)
    def _():
        put(step - 1, 1 - slot).wait()
        put(step - 2, slot).wait()


def _tail(x, attn, y_t, gates, w_glu, abn, sbn, w_out, w_fi, w_fo):
    b, s, d = x.shape
    nchunks = s // CHUNK
    ntiles = b * CHUNK
    x4 = x.reshape(b, nchunks, CHUNK, d)
    a4 = attn.reshape(b, nchunks, CHUNK, d)
    const = lambda a: pl.BlockSpec(a.shape, lambda t: (0,) * a.ndim, pipeline_mode=pl.Buffered(1))
    hbm = pl.BlockSpec(memory_space=pl.ANY)
    tile = lambda t: jnp.minimum(t, ntiles - 1)
    out = pl.pallas_call(
        functools.partial(_tail_kernel, ntiles=ntiles, nj=CHUNK),
        grid=(ntiles + 1,),
        in_specs=[
            hbm, hbm,
            pl.BlockSpec((None, d, nchunks), lambda t: (tile(t) // CHUNK, 0, tile(t) % CHUNK)),
            pl.BlockSpec((None, nchunks, gates.shape[2]), lambda t: (tile(t) // CHUNK, tile(t) % CHUNK, 0)),
            const(w_glu), const(abn), const(sbn), const(w_out),
            const(w_fi), const(w_fo),
        ],
        out_specs=hbm,
        out_shape=jax.ShapeDtypeStruct(x4.shape, x.dtype),
        scratch_shapes=[pltpu.VMEM((2, nchunks, d), x.dtype), pltpu.VMEM((2, nchunks, d), attn.dtype),
                        pltpu.VMEM((2, nchunks, d), x.dtype),
                        pltpu.VMEM((nchunks, d), _BF16), pltpu.VMEM((nchunks, d), _F32),
                        pltpu.SemaphoreType.DMA((2,)), pltpu.SemaphoreType.DMA((2,)),
                        pltpu.SemaphoreType.DMA((2,))],
        compiler_params=pltpu.CompilerParams(
            dimension_semantics=("arbitrary",), vmem_limit_bytes=VMEM_LIMIT),
        name="tail",
    )(x4, a4, y_t, gates, w_glu, abn, sbn, w_out, w_fi, w_fo)
    return out.reshape(b, s, d)


def kernel(x, meta_tokens, norm_mix, w_in, q_norm, k_norm, attn_sinks, lam_re, lam_im, log_dt,
           ssm_b_re, ssm_b_im, ssm_c_re, ssm_c_im, ssm_d, w_glu, attn_branch_norm, ssm_branch_norm,
           w_out, norm_ffn, w_ffn_in, w_ffn_out):
    seq = x.shape[1]
    levels = (seq // CHUNK).bit_length() - 1
    w = w_in[0] * norm_mix[0].astype(_F32)[:, None]
    w_qkv_t = w[:, :QKV_W].T.astype(_BF16)
    w_u_t = w[:, QKV_W:QKV_W + D_MODEL].T.astype(_BF16)
    w_g = w[:, QKV_W + D_MODEL:].astype(_BF16)
    scale = HEAD_DIM ** -0.5 * LOG2_E
    qk_gain = jnp.concatenate([jnp.tile(q_norm[0] * scale, N_Q_HEADS),
                               jnp.tile(k_norm[0], N_KV_HEADS)])[:, None].astype(_F32)
    no_gain = jnp.zeros((8, 1), _F32)
    sink_rows = jnp.repeat((attn_sinks[0] * LOG2_E).reshape(N_KV_HEADS, 1, GQA_GROUP), WINDOW,
                           axis=2).astype(_F32)
    dt = jnp.exp(log_dt[0].astype(_F32))
    lr, li = lam_re[0].astype(_F32), lam_im[0].astype(_F32)
    dtb = jnp.broadcast_to(dt[:, None], lr.shape)
    colp = jnp.stack([lr, li, dtb, jnp.zeros_like(lr)], axis=2)
    d_grp = ssm_d[0].astype(_F32).reshape(SSM_GROUPS, SSM_GROUP_CH, 1)

    meta = meta_tokens.astype(_F32)[None]
    qkv_meta_t = _proj_t(meta, w_qkv_t, qk_gain, norm_heads=N_Q_HEADS + N_KV_HEADS,
                         tm=N_META, name="proj_qkv_meta")
    u_meta_t = _proj_t(meta, w_u_t, no_gain, norm_heads=0, tm=N_META, name="proj_u_meta")
    a_meta = (u_meta_t[0].astype(_F32).reshape(SSM_GROUPS, SSM_GROUP_CH, CHUNK)
              .transpose(0, 2, 1).reshape(SSM_GROUPS, 1, TOEP))

    u_t, xn, qkv_t, w1, w3, pwr, pwi, s_meta = _projections_and_ssm_prep(
        x, w_u_t, w_qkv_t, qk_gain, N_Q_HEADS + N_KV_HEADS, colp,
        ssm_b_re[0].astype(_F32), ssm_b_im[0].astype(_F32),
        ssm_c_re[0].astype(_F32), ssm_c_im[0].astype(_F32), a_meta, d_grp, levels)
    y_t, gates = _ssm_core_and_gates(u_t, w1, w3, pwr, pwi, s_meta, xn, w_g, levels)
    attn = _attention(qkv_t, qkv_meta_t, sink_rows)
    return _tail(x, attn, y_t, gates, w_glu[0].astype(_BF16), attn_branch_norm,
                 ssm_branch_norm, w_out[0].astype(_BF16),
                 (w_ffn_in[0] * norm_ffn[0].astype(_F32)[:, None]).astype(_BF16),
                 w_ffn_out[0].astype(_BF16))
```

```python
import functools

import jax
import jax.numpy as jnp
from jax import lax
from jax.experimental import pallas as pl
from jax.experimental.pallas import tpu as pltpu

D_MODEL = 1024
N_META = 16
HEAD_DIM = 64
N_Q_HEADS = 16
N_KV_HEADS = 4
GQA_GROUP = N_Q_HEADS // N_KV_HEADS
WINDOW = 128
SSM_GROUP_CH = 16
SSM_GROUPS = D_MODEL // SSM_GROUP_CH
SSM_STATE = 64
D_FF = 2816
Q_W = N_Q_HEADS * HEAD_DIM
KV_W = N_KV_HEADS * HEAD_DIM
QKV_W = Q_W + 2 * KV_W
EPS = 1e-6
LOG2_E = 1.4426950408889634

LANES = 128
CHUNK = 16
TOEP = CHUNK * SSM_GROUP_CH
GROUPS_PER_STEP = 8
MXU_DIM = 256
FFN_EDGES = (0, 6 * MXU_DIM, D_FF)
ATTN_BLOCKS = 16
PROJ_PIECE = 256
TAIL_PIECE = 256
PROJ_U_SETS = 2
VMEM_LIMIT = 56 * 1024 * 1024

_F32 = jnp.float32
_BF16 = jnp.bfloat16
_NT = (((1,), (1,)), ((), ()))
_TN = (((0,), (0,)), ((), ()))
_HI = lax.Precision.HIGHEST


def _rms_rows(x, gain):
    return x * lax.rsqrt(jnp.mean(x * x, axis=-1, keepdims=True) + EPS) * gain


def _unit_rms(x):
    return x * lax.rsqrt(jnp.mean(x * x, axis=-1, keepdims=True) + EPS)


def _proj_rows_stages(xn_ref, w_ref, o_ref):
    tm = min(xn_ref.shape[0], PROJ_PIECE)

    def piece(c):
        rows = slice(c * tm, (c + 1) * tm)
        o_ref[rows, :] = jnp.dot(xn_ref[rows, :], w_ref[...], preferred_element_type=_F32).astype(o_ref.dtype)

    return [functools.partial(piece, c) for c in range(xn_ref.shape[0] // tm)]


def _proj_t_stages(x_ref, w_ref, gain_ref, o_ref, norm_heads, xn_ref=None):
    tm_all = x_ref.shape[0]
    tm = min(tm_all, PROJ_PIECE)

    def piece(c):
        cols = slice(c * tm, (c + 1) * tm)
        xn = _unit_rms(x_ref[cols, :]).astype(_BF16)
        if xn_ref is not None:
            xn_ref[cols, :] = xn
        p = lax.dot_general(w_ref[...], xn, _NT, preferred_element_type=_F32)
        if norm_heads:
            rows = norm_heads * HEAD_DIM
            hd = p[:rows].reshape(norm_heads, HEAD_DIM, tm)
            ms = jnp.mean(hd * hd, axis=1, keepdims=True)
            hd = hd * lax.rsqrt(ms + EPS) * gain_ref[...].reshape(norm_heads, HEAD_DIM, 1)
            o_ref[:rows, cols] = hd.reshape(rows, tm).astype(o_ref.dtype)
            o_ref[rows:, cols] = p[rows:].astype(o_ref.dtype)
        else:
            o_ref[:, cols] = p.astype(o_ref.dtype)

    return [functools.partial(piece, c) for c in range(tm_all // tm)]


def _proj_t_kernel(x_ref, w_ref, gain_ref, o_ref, *, norm_heads):
    for stage in _proj_t_stages(x_ref, w_ref, gain_ref, o_ref, norm_heads):
        stage()


def _proj_t(x, w_t, gain, *, norm_heads, tm, name):
    b, s, d = x.shape
    n = w_t.shape[0]
    return pl.pallas_call(
        functools.partial(_proj_t_kernel, norm_heads=norm_heads),
        grid=(b, s // tm),
        in_specs=[
            pl.BlockSpec((None, tm, d), lambda i, j: (i, j, 0)),
            pl.BlockSpec((n, d), lambda i, j: (0, 0)),
            pl.BlockSpec(gain.shape, lambda i, j: (0, 0)),
        ],
        out_specs=pl.BlockSpec((None, n, tm), lambda i, j: (i, 0, j)),
        out_shape=jax.ShapeDtypeStruct((b, n, s), _BF16),
        compiler_params=pltpu.CompilerParams(
            dimension_semantics=("arbitrary", "arbitrary"), vmem_limit_bytes=VMEM_LIMIT),
        name=name,
    )(x, w_t, gain)


def _row_set_copy(hbm4, buf, sem, bb, jj, slot):
    return hbm4.at[bb, :, jj, :], buf.at[slot], sem.at[slot]


def _fetch_row_sets(x_hbm, xbuf, sem):
    nj = CHUNK // PROJ_U_SETS
    last = pl.num_programs(0) * pl.num_programs(1) - 1
    step = pl.program_id(0) * pl.num_programs(1) + pl.program_id(1)
    slot = step % 2
    nchunks = x_hbm.shape[1]

    def fetch(s, sl):
        return [pltpu.make_async_copy(x_hbm.at[s // nj, :, (s % nj) * PROJ_U_SETS + k, :],
                                      xbuf.at[sl, k * nchunks:(k + 1) * nchunks, :], sem.at[sl, k])
                for k in range(PROJ_U_SETS)]

    @pl.when(step == 0)
    def _():
        for c in fetch(step, slot):
            c.start()

    @pl.when(step < last)
    def _():
        for c in fetch(step + 1, 1 - slot):
            c.start()

    for c in fetch(step, slot):
        c.wait()
    return slot


def _proj_kernel(x_hbm, xnat_ref, wu_ref, wq_ref, gain_ref,
                 colp_ref, bre_ref, bim_ref, cre_ref, cim_ref, am_ref, d_ref,
                 u_ref, xn_ref, qkv_ref, w1_ref, w3_ref, pwr_ref, pwi_ref, sm_ref, xbuf, sem,
                 *, levels, norm_heads):
    slot = _fetch_row_sets(x_hbm, xbuf, sem)
    proj_u = _proj_t_stages(xbuf.at[slot], wu_ref, None, u_ref, 0, xn_ref)
    proj_q = _proj_t_stages(xnat_ref, wq_ref, gain_ref, qkv_ref, norm_heads)
    prep = _ssm_prep_stages(colp_ref, bre_ref, bim_ref, cre_ref, cim_ref, am_ref, d_ref,
                            w1_ref, w3_ref, pwr_ref, pwi_ref, sm_ref, levels)
    mixed = [st for pair in zip(proj_q, proj_u) for st in pair]
    for stage in mixed[:3] + prep[:1] + mixed[3:5] + prep[1:] + mixed[5:]:
        stage()


def _projections_and_ssm_prep(x, wu_t, wq_t, qk_gain, norm_heads, colp, b_re, b_im, c_re, c_im,
                              a_meta, d_grp, levels):
    b, s, d = x.shape
    nu, nq = wu_t.shape[0], wq_t.shape[0]
    nchunks = s // CHUNK
    x4 = x.reshape(b, nchunks, CHUNK, d)
    nj = CHUNK // PROJ_U_SETS
    tm = s // nj
    g, ns, gc = SSM_GROUPS, SSM_STATE, SSM_GROUP_CH
    gpp = g // (b * nj)
    assert gpp * b * nj == g
    gspec = lambda *tail: pl.BlockSpec((gpp,) + tail, lambda i, j: (i * nj + j,) + (0,) * len(tail))
    const = lambda a: pl.BlockSpec(a.shape, lambda i, j: (0,) * a.ndim)
    return pl.pallas_call(
        functools.partial(_proj_kernel, levels=levels, norm_heads=norm_heads),
        grid=(b, nj),
        in_specs=[
            pl.BlockSpec(memory_space=pl.ANY),
            pl.BlockSpec((None, tm, d), lambda i, j: (i, j, 0)),
            const(wu_t), const(wq_t), const(qk_gain),
            gspec(ns, 4), gspec(ns, gc), gspec(ns, gc), gspec(gc, ns), gspec(gc, ns), gspec(1, TOEP),
            gspec(gc, 1),
        ],
        out_specs=[
            pl.BlockSpec((None, nu, PROJ_U_SETS * nchunks), lambda i, j: (i, 0, j)),
            pl.BlockSpec((None, PROJ_U_SETS * nchunks, d), lambda i, j: (i, j, 0)),
            pl.BlockSpec((None, nq, tm), lambda i, j: (i, 0, j)),
            gspec(2 * ns, TOEP), gspec(TOEP, TOEP + 2 * ns), gspec(levels, ns, LANES),
            gspec(levels, ns, LANES), gspec(4, ns, LANES),
        ],
        out_shape=[jax.ShapeDtypeStruct((b, nu, s), _BF16),
                   jax.ShapeDtypeStruct((b, s, d), _BF16),
                   jax.ShapeDtypeStruct((b, nq, s), _BF16),
                   jax.ShapeDtypeStruct((g, 2 * ns, TOEP), _BF16),
                   jax.ShapeDtypeStruct((g, TOEP, TOEP + 2 * ns), _BF16),
                   jax.ShapeDtypeStruct((g, levels, ns, LANES), _F32),
                   jax.ShapeDtypeStruct((g, levels, ns, LANES), _F32),
                   jax.ShapeDtypeStruct((g, 4, ns, LANES), _F32)],
        scratch_shapes=[pltpu.VMEM((2, PROJ_U_SETS * nchunks, d), x.dtype),
                        pltpu.SemaphoreType.DMA((2, PROJ_U_SETS))],
        compiler_params=pltpu.CompilerParams(
            dimension_semantics=("arbitrary", "arbitrary"), vmem_limit_bytes=VMEM_LIMIT),
        name="projections_ssm_prep",
    )(x4, x, wu_t, wq_t, qk_gain, colp, b_re, b_im, c_re, c_im, a_meta, d_grp)


def _ssm_prep_stages(colp_ref, bre_ref, bim_ref, cre_ref, cim_ref, am_ref, d_ref,
                     w1_ref, w3_ref, pwr_ref, pwi_ref, sm_ref, levels):
    ns, gc = SSM_STATE, SSM_GROUP_CH

    def split3(a):
        hi = a.astype(_BF16)
        rest = a - hi.astype(_F32)
        mid = rest.astype(_BF16)
        return hi, mid, (rest - mid.astype(_F32)).astype(_BF16)

    pick_cols = lambda a, m3: jnp.dot(jnp.concatenate(split3(a), axis=1), m3, preferred_element_type=_F32)
    pick_rows = lambda m3, a: jnp.dot(m3, jnp.concatenate(split3(a), axis=0), preferred_element_type=_F32)
    l16 = lax.broadcasted_iota(jnp.int32, (gc, TOEP), 1)
    r16 = lax.broadcasted_iota(jnp.int32, (gc, TOEP), 0)
    rc = lax.broadcasted_iota(jnp.int32, (3 * gc, TOEP), 0)
    lc = lax.broadcasted_iota(jnp.int32, (3 * gc, TOEP), 1)
    tile_c3 = (lc % gc == rc % gc).astype(_BF16)
    tau_r = lax.broadcasted_iota(jnp.int32, (3 * LANES, TOEP), 0) % LANES
    jc_l = lax.broadcasted_iota(jnp.int32, (3 * LANES, TOEP), 1)
    pick_rev3 = (tau_r == CHUNK - 1 - jc_l // gc).astype(_BF16)
    tc_r = lax.broadcasted_iota(jnp.int32, (TOEP, 3 * LANES), 0)
    tau_l = lax.broadcasted_iota(jnp.int32, (TOEP, 3 * LANES), 1) % LANES
    pick_next3 = (tau_l == tc_r // gc + 1).astype(_BF16)
    lane = lax.broadcasted_iota(jnp.int32, (ns, LANES), 1)
    lvl = jnp.clip(lane - CHUNK, 0, levels - 1)
    expo = jnp.where(lane <= CHUNK, lane,
                     jnp.where(lane < CHUNK + levels, CHUNK * jnp.left_shift(1, lvl), 0)).astype(_F32)
    is_tau = lane <= CHUNK

    def group(gl):
        lr, li, dt = colp_ref[gl, :, 0:1], colp_ref[gl, :, 1:2], colp_ref[gl, :, 2:3]
        mag = jnp.exp(lr * dt * expo)
        ang = li * dt * expo
        pw_r, pw_i = mag * jnp.cos(ang), mag * jnp.sin(ang)
        ar, ai = pw_r[:, 1:2], pw_i[:, 1:2]
        den = lr * lr + li * li
        nr, ni = ar - 1.0, ai
        fr, fi = (nr * lr + ni * li) / den, (ni * lr - nr * li) / den
        b_re, b_im = bre_ref[gl], bim_ref[gl]
        bbr_t = pick_cols(fr * b_re - fi * b_im, tile_c3)
        bbi_t = pick_cols(fr * b_im + fi * b_re, tile_c3)
        tau_re, tau_im = jnp.where(is_tau, pw_r, 0.0), jnp.where(is_tau, pw_i, 0.0)
        pr, pi = pick_cols(tau_re, pick_rev3), pick_cols(tau_im, pick_rev3)
        xr = pr * bbr_t - pi * bbi_t
        xi = pr * bbi_t + pi * bbr_t
        krev = (jnp.dot(cre_ref[gl], xr, precision=_HI, preferred_element_type=_F32)
                - jnp.dot(cim_ref[gl], xi, precision=_HI, preferred_element_type=_F32))
        krev = krev + jnp.where(l16 - (CHUNK - 1) * gc == r16, d_ref[gl], 0.0)
        rz = jnp.concatenate([krev, jnp.zeros_like(krev)], axis=1)
        toep = jnp.concatenate(
            [rz[:, (CHUNK - 1 - t) * gc:(CHUNK - 1 - t) * gc + TOEP] for t in range(CHUNK)], axis=0)
        w1_ref[gl] = jnp.concatenate([xr, xi], axis=0).astype(w1_ref.dtype)
        gar = pick_rows(pick_next3, tau_re.T)
        gai = pick_rows(pick_next3, tau_im.T)
        crt = jnp.tile(cre_ref[gl], (CHUNK, 1))
        cit = jnp.tile(cim_ref[gl], (CHUNK, 1))
        w3_ref[gl] = jnp.concatenate([toep, crt * gar - cit * gai, -(crt * gai + cit * gar)],
                                     axis=1).astype(w3_ref.dtype)
        for l in range(levels):
            pwr_ref[gl, l] = jnp.broadcast_to(pw_r[:, CHUNK + l:CHUNK + l + 1], (ns, LANES))
            pwi_ref[gl, l] = jnp.broadcast_to(pw_i[:, CHUNK + l:CHUNK + l + 1], (ns, LANES))
        am = am_ref[gl]
        sr = jnp.sum(xr * am, axis=1, keepdims=True)
        si = jnp.sum(xi * am, axis=1, keepdims=True)
        a16r, a16i = pw_r[:, CHUNK:CHUNK + 1], pw_i[:, CHUNK:CHUNK + 1]
        sm_ref[gl, 0] = jnp.broadcast_to(sr, (ns, LANES))
        sm_ref[gl, 1] = jnp.broadcast_to(si, (ns, LANES))
        sm_ref[gl, 2] = jnp.broadcast_to(a16r * sr - a16i * si, (ns, LANES))
        sm_ref[gl, 3] = jnp.broadcast_to(a16r * si + a16i * sr, (ns, LANES))

    return [functools.partial(group, gl) for gl in range(colp_ref.shape[0])]


def _ssm_stages(u_ref, w1_ref, w3_ref, pwr_ref, pwi_ref, sm_ref, y_ref, bre_ref, bim_ref, nchunks, levels):
    ns, gc, gps = SSM_STATE, SSM_GROUP_CH, GROUPS_PER_STEP
    nblk = nchunks // LANES
    blocks = [slice(h * LANES, (h + 1) * LANES) for h in range(nblk)]
    group_rows = [slice(gl * ns, (gl + 1) * ns) for gl in range(gps)]

    def chunk_operand(gl):
        r0 = gl * gc
        return jnp.concatenate(
            [u_ref[r0:r0 + gc, j * nchunks:(j + 1) * nchunks] for j in range(CHUNK)], axis=0)

    def shifted(vals, sh, first):
        if sh % LANES == 0:
            k = sh // LANES
            return [None if h < k else vals[h - k] for h in range(nblk)]
        lane = lax.broadcasted_iota(jnp.int32, (ns, LANES), 1)
        rot = [pltpu.roll(v, sh, axis=1) for v in vals]
        keep = lane >= sh
        return [jnp.where(keep, rot[h], first if h == 0 else rot[h - 1]) for h in range(nblk)]

    def state_increments():
        lane0 = lax.broadcasted_iota(jnp.int32, (ns, LANES), 1) == 0
        for gl, rs in enumerate(group_rows):
            r1 = jnp.dot(w1_ref[gl], chunk_operand(gl), preferred_element_type=_F32)
            bre_ref[rs, :] = r1[:ns]
            bim_ref[rs, :] = r1[ns:]
            bre_ref[rs, blocks[0]] += jnp.where(lane0, sm_ref[gl, 2], 0.0)
            bim_ref[rs, blocks[0]] += jnp.where(lane0, sm_ref[gl, 3], 0.0)

    def scan_level(lvl):
        for gl, rs in enumerate(group_rows):
            p_re, p_im = pwr_ref[gl, lvl], pwi_ref[gl, lvl]
            s_re = [bre_ref[rs, blk] for blk in blocks]
            s_im = [bim_ref[rs, blk] for blk in blocks]
            t_re, t_im = shifted(s_re, 1 << lvl, 0.0), shifted(s_im, 1 << lvl, 0.0)
            for h in range(nblk):
                if t_re[h] is not None:
                    bre_ref[rs, blocks[h]] = s_re[h] + p_re * t_re[h] - p_im * t_im[h]
                    bim_ref[rs, blocks[h]] = s_im[h] + p_re * t_im[h] + p_im * t_re[h]

    def entering_states():
        for gl, rs in enumerate(group_rows):
            prev_re = shifted([bre_ref[rs, blk] for blk in blocks], 1, sm_ref[gl, 0])
            prev_im = shifted([bim_ref[rs, blk] for blk in blocks], 1, sm_ref[gl, 1])
            for h in range(nblk):
                bre_ref[rs, blocks[h]] = prev_re[h]
                bim_ref[rs, blocks[h]] = prev_im[h]

    def outputs(gl):
        r0, rs = gl * gc, group_rows[gl]
        rhs = jnp.concatenate([chunk_operand(gl), bre_ref[rs, :].astype(_BF16),
                               bim_ref[rs, :].astype(_BF16)], axis=0)
        y = jnp.dot(w3_ref[gl], rhs, preferred_element_type=_F32)
        for t in range(CHUNK):
            y_ref[r0:r0 + gc, t * nchunks:(t + 1) * nchunks] = y[t * gc:(t + 1) * gc, :]

    return ([state_increments] + [functools.partial(scan_level, lvl) for lvl in range(levels)]
            + [entering_states] + [functools.partial(outputs, gl) for gl in range(gps)])


def _ssm_gates_kernel(u_ref, w1_ref, w3_ref, pwr_ref, pwi_ref, sm_ref, xn_ref, wg_ref,
                      y_ref, g_ref, bre_ref, bim_ref, *, nchunks, levels):
    ssm = _ssm_stages(u_ref, w1_ref, w3_ref, pwr_ref, pwi_ref, sm_ref, y_ref, bre_ref, bim_ref,
                      nchunks, levels)
    proj = _proj_rows_stages(xn_ref, wg_ref, g_ref)
    first, stride = 1, max(1, (levels + 1) // len(proj))
    for i, stage in enumerate(ssm):
        if proj and i >= first and (i - first) % stride == 0:
            proj.pop(0)()
        stage()
    for stage in proj:
        stage()


def _ssm_core_and_gates(u_t, w1, w3, pwr, pwi, s_meta, xn, w_gates, levels):
    b, n, s = u_t.shape
    gps, ns = GROUPS_PER_STEP, SSM_STATE
    rows = gps * SSM_GROUP_CH
    nchunks = s // CHUNK
    d = xn.shape[2]
    tm = s // (n // rows)
    tiles = s // tm
    wspec = lambda *tail: pl.BlockSpec((gps,) + tail, lambda gb, i: (gb,) + (0,) * len(tail))
    tile = lambda gb, i: ((gb * b + i) // tiles, (gb * b + i) % tiles, 0)
    return pl.pallas_call(
        functools.partial(_ssm_gates_kernel, nchunks=nchunks, levels=levels),
        grid=(n // rows, b),
        in_specs=[
            pl.BlockSpec((None, rows, s), lambda gb, i: (i, gb, 0)),
            wspec(2 * ns, TOEP), wspec(TOEP, TOEP + 2 * ns),
            wspec(levels, ns, LANES), wspec(levels, ns, LANES), wspec(4, ns, LANES),
            pl.BlockSpec((None, tm, d), tile),
            pl.BlockSpec(w_gates.shape, lambda gb, i: (0, 0)),
        ],
        out_specs=[
            pl.BlockSpec((None, rows, s), lambda gb, i: (i, gb, 0)),
            pl.BlockSpec((None, tm, w_gates.shape[1]), tile),
        ],
        out_shape=[jax.ShapeDtypeStruct((b, n, s), _F32),
                   jax.ShapeDtypeStruct((b, s, w_gates.shape[1]), _BF16)],
        scratch_shapes=[pltpu.VMEM((gps * ns, nchunks), _F32), pltpu.VMEM((gps * ns, nchunks), _F32)],
        compiler_params=pltpu.CompilerParams(
            dimension_semantics=("arbitrary", "arbitrary"), vmem_limit_bytes=VMEM_LIMIT),
        name="ssm_core_gates",
    )(u_t, w1, w3, pwr, pwi, s_meta, xn, w_gates)


def _attn_kernel(q_ref, k_ref, v_ref, kp_ref, vp_ref, km_ref, vm_ref, sink_ref, tri_ref, o_ref, acc_ref):
    first_step = pl.program_id(1) == 0
    wq = GQA_GROUP * WINDOW
    kj = lax.broadcasted_iota(jnp.int32, (WINDOW, wq), 0)
    qi = lax.broadcasted_iota(jnp.int32, (WINDOW, wq), 1) % WINDOW
    in_cur = kj <= qi
    units = [(blk, h) for blk in range(ATTN_BLOCKS) for h in range(N_KV_HEADS)]

    def kv_rows(ref, prev_ref, blk, h):
        rows = slice(h * HEAD_DIM, (h + 1) * HEAD_DIM)
        cur = ref[rows, blk * WINDOW:(blk + 1) * WINDOW]
        prev = prev_ref[rows, :] if blk == 0 else ref[rows, (blk - 1) * WINDOW:blk * WINDOW]
        return cur, prev

    scores = []
    for blk, h in units:
        q4 = jnp.concatenate(
            [q_ref[(h * GQA_GROUP + r) * HEAD_DIM:(h * GQA_GROUP + r + 1) * HEAD_DIM,
                   blk * WINDOW:(blk + 1) * WINDOW] for r in range(GQA_GROUP)], axis=1)
        k_cur, k_prev = kv_rows(k_ref, kp_ref, blk, h)
        k_meta = km_ref[h * HEAD_DIM:(h + 1) * HEAD_DIM, :]
        scores.append(tuple(lax.dot_general(k, q4, _TN, preferred_element_type=_F32)
                            for k in (k_cur, k_prev, k_meta)))
    for (blk, h), (s_cur, s_prev, s_meta) in zip(units, scores):
        if blk == 0:
            s_prev = jnp.where(first_step, -jnp.inf, s_prev)
        s_sel = jnp.where(in_cur, s_cur, s_prev)
        sink = sink_ref[h]
        m = jnp.maximum(jnp.maximum(jnp.max(s_sel, axis=0, keepdims=True),
                                    jnp.max(s_meta, axis=0, keepdims=True)), sink)
        e_sel = jnp.exp2(s_sel - m)
        e_meta = jnp.exp2(s_meta - m)
        den = (jnp.sum(e_sel, axis=0, keepdims=True) + jnp.sum(e_meta, axis=0, keepdims=True)
               + jnp.exp2(sink - m))
        e_bf = e_sel.astype(_BF16)
        p_cur = e_bf * tri_ref[...]
        p_all = jnp.concatenate([p_cur, e_bf - p_cur, e_meta.astype(_BF16)], axis=0)
        v_cur, v_prev = kv_rows(v_ref, vp_ref, blk, h)
        v_all = jnp.concatenate([v_cur, v_prev, vm_ref[h * HEAD_DIM:(h + 1) * HEAD_DIM, :]], axis=1)
        o = jnp.dot(v_all, p_all, preferred_element_type=_F32) * (1.0 / den)
        for r in range(GQA_GROUP):
            hq = h * GQA_GROUP + r
            acc_ref[hq * HEAD_DIM:(hq + 1) * HEAD_DIM, blk * WINDOW:(blk + 1) * WINDOW] = (
                o[:, r * WINDOW:(r + 1) * WINDOW])
    o_ref[...] = acc_ref[...].T.astype(o_ref.dtype)


def _attention(qkv_t, qkv_meta_t, sink_rows):
    b, _, s = qkv_t.shape
    kj = lax.broadcasted_iota(jnp.int32, (WINDOW, GQA_GROUP * WINDOW), 0)
    qi = lax.broadcasted_iota(jnp.int32, (WINDOW, GQA_GROUP * WINDOW), 1) % WINDOW
    tri = (kj <= qi).astype(_BF16)
    tq = ATTN_BLOCKS * WINDOW
    kblk = Q_W // KV_W
    prev_blk = lambda n: jnp.maximum(n * ATTN_BLOCKS - 1, 0)
    return pl.pallas_call(
        _attn_kernel,
        grid=(b, s // tq),
        in_specs=[
            pl.BlockSpec((None, Q_W, tq), lambda i, n: (i, 0, n)),
            pl.BlockSpec((None, KV_W, tq), lambda i, n: (i, kblk, n)),
            pl.BlockSpec((None, KV_W, tq), lambda i, n: (i, kblk + 1, n)),
            pl.BlockSpec((None, KV_W, WINDOW), lambda i, n: (i, kblk, prev_blk(n))),
            pl.BlockSpec((None, KV_W, WINDOW), lambda i, n: (i, kblk + 1, prev_blk(n))),
            pl.BlockSpec((None, KV_W, N_META), lambda i, n: (0, kblk, 0)),
            pl.BlockSpec((None, KV_W, N_META), lambda i, n: (0, kblk + 1, 0)),
            pl.BlockSpec(sink_rows.shape, lambda i, n: (0, 0, 0)),
            pl.BlockSpec(tri.shape, lambda i, n: (0, 0)),
        ],
        out_specs=pl.BlockSpec((None, tq, Q_W), lambda i, n: (i, n, 0)),
        out_shape=jax.ShapeDtypeStruct((b, s, Q_W), _F32),
        scratch_shapes=[pltpu.VMEM((Q_W, tq), _F32)],
        compiler_params=pltpu.CompilerParams(
            dimension_semantics=("arbitrary", "arbitrary"), vmem_limit_bytes=VMEM_LIMIT),
        name="swa_attention",
    )(qkv_t, qkv_t, qkv_t, qkv_t, qkv_t, qkv_meta_t, qkv_meta_t, sink_rows, tri)


def _tail_kernel(x_hbm, attn_hbm, y_ref, g_ref, wglu_ref, abn_ref, sbn_ref, wout_ref,
                 wfi_ref, wfo_ref, o_hbm, xbuf, abuf, obuf, hn_ref, part_ref, sem_x, sem_a, sem_o,
                 *, ntiles, nj):
    d = D_MODEL
    step = pl.program_id(0)
    slot = step % 2

    def fetch(t, sl):
        bb, jj = t // nj, t % nj
        return (pltpu.make_async_copy(*_row_set_copy(x_hbm, xbuf, sem_x, bb, jj, sl)),
                pltpu.make_async_copy(*_row_set_copy(attn_hbm, abuf, sem_a, bb, jj, sl)))

    def put(t, sl):
        dst, src, sem = _row_set_copy(o_hbm, obuf, sem_o, t // nj, t % nj, sl)
        return pltpu.make_async_copy(src, dst, sem)

    @pl.when(step == 0)
    def _():
        for c in fetch(0, 0):
            c.start()
        hn_ref[...] = jnp.zeros_like(hn_ref)
        part_ref[...] = jnp.zeros_like(part_ref)

    @pl.when(step + 1 < ntiles)
    def _():
        for c in fetch(step + 1, 1 - slot):
            c.start()

    @pl.when(step < ntiles)
    def _():
        for c in fetch(step, slot):
            c.wait()

    @pl.when(step >= 3)
    def _():
        put(step - 3, 1 - slot).wait()

    nrows = xbuf.shape[1]
    pieces = [slice(r, r + TAIL_PIECE) for r in range(0, nrows, TAIL_PIECE)]
    (lo0, hi0), (lo1, hi1) = zip(FFN_EDGES[:-1], FFN_EDGES[1:])

    def ffn_chunk(hn, lo, hi):
        gate = jnp.dot(hn, wfi_ref[:, lo:hi], preferred_element_type=_F32)
        up = jnp.dot(hn, wfi_ref[:, D_FF + lo:D_FF + hi], preferred_element_type=_F32)
        act = (jax.nn.silu(gate) * up).astype(_BF16)
        return jnp.dot(act, wfo_ref[lo:hi, :], preferred_element_type=_F32)

    front = []
    for p in pieces:
        z = jax.nn.gelu(y_ref[:, p]).astype(_BF16)
        front.append(lax.dot_general(z, wglu_ref[...], _TN, preferred_element_type=_F32))
    for p in pieces:
        obuf[1 - slot, p, :] = part_ref[p, :] + ffn_chunk(hn_ref[p, :], lo1, hi1)
    hs = []
    for p, zz in zip(pieces, front):
        ssm = zz[:, :d] * jax.nn.sigmoid(zz[:, d:])
        merged = (jax.nn.sigmoid(g_ref[p, :d].astype(_F32)) * _rms_rows(abuf[slot, p, :], abn_ref[...])
                  + jax.nn.sigmoid(g_ref[p, d:].astype(_F32)) * _rms_rows(ssm, sbn_ref[...]))
        h = xbuf[slot, p, :] + jnp.dot(merged.astype(_BF16), wout_ref[...], preferred_element_type=_F32)
        hs.append((h, _unit_rms(h).astype(_BF16)))
    for p, (h, hn) in zip(pieces, hs):
        part_ref[p, :] = h + ffn_chunk(hn, lo0, hi0)
        hn_ref[p, :] = hn

    @pl.when(step >= 1)
    def _():
        put(step - 1, 1 - slot).start()

    @pl.when(step == ntiles)
    def _():
        put(step - 1, 1 - slot).wait()
        put(step - 2, slot).wait()


def _tail(x, attn, y_t, gates, w_glu, abn, sbn, w_out, w_fi, w_fo):
    b, s, d = x.shape
    nchunks = s // CHUNK
    ntiles = b * CHUNK
    x4 = x.reshape(b, nchunks, CHUNK, d)
    a4 = attn.reshape(b, nchunks, CHUNK, d)
    const = lambda a: pl.BlockSpec(a.shape, lambda t: (0,) * a.ndim, pipeline_mode=pl.Buffered(1))
    hbm = pl.BlockSpec(memory_space=pl.ANY)
    tile = lambda t: jnp.minimum(t, ntiles - 1)
    out = pl.pallas_call(
        functools.partial(_tail_kernel, ntiles=ntiles, nj=CHUNK),
        grid=(ntiles + 1,),
        in_specs=[
            hbm, hbm,
            pl.BlockSpec((None, d, nchunks), lambda t: (tile(t) // CHUNK, 0, tile(t) % CHUNK)),
            pl.BlockSpec((None, nchunks, gates.shape[2]), lambda t: (tile(t) // CHUNK, tile(t) % CHUNK, 0)),
            const(w_glu), const(abn), const(sbn), const(w_out),
            const(w_fi), const(w_fo),
        ],
        out_specs=hbm,
        out_shape=jax.ShapeDtypeStruct(x4.shape, x.dtype),
        scratch_shapes=[pltpu.VMEM((2, nchunks, d), x.dtype), pltpu.VMEM((2, nchunks, d), attn.dtype),
                        pltpu.VMEM((2, nchunks, d), x.dtype),
                        pltpu.VMEM((nchunks, d), _BF16), pltpu.VMEM((nchunks, d), _F32),
                        pltpu.SemaphoreType.DMA((2,)), pltpu.SemaphoreType.DMA((2,)),
                        pltpu.SemaphoreType.DMA((2,))],
        compiler_params=pltpu.CompilerParams(
            dimension_semantics=("arbitrary",), vmem_limit_bytes=VMEM_LIMIT),
        name="tail",
    )(x4, a4, y_t, gates, w_glu, abn, sbn, w_out, w_fi, w_fo)
    return out.reshape(b, s, d)


def kernel(x, meta_tokens, norm_mix, w_in, q_norm, k_norm, attn_sinks, lam_re, lam_im, log_dt,
           ssm_b_re, ssm_b_im, ssm_c_re, ssm_c_im, ssm_d, w_glu, attn_branch_norm, ssm_branch_norm,
           w_out, norm_ffn, w_ffn_in, w_ffn_out):
    seq = x.shape[1]
    levels = (seq // CHUNK).bit_length() - 1
    w = w_in[0] * norm_mix[0].astype(_F32)[:, None]
    w_qkv_t = w[:, :QKV_W].T.astype(_BF16)
    w_u_t = w[:, QKV_W:QKV_W + D_MODEL].T.astype(_BF16)
    w_g = w[:, QKV_W + D_MODEL:].astype(_BF16)
    scale = HEAD_DIM ** -0.5 * LOG2_E
    qk_gain = jnp.concatenate([jnp.tile(q_norm[0] * scale, N_Q_HEADS),
                               jnp.tile(k_norm[0], N_KV_HEADS)])[:, None].astype(_F32)
    no_gain = jnp.zeros((8, 1), _F32)
    sink_rows = jnp.repeat((attn_sinks[0] * LOG2_E).reshape(N_KV_HEADS, 1, GQA_GROUP), WINDOW,
                           axis=2).astype(_F32)
    dt = jnp.exp(log_dt[0].astype(_F32))
    lr, li = lam_re[0].astype(_F32), lam_im[0].astype(_F32)
    dtb = jnp.broadcast_to(dt[:, None], lr.shape)
    colp = jnp.stack([lr, li, dtb, jnp.zeros_like(lr)], axis=2)
    d_grp = ssm_d[0].astype(_F32).reshape(SSM_GROUPS, SSM_GROUP_CH, 1)

    meta = meta_tokens.astype(_F32)[None]
    qkv_meta_t = _proj_t(meta, w_qkv_t, qk_gain, norm_heads=N_Q_HEADS + N_KV_HEADS,
                         tm=N_META, name="proj_qkv_meta")
    u_meta_t = _proj_t(meta, w_u_t, no_gain, norm_heads=0, tm=N_META, name="proj_u_meta")
    a_meta = (u_meta_t[0].astype(_F32).reshape(SSM_GROUPS, SSM_GROUP_CH, CHUNK)
              .transpose(0, 2, 1).reshape(SSM_GROUPS, 1, TOEP))

    u_t, xn, qkv_t, w1, w3, pwr, pwi, s_meta = _projections_and_ssm_prep(
        x, w_u_t, w_qkv_t, qk_gain, N_Q_HEADS + N_KV_HEADS, colp,
        ssm_b_re[0].astype(_F32), ssm_b_im[0].astype(_F32),
        ssm_c_re[0].astype(_F32), ssm_c_im[0].astype(_F32), a_meta, d_grp, levels)
    y_t, gates = _ssm_core_and_gates(u_t, w1, w3, pwr, pwi, s_meta, xn, w_g, levels)
    attn = _attention(qkv_t, qkv_meta_t, sink_rows)
    return _tail(x, attn, y_t, gates, w_glu[0].astype(_BF16), attn_branch_norm,
                 ssm_branch_norm, w_out[0].astype(_BF16),
                 (w_ffn_in[0] * norm_ffn[0].astype(_F32)[:, None]).astype(_BF16),
                 w_ffn_out[0].astype(_BF16))
```

```python
import functools

import jax
import jax.numpy as jnp
from jax import lax
from jax.experimental import pallas as pl
from jax.experimental.pallas import tpu as pltpu

D_MODEL = 1024
N_META = 16
HEAD_DIM = 64
N_Q_HEADS = 16
N_KV_HEADS = 4
GQA_GROUP = N_Q_HEADS // N_KV_HEADS
WINDOW = 128
SSM_GROUP_CH = 16
SSM_GROUPS = D_MODEL // SSM_GROUP_CH
SSM_STATE = 64
D_FF = 2816
Q_W = N_Q_HEADS * HEAD_DIM
KV_W = N_KV_HEADS * HEAD_DIM
QKV_W = Q_W + 2 * KV_W
EPS = 1e-6
LOG2_E = 1.4426950408889634

LANES = 128
CHUNK = 16
TOEP = CHUNK * SSM_GROUP_CH
GROUPS_PER_STEP = 8
MXU_DIM = 256
FFN_EDGES = (0, 6 * MXU_DIM, D_FF)
ATTN_BLOCKS = 16
PROJ_PIECE = 256
TAIL_PIECE = 256
PROJ_U_SETS = 2
VMEM_LIMIT = 56 * 1024 * 1024

_F32 = jnp.float32
_BF16 = jnp.bfloat16
_NT = (((1,), (1,)), ((), ()))
_TN = (((0,), (0,)), ((), ()))
_HI = lax.Precision.HIGHEST


def _rms_rows(x, gain):
    return x * lax.rsqrt(jnp.mean(x * x, axis=-1, keepdims=True) + EPS) * gain


def _unit_rms(x):
    return x * lax.rsqrt(jnp.mean(x * x, axis=-1, keepdims=True) + EPS)


def _proj_rows_stages(xn_ref, w_ref, o_ref):
    tm = min(xn_ref.shape[0], PROJ_PIECE)

    def piece(c):
        rows = slice(c * tm, (c + 1) * tm)
        o_ref[rows, :] = jnp.dot(xn_ref[rows, :], w_ref[...], preferred_element_type=_F32).astype(o_ref.dtype)

    return [functools.partial(piece, c) for c in range(xn_ref.shape[0] // tm)]


def _proj_t_stages(x_ref, w_ref, gain_ref, o_ref, norm_heads, xn_ref=None):
    tm_all = x_ref.shape[0]
    tm = min(tm_all, PROJ_PIECE)

    def piece(c):
        cols = slice(c * tm, (c + 1) * tm)
        xn = _unit_rms(x_ref[cols, :]).astype(_BF16)
        if xn_ref is not None:
            xn_ref[cols, :] = xn
        p = lax.dot_general(w_ref[...], xn, _NT, preferred_element_type=_F32)
        if norm_heads:
            rows = norm_heads * HEAD_DIM
            hd = p[:rows].reshape(norm_heads, HEAD_DIM, tm)
            ms = jnp.mean(hd * hd, axis=1, keepdims=True)
            hd = hd * lax.rsqrt(ms + EPS) * gain_ref[...].reshape(norm_heads, HEAD_DIM, 1)
            o_ref[:rows, cols] = hd.reshape(rows, tm).astype(o_ref.dtype)
            o_ref[rows:, cols] = p[rows:].astype(o_ref.dtype)
        else:
            o_ref[:, cols] = p.astype(o_ref.dtype)

    return [functools.partial(piece, c) for c in range(tm_all // tm)]


def _proj_t_kernel(x_ref, w_ref, gain_ref, o_ref, *, norm_heads):
    for stage in _proj_t_stages(x_ref, w_ref, gain_ref, o_ref, norm_heads):
        stage()


def _proj_t(x, w_t, gain, *, norm_heads, tm, name):
    b, s, d = x.shape
    n = w_t.shape[0]
    return pl.pallas_call(
        functools.partial(_proj_t_kernel, norm_heads=norm_heads),
        grid=(b, s // tm),
        in_specs=[
            pl.BlockSpec((None, tm, d), lambda i, j: (i, j, 0)),
            pl.BlockSpec((n, d), lambda i, j: (0, 0)),
            pl.BlockSpec(gain.shape, lambda i, j: (0, 0)),
        ],
        out_specs=pl.BlockSpec((None, n, tm), lambda i, j: (i, 0, j)),
        out_shape=jax.ShapeDtypeStruct((b, n, s), _BF16),
        compiler_params=pltpu.CompilerParams(
            dimension_semantics=("arbitrary", "arbitrary"), vmem_limit_bytes=VMEM_LIMIT),
        name=name,
    )(x, w_t, gain)


def _row_set_copy(hbm4, buf, sem, bb, jj, slot):
    return hbm4.at[bb, :, jj, :], buf.at[slot], sem.at[slot]


def _fetch_row_sets(x_hbm, xbuf, sem):
    nj = CHUNK // PROJ_U_SETS
    last = pl.num_programs(0) * pl.num_programs(1) - 1
    step = pl.program_id(0) * pl.num_programs(1) + pl.program_id(1)
    slot = step % 2
    nchunks = x_hbm.shape[1]

    def fetch(s, sl):
        return [pltpu.make_async_copy(x_hbm.at[s // nj, :, (s % nj) * PROJ_U_SETS + k, :],
                                      xbuf.at[sl, k * nchunks:(k + 1) * nchunks, :], sem.at[sl, k])
                for k in range(PROJ_U_SETS)]

    @pl.when(step == 0)
    def _():
        for c in fetch(step, slot):
            c.start()

    @pl.when(step < last)
    def _():
        for c in fetch(step + 1, 1 - slot):
            c.start()

    for c in fetch(step, slot):
        c.wait()
    return slot


def _proj_kernel(x_hbm, xnat_ref, wu_ref, wq_ref, gain_ref,
                 colp_ref, bre_ref, bim_ref, cre_ref, cim_ref, am_ref, d_ref,
                 u_ref, xn_ref, qkv_ref, w1_ref, w3_ref, pwr_ref, pwi_ref, sm_ref, xbuf, sem,
                 *, levels, norm_heads):
    slot = _fetch_row_sets(x_hbm, xbuf, sem)
    proj_u = _proj_t_stages(xbuf.at[slot], wu_ref, None, u_ref, 0, xn_ref)
    proj_q = _proj_t_stages(xnat_ref, wq_ref, gain_ref, qkv_ref, norm_heads)
    prep = _ssm_prep_stages(colp_ref, bre_ref, bim_ref, cre_ref, cim_ref, am_ref, d_ref,
                            w1_ref, w3_ref, pwr_ref, pwi_ref, sm_ref, levels)
    mixed = [st for pair in zip(proj_q, proj_u) for st in pair]
    for stage in mixed[:3] + prep[:1] + mixed[3:5] + prep[1:] + mixed[5:]:
        stage()


def _projections_and_ssm_prep(x, wu_t, wq_t, qk_gain, norm_heads, colp, b_re, b_im, c_re, c_im,
                              a_meta, d_grp, levels):
    b, s, d = x.shape
    nu, nq = wu_t.shape[0], wq_t.shape[0]
    nchunks = s // CHUNK
    x4 = x.reshape(b, nchunks, CHUNK, d)
    nj = CHUNK // PROJ_U_SETS
    tm = s // nj
    g, ns, gc = SSM_GROUPS, SSM_STATE, SSM_GROUP_CH
    gpp = g // (b * nj)
    assert gpp * b * nj == g
    gspec = lambda *tail: pl.BlockSpec((gpp,) + tail, lambda i, j: (i * nj + j,) + (0,) * len(tail))
    const = lambda a: pl.BlockSpec(a.shape, lambda i, j: (0,) * a.ndim)
    return pl.pallas_call(
        functools.partial(_proj_kernel, levels=levels, norm_heads=norm_heads),
        grid=(b, nj),
        in_specs=[
            pl.BlockSpec(memory_space=pl.ANY),
            pl.BlockSpec((None, tm, d), lambda i, j: (i, j, 0)),
            const(wu_t), const(wq_t), const(qk_gain),
            gspec(ns, 4), gspec(ns, gc), gspec(ns, gc), gspec(gc, ns), gspec(gc, ns), gspec(1, TOEP),
            gspec(gc, 1),
        ],
        out_specs=[
            pl.BlockSpec((None, nu, PROJ_U_SETS * nchunks), lambda i, j: (i, 0, j)),
            pl.BlockSpec((None, PROJ_U_SETS * nchunks, d), lambda i, j: (i, j, 0)),
            pl.BlockSpec((None, nq, tm), lambda i, j: (i, 0, j)),
            gspec(2 * ns, TOEP), gspec(TOEP, TOEP + 2 * ns), gspec(levels, ns, LANES),
            gspec(levels, ns, LANES), gspec(4, ns, LANES),
        ],
        out_shape=[jax.ShapeDtypeStruct((b, nu, s), _BF16),
                   jax.ShapeDtypeStruct((b, s, d), _BF16),
                   jax.ShapeDtypeStruct((b, nq, s), _BF16),
                   jax.ShapeDtypeStruct((g, 2 * ns, TOEP), _BF16),
                   jax.ShapeDtypeStruct((g, TOEP, TOEP + 2 * ns), _BF16),
                   jax.ShapeDtypeStruct((g, levels, ns, LANES), _F32),
                   jax.ShapeDtypeStruct((g, levels, ns, LANES), _F32),
                   jax.ShapeDtypeStruct((g, 4, ns, LANES), _F32)],
        scratch_shapes=[pltpu.VMEM((2, PROJ_U_SETS * nchunks, d), x.dtype),
                        pltpu.SemaphoreType.DMA((2, PROJ_U_SETS))],
        compiler_params=pltpu.CompilerParams(
            dimension_semantics=("arbitrary", "arbitrary"), vmem_limit_bytes=VMEM_LIMIT),
        name="projections_ssm_prep",
    )(x4, x, wu_t, wq_t, qk_gain, colp, b_re, b_im, c_re, c_im, a_meta, d_grp)


def _ssm_prep_stages(colp_ref, bre_ref, bim_ref, cre_ref, cim_ref, am_ref, d_ref,
                     w1_ref, w3_ref, pwr_ref, pwi_ref, sm_ref, levels):
    ns, gc = SSM_STATE, SSM_GROUP_CH

    def split3(a):
        hi = a.astype(_BF16)
        rest = a - hi.astype(_F32)
        mid = rest.astype(_BF16)
        return hi, mid, (rest - mid.astype(_F32)).astype(_BF16)

    pick_cols = lambda a, m3: jnp.dot(jnp.concatenate(split3(a), axis=1), m3, preferred_element_type=_F32)
    pick_rows = lambda m3, a: jnp.dot(m3, jnp.concatenate(split3(a), axis=0), preferred_element_type=_F32)
    l16 = lax.broadcasted_iota(jnp.int32, (gc, TOEP), 1)
    r16 = lax.broadcasted_iota(jnp.int32, (gc, TOEP), 0)
    rc = lax.broadcasted_iota(jnp.int32, (3 * gc, TOEP), 0)
    lc = lax.broadcasted_iota(jnp.int32, (3 * gc, TOEP), 1)
    tile_c3 = (lc % gc == rc % gc).astype(_BF16)
    tau_r = lax.broadcasted_iota(jnp.int32, (3 * LANES, TOEP), 0) % LANES
    jc_l = lax.broadcasted_iota(jnp.int32, (3 * LANES, TOEP), 1)
    pick_rev3 = (tau_r == CHUNK - 1 - jc_l // gc).astype(_BF16)
    tc_r = lax.broadcasted_iota(jnp.int32, (TOEP, 3 * LANES), 0)
    tau_l = lax.broadcasted_iota(jnp.int32, (TOEP, 3 * LANES), 1) % LANES
    pick_next3 = (tau_l == tc_r // gc + 1).astype(_BF16)
    lane = lax.broadcasted_iota(jnp.int32, (ns, LANES), 1)
    lvl = jnp.clip(lane - CHUNK, 0, levels - 1)
    expo = jnp.where(lane <= CHUNK, lane,
                     jnp.where(lane < CHUNK + levels, CHUNK * jnp.left_shift(1, lvl), 0)).astype(_F32)
    is_tau = lane <= CHUNK

    def group(gl):
        lr, li, dt = colp_ref[gl, :, 0:1], colp_ref[gl, :, 1:2], colp_ref[gl, :, 2:3]
        mag = jnp.exp(lr * dt * expo)
        ang = li * dt * expo
        pw_r, pw_i = mag * jnp.cos(ang), mag * jnp.sin(ang)
        ar, ai = pw_r[:, 1:2], pw_i[:, 1:2]
        den = lr * lr + li * li
        nr, ni = ar - 1.0, ai
        fr, fi = (nr * lr + ni * li) / den, (ni * lr - nr * li) / den
        b_re, b_im = bre_ref[gl], bim_ref[gl]
        bbr_t = pick_cols(fr * b_re - fi * b_im, tile_c3)
        bbi_t = pick_cols(fr * b_im + fi * b_re, tile_c3)
        tau_re, tau_im = jnp.where(is_tau, pw_r, 0.0), jnp.where(is_tau, pw_i, 0.0)
        pr, pi = pick_cols(tau_re, pick_rev3), pick_cols(tau_im, pick_rev3)
        xr = pr * bbr_t - pi * bbi_t
        xi = pr * bbi_t + pi * bbr_t
        krev = (jnp.dot(cre_ref[gl], xr, precision=_HI, preferred_element_type=_F32)
                - jnp.dot(cim_ref[gl], xi, precision=_HI, preferred_element_type=_F32))
        krev = krev + jnp.where(l16 - (CHUNK - 1) * gc == r16, d_ref[gl], 0.0)
        rz = jnp.concatenate([krev, jnp.zeros_like(krev)], axis=1)
        toep = jnp.concatenate(
            [rz[:, (CHUNK - 1 - t) * gc:(CHUNK - 1 - t) * gc + TOEP] for t in range(CHUNK)], axis=0)
        w1_ref[gl] = jnp.concatenate([xr, xi], axis=0).astype(w1_ref.dtype)
        gar = pick_rows(pick_next3, tau_re.T)
        gai = pick_rows(pick_next3, tau_im.T)
        crt = jnp.tile(cre_ref[gl], (CHUNK, 1))
        cit = jnp.tile(cim_ref[gl], (CHUNK, 1))
        w3_ref[gl] = jnp.concatenate([toep, crt * gar - cit * gai, -(crt * gai + cit * gar)],
                                     axis=1).astype(w3_ref.dtype)
        for l in range(levels):
            pwr_ref[gl, l] = jnp.broadcast_to(pw_r[:, CHUNK + l:CHUNK + l + 1], (ns, LANES))
            pwi_ref[gl, l] = jnp.broadcast_to(pw_i[:, CHUNK + l:CHUNK + l + 1], (ns, LANES))
        am = am_ref[gl]
        sr = jnp.sum(xr * am, axis=1, keepdims=True)
        si = jnp.sum(xi * am, axis=1, keepdims=True)
        a16r, a16i = pw_r[:, CHUNK:CHUNK + 1], pw_i[:, CHUNK:CHUNK + 1]
        sm_ref[gl, 0] = jnp.broadcast_to(sr, (ns, LANES))
        sm_ref[gl, 1] = jnp.broadcast_to(si, (ns, LANES))
        sm_ref[gl, 2] = jnp.broadcast_to(a16r * sr - a16i * si, (ns, LANES))
        sm_ref[gl, 3] = jnp.broadcast_to(a16r * si + a16i * sr, (ns, LANES))

    return [functools.partial(group, gl) for gl in range(colp_ref.shape[0])]


def _ssm_stages(u_ref, w1_ref, w3_ref, pwr_ref, pwi_ref, sm_ref, y_ref, bre_ref, bim_ref, nchunks, levels):
    ns, gc, gps = SSM_STATE, SSM_GROUP_CH, GROUPS_PER_STEP
    nblk = nchunks // LANES
    blocks = [slice(h * LANES, (h + 1) * LANES) for h in range(nblk)]
    group_rows = [slice(gl * ns, (gl + 1) * ns) for gl in range(gps)]

    def chunk_operand(gl):
        r0 = gl * gc
        return jnp.concatenate(
            [u_ref[r0:r0 + gc, j * nchunks:(j + 1) * nchunks] for j in range(CHUNK)], axis=0)

    def shifted(vals, sh, first):
        if sh % LANES == 0:
            k = sh // LANES
            return [None if h < k else vals[h - k] for h in range(nblk)]
        lane = lax.broadcasted_iota(jnp.int32, (ns, LANES), 1)
        rot = [pltpu.roll(v, sh, axis=1) for v in vals]
        keep = lane >= sh
        return [jnp.where(keep, rot[h], first if h == 0 else rot[h - 1]) for h in range(nblk)]

    def state_increments():
        lane0 = lax.broadcasted_iota(jnp.int32, (ns, LANES), 1) == 0
        for gl, rs in enumerate(group_rows):
            r1 = jnp.dot(w1_ref[gl], chunk_operand(gl), preferred_element_type=_F32)
            bre_ref[rs, :] = r1[:ns]
            bim_ref[rs, :] = r1[ns:]
            bre_ref[rs, blocks[0]] += jnp.where(lane0, sm_ref[gl, 2], 0.0)
            bim_ref[rs, blocks[0]] += jnp.where(lane0, sm_ref[gl, 3], 0.0)

    def scan_level(lvl):
        for gl, rs in enumerate(group_rows):
            p_re, p_im = pwr_ref[gl, lvl], pwi_ref[gl, lvl]
            s_re = [bre_ref[rs, blk] for blk in blocks]
            s_im = [bim_ref[rs, blk] for blk in blocks]
            t_re, t_im = shifted(s_re, 1 << lvl, 0.0), shifted(s_im, 1 << lvl, 0.0)
            for h in range(nblk):
                if t_re[h] is not None:
                    bre_ref[rs, blocks[h]] = s_re[h] + p_re * t_re[h] - p_im * t_im[h]
                    bim_ref[rs, blocks[h]] = s_im[h] + p_re * t_im[h] + p_im * t_re[h]

    def outputs(gl):
        r0, rs = gl * gc, group_rows[gl]
        prev_re = shifted([bre_ref[rs, blk] for blk in blocks], 1, sm_ref[gl, 0])
        prev_im = shifted([bim_ref[rs, blk] for blk in blocks], 1, sm_ref[gl, 1])
        rhs = jnp.concatenate([chunk_operand(gl), jnp.concatenate(prev_re, axis=1).astype(_BF16),
                               jnp.concatenate(prev_im, axis=1).astype(_BF16)], axis=0)
        y = jnp.dot(w3_ref[gl], rhs, preferred_element_type=_F32)
        for t in range(CHUNK):
            y_ref[r0:r0 + gc, t * nchunks:(t + 1) * nchunks] = y[t * gc:(t + 1) * gc, :]

    return ([state_increments] + [functools.partial(scan_level, lvl) for lvl in range(levels)]
            + [functools.partial(outputs, gl) for gl in range(gps)])


def _ssm_gates_kernel(u_ref, w1_ref, w3_ref, pwr_ref, pwi_ref, sm_ref, xn_ref, wg_ref,
                      y_ref, g_ref, bre_ref, bim_ref, *, nchunks, levels):
    ssm = _ssm_stages(u_ref, w1_ref, w3_ref, pwr_ref, pwi_ref, sm_ref, y_ref, bre_ref, bim_ref,
                      nchunks, levels)
    proj = _proj_rows_stages(xn_ref, wg_ref, g_ref)
    first, stride = 1, max(1, (levels + 1) // len(proj))
    for i, stage in enumerate(ssm):
        if proj and i >= first and (i - first) % stride == 0:
            proj.pop(0)()
        stage()
    for stage in proj:
        stage()


def _ssm_core_and_gates(u_t, w1, w3, pwr, pwi, s_meta, xn, w_gates, levels):
    b, n, s = u_t.shape
    gps, ns = GROUPS_PER_STEP, SSM_STATE
    rows = gps * SSM_GROUP_CH
    nchunks = s // CHUNK
    d = xn.shape[2]
    tm = s // (n // rows)
    tiles = s // tm
    wspec = lambda *tail: pl.BlockSpec((gps,) + tail, lambda gb, i: (gb,) + (0,) * len(tail))
    tile = lambda gb, i: ((gb * b + i) // tiles, (gb * b + i) % tiles, 0)
    return pl.pallas_call(
        functools.partial(_ssm_gates_kernel, nchunks=nchunks, levels=levels),
        grid=(n // rows, b),
        in_specs=[
            pl.BlockSpec((None, rows, s), lambda gb, i: (i, gb, 0)),
            wspec(2 * ns, TOEP), wspec(TOEP, TOEP + 2 * ns),
            wspec(levels, ns, LANES), wspec(levels, ns, LANES), wspec(4, ns, LANES),
            pl.BlockSpec((None, tm, d), tile),
            pl.BlockSpec(w_gates.shape, lambda gb, i: (0, 0)),
        ],
        out_specs=[
            pl.BlockSpec((None, rows, s), lambda gb, i: (i, gb, 0)),
            pl.BlockSpec((None, tm, w_gates.shape[1]), tile),
        ],
        out_shape=[jax.ShapeDtypeStruct((b, n, s), _F32),
                   jax.ShapeDtypeStruct((b, s, w_gates.shape[1]), _BF16)],
        scratch_shapes=[pltpu.VMEM((gps * ns, nchunks), _F32), pltpu.VMEM((gps * ns, nchunks), _F32)],
        compiler_params=pltpu.CompilerParams(
            dimension_semantics=("arbitrary", "arbitrary"), vmem_limit_bytes=VMEM_LIMIT),
        name="ssm_core_gates",
    )(u_t, w1, w3, pwr, pwi, s_meta, xn, w_gates)


def _attn_kernel(q_ref, k_ref, v_ref, kp_ref, vp_ref, km_ref, vm_ref, sink_ref, tri_ref, o_ref, acc_ref):
    first_step = pl.program_id(1) == 0
    wq = GQA_GROUP * WINDOW
    kj = lax.broadcasted_iota(jnp.int32, (WINDOW, wq), 0)
    qi = lax.broadcasted_iota(jnp.int32, (WINDOW, wq), 1) % WINDOW
    in_cur = kj <= qi
    units = [(blk, h) for blk in range(ATTN_BLOCKS) for h in range(N_KV_HEADS)]

    def kv_rows(ref, prev_ref, blk, h):
        rows = slice(h * HEAD_DIM, (h + 1) * HEAD_DIM)
        cur = ref[rows, blk * WINDOW:(blk + 1) * WINDOW]
        prev = prev_ref[rows, :] if blk == 0 else ref[rows, (blk - 1) * WINDOW:blk * WINDOW]
        return cur, prev

    scores = []
    for blk, h in units:
        q4 = jnp.concatenate(
            [q_ref[(h * GQA_GROUP + r) * HEAD_DIM:(h * GQA_GROUP + r + 1) * HEAD_DIM,
                   blk * WINDOW:(blk + 1) * WINDOW] for r in range(GQA_GROUP)], axis=1)
        k_cur, k_prev = kv_rows(k_ref, kp_ref, blk, h)
        k_meta = km_ref[h * HEAD_DIM:(h + 1) * HEAD_DIM, :]
        scores.append(tuple(lax.dot_general(k, q4, _TN, preferred_element_type=_F32)
                            for k in (k_cur, k_prev, k_meta)))
    for (blk, h), (s_cur, s_prev, s_meta) in zip(units, scores):
        if blk == 0:
            s_prev = jnp.where(first_step, -jnp.inf, s_prev)
        s_sel = jnp.where(in_cur, s_cur, s_prev)
        sink = sink_ref[h]
        m = jnp.maximum(jnp.maximum(jnp.max(s_sel, axis=0, keepdims=True),
                                    jnp.max(s_meta, axis=0, keepdims=True)), sink)
        e_sel = jnp.exp2(s_sel - m)
        e_meta = jnp.exp2(s_meta - m)
        den = (jnp.sum(e_sel, axis=0, keepdims=True) + jnp.sum(e_meta, axis=0, keepdims=True)
               + jnp.exp2(sink - m))
        e_bf = e_sel.astype(_BF16)
        p_cur = e_bf * tri_ref[...]
        p_all = jnp.concatenate([p_cur, e_bf - p_cur, e_meta.astype(_BF16)], axis=0)
        v_cur, v_prev = kv_rows(v_ref, vp_ref, blk, h)
        v_all = jnp.concatenate([v_cur, v_prev, vm_ref[h * HEAD_DIM:(h + 1) * HEAD_DIM, :]], axis=1)
        o = jnp.dot(v_all, p_all, preferred_element_type=_F32) * (1.0 / den)
        for r in range(GQA_GROUP):
            hq = h * GQA_GROUP + r
            acc_ref[hq * HEAD_DIM:(hq + 1) * HEAD_DIM, blk * WINDOW:(blk + 1) * WINDOW] = (
                o[:, r * WINDOW:(r + 1) * WINDOW])
    o_ref[...] = acc_ref[...].T.astype(o_ref.dtype)


def _attention(qkv_t, qkv_meta_t, sink_rows):
    b, _, s = qkv_t.shape
    kj = lax.broadcasted_iota(jnp.int32, (WINDOW, GQA_GROUP * WINDOW), 0)
    qi = lax.broadcasted_iota(jnp.int32, (WINDOW, GQA_GROUP * WINDOW), 1) % WINDOW
    tri = (kj <= qi).astype(_BF16)
    tq = ATTN_BLOCKS * WINDOW
    kblk = Q_W // KV_W
    prev_blk = lambda n: jnp.maximum(n * ATTN_BLOCKS - 1, 0)
    return pl.pallas_call(
        _attn_kernel,
        grid=(b, s // tq),
        in_specs=[
            pl.BlockSpec((None, Q_W, tq), lambda i, n: (i, 0, n)),
            pl.BlockSpec((None, KV_W, tq), lambda i, n: (i, kblk, n)),
            pl.BlockSpec((None, KV_W, tq), lambda i, n: (i, kblk + 1, n)),
            pl.BlockSpec((None, KV_W, WINDOW), lambda i, n: (i, kblk, prev_blk(n))),
            pl.BlockSpec((None, KV_W, WINDOW), lambda i, n: (i, kblk + 1, prev_blk(n))),
            pl.BlockSpec((None, KV_W, N_META), lambda i, n: (0, kblk, 0)),
            pl.BlockSpec((None, KV_W, N_META), lambda i, n: (0, kblk + 1, 0)),
            pl.BlockSpec(sink_rows.shape, lambda i, n: (0, 0, 0)),
            pl.BlockSpec(tri.shape, lambda i, n: (0, 0)),
        ],
        out_specs=pl.BlockSpec((None, tq, Q_W), lambda i, n: (i, n, 0)),
        out_shape=jax.ShapeDtypeStruct((b, s, Q_W), _F32),
        scratch_shapes=[pltpu.VMEM((Q_W, tq), _F32)],
        compiler_params=pltpu.CompilerParams(
            dimension_semantics=("arbitrary", "arbitrary"), vmem_limit_bytes=VMEM_LIMIT),
        name="swa_attention",
    )(qkv_t, qkv_t, qkv_t, qkv_t, qkv_t, qkv_meta_t, qkv_meta_t, sink_rows, tri)


def _tail_kernel(x_hbm, attn_hbm, y_ref, g_ref, wglu_ref, abn_ref, sbn_ref, wout_ref,
                 wfi_ref, wfo_ref, o_hbm, xbuf, abuf, obuf, hn_ref, part_ref, sem_x, sem_a, sem_o,
                 *, ntiles, nj):
    d = D_MODEL
    step = pl.program_id(0)
    slot = step % 2

    def fetch(t, sl):
        bb, jj = t // nj, t % nj
        return (pltpu.make_async_copy(*_row_set_copy(x_hbm, xbuf, sem_x, bb, jj, sl)),
                pltpu.make_async_copy(*_row_set_copy(attn_hbm, abuf, sem_a, bb, jj, sl)))

    def put(t, sl):
        dst, src, sem = _row_set_copy(o_hbm, obuf, sem_o, t // nj, t % nj, sl)
        return pltpu.make_async_copy(src, dst, sem)

    @pl.when(step == 0)
    def _():
        for c in fetch(0, 0):
            c.start()
        hn_ref[...] = jnp.zeros_like(hn_ref)
        part_ref[...] = jnp.zeros_like(part_ref)

    @pl.when(step + 1 < ntiles)
    def _():
        for c in fetch(step + 1, 1 - slot):
            c.start()

    @pl.when(step < ntiles)
    def _():
        for c in fetch(step, slot):
            c.wait()

    @pl.when(step >= 3)
    def _():
        put(step - 3, 1 - slot).wait()

    nrows = xbuf.shape[1]
    pieces = [slice(r, r + TAIL_PIECE) for r in range(0, nrows, TAIL_PIECE)]
    (lo0, hi0), (lo1, hi1) = zip(FFN_EDGES[:-1], FFN_EDGES[1:])

    def ffn_chunk(hn, lo, hi):
        gate = jnp.dot(hn, wfi_ref[:, lo:hi], preferred_element_type=_F32)
        up = jnp.dot(hn, wfi_ref[:, D_FF + lo:D_FF + hi], preferred_element_type=_F32)
        act = (jax.nn.silu(gate) * up).astype(_BF16)
        return jnp.dot(act, wfo_ref[lo:hi, :], preferred_element_type=_F32)

    front = []
    for p in pieces:
        z = jax.nn.gelu(y_ref[:, p]).astype(_BF16)
        front.append(lax.dot_general(z, wglu_ref[...], _TN, preferred_element_type=_F32))
    for p in pieces:
        obuf[1 - slot, p, :] = part_ref[p, :] + ffn_chunk(hn_ref[p, :], lo1, hi1)
    hs = []
    for p, zz in zip(pieces, front):
        ssm = zz[:, :d] * jax.nn.sigmoid(zz[:, d:])
        merged = (jax.nn.sigmoid(g_ref[p, :d].astype(_F32)) * _rms_rows(abuf[slot, p, :], abn_ref[...])
                  + jax.nn.sigmoid(g_ref[p, d:].astype(_F32)) * _rms_rows(ssm, sbn_ref[...]))
        h = xbuf[slot, p, :] + jnp.dot(merged.astype(_BF16), wout_ref[...], preferred_element_type=_F32)
        hs.append((h, _unit_rms(h).astype(_BF16)))
    for p, (h, hn) in zip(pieces, hs):
        part_ref[p, :] = h + ffn_chunk(hn, lo0, hi0)
        hn_ref[p, :] = hn

    @pl.when(step >= 1)
    def _():
        put(step - 1, 1 - slot).start()

    @pl.when(step == ntiles)
    def _():
        put(step - 1, 1 - slot).wait()
        put(step - 2, slot).wait()


def _tail(x, attn, y_t, gates, w_glu, abn, sbn, w_out, w_fi, w_fo):
    b, s, d = x.shape
    nchunks = s // CHUNK
    ntiles = b * CHUNK
    x4 = x.reshape(b, nchunks, CHUNK, d)
    a4 = attn.reshape(b, nchunks, CHUNK, d)
    const = lambda a: pl.BlockSpec(a.shape, lambda t: (0,) * a.ndim, pipeline_mode=pl.Buffered(1))
    hbm = pl.BlockSpec(memory_space=pl.ANY)
    tile = lambda t: jnp.minimum(t, ntiles - 1)
    out = pl.pallas_call(
        functools.partial(_tail_kernel, ntiles=ntiles, nj=CHUNK),
        grid=(ntiles + 1,),
        in_specs=[
            hbm, hbm,
            pl.BlockSpec((None, d, nchunks), lambda t: (tile(t) // CHUNK, 0, tile(t) % CHUNK)),
            pl.BlockSpec((None, nchunks, gates.shape[2]), lambda t: (tile(t) // CHUNK, tile(t) % CHUNK, 0)),
            const(w_glu), const(abn), const(sbn), const(w_out),
            const(w_fi), const(w_fo),
        ],
        out_specs=hbm,
        out_shape=jax.ShapeDtypeStruct(x4.shape, x.dtype),
        scratch_shapes=[pltpu.VMEM((2, nchunks, d), x.dtype), pltpu.VMEM((2, nchunks, d), attn.dtype),
                        pltpu.VMEM((2, nchunks, d), x.dtype),
                        pltpu.VMEM((nchunks, d), _BF16), pltpu.VMEM((nchunks, d), _F32),
                        pltpu.SemaphoreType.DMA((2,)), pltpu.SemaphoreType.DMA((2,)),
                        pltpu.SemaphoreType.DMA((2,))],
        compiler_params=pltpu.CompilerParams(
            dimension_semantics=("arbitrary",), vmem_limit_bytes=VMEM_LIMIT),
        name="tail",
    )(x4, a4, y_t, gates, w_glu, abn, sbn, w_out, w_fi, w_fo)
    return out.reshape(b, s, d)


def kernel(x, meta_tokens, norm_mix, w_in, q_norm, k_norm, attn_sinks, lam_re, lam_im, log_dt,
           ssm_b_re, ssm_b_im, ssm_c_re, ssm_c_im, ssm_d, w_glu, attn_branch_norm, ssm_branch_norm,
           w_out, norm_ffn, w_ffn_in, w_ffn_out):
    seq = x.shape[1]
    levels = (seq // CHUNK).bit_length() - 1
    w = w_in[0] * norm_mix[0].astype(_F32)[:, None]
    w_qkv_t = w[:, :QKV_W].T.astype(_BF16)
    w_u_t = w[:, QKV_W:QKV_W + D_MODEL].T.astype(_BF16)
    w_g = w[:, QKV_W + D_MODEL:].astype(_BF16)
    scale = HEAD_DIM ** -0.5 * LOG2_E
    qk_gain = jnp.concatenate([jnp.tile(q_norm[0] * scale, N_Q_HEADS),
                               jnp.tile(k_norm[0], N_KV_HEADS)])[:, None].astype(_F32)
    no_gain = jnp.zeros((8, 1), _F32)
    sink_rows = jnp.repeat((attn_sinks[0] * LOG2_E).reshape(N_KV_HEADS, 1, GQA_GROUP), WINDOW,
                           axis=2).astype(_F32)
    dt = jnp.exp(log_dt[0].astype(_F32))
    lr, li = lam_re[0].astype(_F32), lam_im[0].astype(_F32)
    dtb = jnp.broadcast_to(dt[:, None], lr.shape)
    colp = jnp.stack([lr, li, dtb, jnp.zeros_like(lr)], axis=2)
    d_grp = ssm_d[0].astype(_F32).reshape(SSM_GROUPS, SSM_GROUP_CH, 1)

    meta = meta_tokens.astype(_F32)[None]
    qkv_meta_t = _proj_t(meta, w_qkv_t, qk_gain, norm_heads=N_Q_HEADS + N_KV_HEADS,
                         tm=N_META, name="proj_qkv_meta")
    u_meta_t = _proj_t(meta, w_u_t, no_gain, norm_heads=0, tm=N_META, name="proj_u_meta")
    a_meta = (u_meta_t[0].astype(_F32).reshape(SSM_GROUPS, SSM_GROUP_CH, CHUNK)
              .transpose(0, 2, 1).reshape(SSM_GROUPS, 1, TOEP))

    u_t, xn, qkv_t, w1, w3, pwr, pwi, s_meta = _projections_and_ssm_prep(
        x, w_u_t, w_qkv_t, qk_gain, N_Q_HEADS + N_KV_HEADS, colp,
        ssm_b_re[0].astype(_F32), ssm_b_im[0].astype(_F32),
        ssm_c_re[0].astype(_F32), ssm_c_im[0].astype(_F32), a_meta, d_grp, levels)
    y_t, gates = _ssm_core_and_gates(u_t, w1, w3, pwr, pwi, s_meta, xn, w_g, levels)
    attn = _attention(qkv_t, qkv_meta_t, sink_rows)
    return _tail(x, attn, y_t, gates, w_glu[0].astype(_BF16), attn_branch_norm,
                 ssm_branch_norm, w_out[0].astype(_BF16),
                 (w_ffn_in[0] * norm_ffn[0].astype(_F32)[:, None]).astype(_BF16),
                 w_ffn_out[0].astype(_BF16))
```

```python
import functools

import jax
import jax.numpy as jnp
from jax import lax
from jax.experimental import pallas as pl
from jax.experimental.pallas import tpu as pltpu

D_MODEL = 1024
N_META = 16
HEAD_DIM = 64
N_Q_HEADS = 16
N_KV_HEADS = 4
GQA_GROUP = N_Q_HEADS // N_KV_HEADS
WINDOW = 128
SSM_GROUP_CH = 16
SSM_GROUPS = D_MODEL // SSM_GROUP_CH
SSM_STATE = 64
D_FF = 2816
Q_W = N_Q_HEADS * HEAD_DIM
KV_W = N_KV_HEADS * HEAD_DIM
QKV_W = Q_W + 2 * KV_W
EPS = 1e-6
LOG2_E = 1.4426950408889634

LANES = 128
CHUNK = 16
TOEP = CHUNK * SSM_GROUP_CH
GROUPS_PER_STEP = 8
MXU_DIM = 256
FFN_EDGES = (0, 6 * MXU_DIM, D_FF)
ATTN_BLOCKS = 16
PROJ_PIECE = 256
TAIL_PIECE = 256
PROJ_U_SETS = 2
VMEM_LIMIT = 56 * 1024 * 1024

_F32 = jnp.float32
_BF16 = jnp.bfloat16
_NT = (((1,), (1,)), ((), ()))
_TN = (((0,), (0,)), ((), ()))
_HI = lax.Precision.HIGHEST


def _rms_rows(x, gain):
    return x * lax.rsqrt(jnp.mean(x * x, axis=-1, keepdims=True) + EPS) * gain


def _unit_rms(x):
    return x * lax.rsqrt(jnp.mean(x * x, axis=-1, keepdims=True) + EPS)


def _proj_rows_stages(xn_ref, w_ref, o_ref):
    tm = min(xn_ref.shape[0], PROJ_PIECE)

    def piece(c):
        rows = slice(c * tm, (c + 1) * tm)
        o_ref[rows, :] = jnp.dot(xn_ref[rows, :], w_ref[...], preferred_element_type=_F32).astype(o_ref.dtype)

    return [functools.partial(piece, c) for c in range(xn_ref.shape[0] // tm)]


def _proj_t_stages(x_ref, w_ref, gain_ref, o_ref, norm_heads, xn_ref=None):
    tm_all = x_ref.shape[0]
    tm = min(tm_all, PROJ_PIECE)

    def piece(c):
        cols = slice(c * tm, (c + 1) * tm)
        xn = _unit_rms(x_ref[cols, :]).astype(_BF16)
        if xn_ref is not None:
            xn_ref[cols, :] = xn
        p = lax.dot_general(w_ref[...], xn, _NT, preferred_element_type=_F32)
        if norm_heads:
            rows = norm_heads * HEAD_DIM
            hd = p[:rows].reshape(norm_heads, HEAD_DIM, tm)
            ms = jnp.mean(hd * hd, axis=1, keepdims=True)
            hd = hd * lax.rsqrt(ms + EPS) * gain_ref[...].reshape(norm_heads, HEAD_DIM, 1)
            o_ref[:rows, cols] = hd.reshape(rows, tm).astype(o_ref.dtype)
            o_ref[rows:, cols] = p[rows:].astype(o_ref.dtype)
        else:
            o_ref[:, cols] = p.astype(o_ref.dtype)

    return [functools.partial(piece, c) for c in range(tm_all // tm)]


def _proj_t_kernel(x_ref, w_ref, gain_ref, o_ref, *, norm_heads):
    for stage in _proj_t_stages(x_ref, w_ref, gain_ref, o_ref, norm_heads):
        stage()


def _proj_t(x, w_t, gain, *, norm_heads, tm, name):
    b, s, d = x.shape
    n = w_t.shape[0]
    return pl.pallas_call(
        functools.partial(_proj_t_kernel, norm_heads=norm_heads),
        grid=(b, s // tm),
        in_specs=[
            pl.BlockSpec((None, tm, d), lambda i, j: (i, j, 0)),
            pl.BlockSpec((n, d), lambda i, j: (0, 0)),
            pl.BlockSpec(gain.shape, lambda i, j: (0, 0)),
        ],
        out_specs=pl.BlockSpec((None, n, tm), lambda i, j: (i, 0, j)),
        out_shape=jax.ShapeDtypeStruct((b, n, s), _BF16),
        compiler_params=pltpu.CompilerParams(
            dimension_semantics=("arbitrary", "arbitrary"), vmem_limit_bytes=VMEM_LIMIT),
        name=name,
    )(x, w_t, gain)


def _row_set_copy(hbm4, buf, sem, bb, jj, slot):
    return hbm4.at[bb, :, jj, :], buf.at[slot], sem.at[slot]


def _fetch_row_sets(x_hbm, xbuf, sem):
    nj = CHUNK // PROJ_U_SETS
    last = pl.num_programs(0) * pl.num_programs(1) - 1
    step = pl.program_id(0) * pl.num_programs(1) + pl.program_id(1)
    slot = step % 2
    nchunks = x_hbm.shape[1]

    def fetch(s, sl):
        return [pltpu.make_async_copy(x_hbm.at[s // nj, :, (s % nj) * PROJ_U_SETS + k, :],
                                      xbuf.at[sl, k * nchunks:(k + 1) * nchunks, :], sem.at[sl, k])
                for k in range(PROJ_U_SETS)]

    @pl.when(step == 0)
    def _():
        for c in fetch(step, slot):
            c.start()

    @pl.when(step < last)
    def _():
        for c in fetch(step + 1, 1 - slot):
            c.start()

    for c in fetch(step, slot):
        c.wait()
    return slot


def _proj_kernel(x_hbm, xnat_ref, wu_ref, wq_ref, gain_ref,
                 colp_ref, bre_ref, bim_ref, cre_ref, cim_ref, am_ref, d_ref,
                 u_ref, xn_ref, qkv_ref, w1_ref, w3_ref, pwr_ref, pwi_ref, sm_ref, xbuf, sem,
                 *, levels, norm_heads):
    slot = _fetch_row_sets(x_hbm, xbuf, sem)
    proj_u = _proj_t_stages(xbuf.at[slot], wu_ref, None, u_ref, 0, xn_ref)
    proj_q = _proj_t_stages(xnat_ref, wq_ref, gain_ref, qkv_ref, norm_heads)
    prep = _ssm_prep_stages(colp_ref, bre_ref, bim_ref, cre_ref, cim_ref, am_ref, d_ref,
                            w1_ref, w3_ref, pwr_ref, pwi_ref, sm_ref, levels)
    mixed = [st for pair in zip(proj_q, proj_u) for st in pair]
    for stage in mixed[:3] + prep[:1] + mixed[3:5] + prep[1:] + mixed[5:]:
        stage()


def _projections_and_ssm_prep(x, wu_t, wq_t, qk_gain, norm_heads, colp, b_re, b_im, c_re, c_im,
                              a_meta, d_grp, levels):
    b, s, d = x.shape
    nu, nq = wu_t.shape[0], wq_t.shape[0]
    nchunks = s // CHUNK
    x4 = x.reshape(b, nchunks, CHUNK, d)
    nj = CHUNK // PROJ_U_SETS
    tm = s // nj
    g, ns, gc = SSM_GROUPS, SSM_STATE, SSM_GROUP_CH
    gpp = g // (b * nj)
    assert gpp * b * nj == g
    gspec = lambda *tail: pl.BlockSpec((gpp,) + tail, lambda i, j: (i * nj + j,) + (0,) * len(tail))
    const = lambda a: pl.BlockSpec(a.shape, lambda i, j: (0,) * a.ndim)
    return pl.pallas_call(
        functools.partial(_proj_kernel, levels=levels, norm_heads=norm_heads),
        grid=(b, nj),
        in_specs=[
            pl.BlockSpec(memory_space=pl.ANY),
            pl.BlockSpec((None, tm, d), lambda i, j: (i, j, 0)),
            const(wu_t), const(wq_t), const(qk_gain),
            gspec(ns, 4), gspec(ns, gc), gspec(ns, gc), gspec(gc, ns), gspec(gc, ns), gspec(1, TOEP),
            gspec(gc, 1),
        ],
        out_specs=[
            pl.BlockSpec((None, nu, PROJ_U_SETS * nchunks), lambda i, j: (i, 0, j)),
            pl.BlockSpec((None, PROJ_U_SETS * nchunks, d), lambda i, j: (i, j, 0)),
            pl.BlockSpec((None, nq, tm), lambda i, j: (i, 0, j)),
            gspec(2 * ns, TOEP), gspec(TOEP, TOEP + 2 * ns), gspec(levels, ns, LANES),
            gspec(levels, ns, LANES), gspec(4, ns, LANES),
        ],
        out_shape=[jax.ShapeDtypeStruct((b, nu, s), _BF16),
                   jax.ShapeDtypeStruct((b, s, d), _BF16),
                   jax.ShapeDtypeStruct((b, nq, s), _BF16),
                   jax.ShapeDtypeStruct((g, 2 * ns, TOEP), _BF16),
                   jax.ShapeDtypeStruct((g, TOEP, TOEP + 2 * ns), _BF16),
                   jax.ShapeDtypeStruct((g, levels, ns, LANES), _F32),
                   jax.ShapeDtypeStruct((g, levels, ns, LANES), _F32),
                   jax.ShapeDtypeStruct((g, 4, ns, LANES), _F32)],
        scratch_shapes=[pltpu.VMEM((2, PROJ_U_SETS * nchunks, d), x.dtype),
                        pltpu.SemaphoreType.DMA((2, PROJ_U_SETS))],
        compiler_params=pltpu.CompilerParams(
            dimension_semantics=("arbitrary", "arbitrary"), vmem_limit_bytes=VMEM_LIMIT),
        name="projections_ssm_prep",
    )(x4, x, wu_t, wq_t, qk_gain, colp, b_re, b_im, c_re, c_im, a_meta, d_grp)


def _ssm_prep_stages(colp_ref, bre_ref, bim_ref, cre_ref, cim_ref, am_ref, d_ref,
                     w1_ref, w3_ref, pwr_ref, pwi_ref, sm_ref, levels):
    ns, gc = SSM_STATE, SSM_GROUP_CH

    def split3(a):
        hi = a.astype(_BF16)
        rest = a - hi.astype(_F32)
        mid = rest.astype(_BF16)
        return hi, mid, (rest - mid.astype(_F32)).astype(_BF16)

    pick_cols = lambda a, m3: jnp.dot(jnp.concatenate(split3(a), axis=1), m3, preferred_element_type=_F32)
    pick_rows = lambda m3, a: jnp.dot(m3, jnp.concatenate(split3(a), axis=0), preferred_element_type=_F32)
    l16 = lax.broadcasted_iota(jnp.int32, (gc, TOEP), 1)
    r16 = lax.broadcasted_iota(jnp.int32, (gc, TOEP), 0)
    rc = lax.broadcasted_iota(jnp.int32, (3 * gc, TOEP), 0)
    lc = lax.broadcasted_iota(jnp.int32, (3 * gc, TOEP), 1)
    tile_c3 = (lc % gc == rc % gc).astype(_BF16)
    tau_r = lax.broadcasted_iota(jnp.int32, (3 * LANES, TOEP), 0) % LANES
    jc_l = lax.broadcasted_iota(jnp.int32, (3 * LANES, TOEP), 1)
    pick_rev3 = (tau_r == CHUNK - 1 - jc_l // gc).astype(_BF16)
    tc_r = lax.broadcasted_iota(jnp.int32, (TOEP, 3 * LANES), 0)
    tau_l = lax.broadcasted_iota(jnp.int32, (TOEP, 3 * LANES), 1) % LANES
    pick_next3 = (tau_l == tc_r // gc + 1).astype(_BF16)
    lane = lax.broadcasted_iota(jnp.int32, (ns, LANES), 1)
    lvl = jnp.clip(lane - CHUNK, 0, levels - 1)
    expo = jnp.where(lane <= CHUNK, lane,
                     jnp.where(lane < CHUNK + levels, CHUNK * jnp.left_shift(1, lvl), 0)).astype(_F32)
    is_tau = lane <= CHUNK

    def group(gl):
        lr, li, dt = colp_ref[gl, :, 0:1], colp_ref[gl, :, 1:2], colp_ref[gl, :, 2:3]
        mag = jnp.exp(lr * dt * expo)
        ang = li * dt * expo
        pw_r, pw_i = mag * jnp.cos(ang), mag * jnp.sin(ang)
        ar, ai = pw_r[:, 1:2], pw_i[:, 1:2]
        den = lr * lr + li * li
        nr, ni = ar - 1.0, ai
        fr, fi = (nr * lr + ni * li) / den, (ni * lr - nr * li) / den
        b_re, b_im = bre_ref[gl], bim_ref[gl]
        bbr_t = pick_cols(fr * b_re - fi * b_im, tile_c3)
        bbi_t = pick_cols(fr * b_im + fi * b_re, tile_c3)
        tau_re, tau_im = jnp.where(is_tau, pw_r, 0.0), jnp.where(is_tau, pw_i, 0.0)
        pr, pi = pick_cols(tau_re, pick_rev3), pick_cols(tau_im, pick_rev3)
        xr = pr * bbr_t - pi * bbi_t
        xi = pr * bbi_t + pi * bbr_t
        krev = (jnp.dot(cre_ref[gl], xr, precision=_HI, preferred_element_type=_F32)
                - jnp.dot(cim_ref[gl], xi, precision=_HI, preferred_element_type=_F32))
        krev = krev + jnp.where(l16 - (CHUNK - 1) * gc == r16, d_ref[gl], 0.0)
        rz = jnp.concatenate([krev, jnp.zeros_like(krev)], axis=1)
        toep = jnp.concatenate(
            [rz[:, (CHUNK - 1 - t) * gc:(CHUNK - 1 - t) * gc + TOEP] for t in range(CHUNK)], axis=0)
        w1_ref[gl] = jnp.concatenate([xr, xi], axis=0).astype(w1_ref.dtype)
        gar = pick_rows(pick_next3, tau_re.T)
        gai = pick_rows(pick_next3, tau_im.T)
        crt = jnp.tile(cre_ref[gl], (CHUNK, 1))
        cit = jnp.tile(cim_ref[gl], (CHUNK, 1))
        w3_ref[gl] = jnp.concatenate([toep, crt * gar - cit * gai, -(crt * gai + cit * gar)],
                                     axis=1).astype(w3_ref.dtype)
        for l in range(levels):
            pwr_ref[gl, l] = jnp.broadcast_to(pw_r[:, CHUNK + l:CHUNK + l + 1], (ns, LANES))
            pwi_ref[gl, l] = jnp.broadcast_to(pw_i[:, CHUNK + l:CHUNK + l + 1], (ns, LANES))
        am = am_ref[gl]
        sr = jnp.sum(xr * am, axis=1, keepdims=True)
        si = jnp.sum(xi * am, axis=1, keepdims=True)
        a16r, a16i = pw_r[:, CHUNK:CHUNK + 1], pw_i[:, CHUNK:CHUNK + 1]
        sm_ref[gl, 0] = jnp.broadcast_to(sr, (ns, LANES))
        sm_ref[gl, 1] = jnp.broadcast_to(si, (ns, LANES))
        sm_ref[gl, 2] = jnp.broadcast_to(a16r * sr - a16i * si, (ns, LANES))
        sm_ref[gl, 3] = jnp.broadcast_to(a16r * si + a16i * sr, (ns, LANES))

    return [functools.partial(group, gl) for gl in range(colp_ref.shape[0])]


def _ssm_stages(u_ref, w1_ref, w3_ref, pwr_ref, pwi_ref, sm_ref, y_ref, bre_ref, bim_ref, nchunks, levels):
    ns, gc, gps = SSM_STATE, SSM_GROUP_CH, GROUPS_PER_STEP
    nblk = nchunks // LANES
    blocks = [slice(h * LANES, (h + 1) * LANES) for h in range(nblk)]
    group_rows = [slice(gl * ns, (gl + 1) * ns) for gl in range(gps)]

    def chunk_operand(gl):
        r0 = gl * gc
        return jnp.concatenate(
            [u_ref[r0:r0 + gc, j * nchunks:(j + 1) * nchunks] for j in range(CHUNK)], axis=0)

    def shifted(vals, sh, first):
        if sh % LANES == 0:
            k = sh // LANES
            return [None if h < k else vals[h - k] for h in range(nblk)]
        lane = lax.broadcasted_iota(jnp.int32, (ns, LANES), 1)
        rot = [pltpu.roll(v, sh, axis=1) for v in vals]
        keep = lane >= sh
        return [jnp.where(keep, rot[h], first if h == 0 else rot[h - 1]) for h in range(nblk)]

    def state_increments():
        lane0 = lax.broadcasted_iota(jnp.int32, (ns, LANES), 1) == 0
        for gl, rs in enumerate(group_rows):
            r1 = jnp.dot(w1_ref[gl], chunk_operand(gl), preferred_element_type=_F32)
            bre_ref[rs, :] = r1[:ns]
            bim_ref[rs, :] = r1[ns:]
            bre_ref[rs, blocks[0]] += jnp.where(lane0, sm_ref[gl, 2], 0.0)
            bim_ref[rs, blocks[0]] += jnp.where(lane0, sm_ref[gl, 3], 0.0)

    def scan_level(lvl):
        for gl, rs in enumerate(group_rows):
            p_re, p_im = pwr_ref[gl, lvl], pwi_ref[gl, lvl]
            s_re = [bre_ref[rs, blk] for blk in blocks]
            s_im = [bim_ref[rs, blk] for blk in blocks]
            t_re, t_im = shifted(s_re, 1 << lvl, 0.0), shifted(s_im, 1 << lvl, 0.0)
            for h in range(nblk):
                if t_re[h] is not None:
                    bre_ref[rs, blocks[h]] = s_re[h] + p_re * t_re[h] - p_im * t_im[h]
                    bim_ref[rs, blocks[h]] = s_im[h] + p_re * t_im[h] + p_im * t_re[h]

    def entering_states():
        for gl, rs in enumerate(group_rows):
            prev_re = shifted([bre_ref[rs, blk] for blk in blocks], 1, sm_ref[gl, 0])
            prev_im = shifted([bim_ref[rs, blk] for blk in blocks], 1, sm_ref[gl, 1])
            for h in range(nblk):
                bre_ref[rs, blocks[h]] = prev_re[h]
                bim_ref[rs, blocks[h]] = prev_im[h]

    def outputs(gl):
        r0, rs = gl * gc, group_rows[gl]
        rhs = jnp.concatenate([chunk_operand(gl), bre_ref[rs, :].astype(_BF16),
                               bim_ref[rs, :].astype(_BF16)], axis=0)
        y = jnp.dot(w3_ref[gl], rhs, preferred_element_type=_F32)
        for t in range(CHUNK):
            y_ref[r0:r0 + gc, t * nchunks:(t + 1) * nchunks] = y[t * gc:(t + 1) * gc, :]

    return ([state_increments] + [functools.partial(scan_level, lvl) for lvl in range(levels)]
            + [entering_states] + [functools.partial(outputs, gl) for gl in range(gps)])


def _ssm_gates_kernel(u_ref, w1_ref, w3_ref, pwr_ref, pwi_ref, sm_ref, xn_ref, wg_ref,
                      y_ref, g_ref, bre_ref, bim_ref, *, nchunks, levels):
    ssm = _ssm_stages(u_ref, w1_ref, w3_ref, pwr_ref, pwi_ref, sm_ref, y_ref, bre_ref, bim_ref,
                      nchunks, levels)
    proj = _proj_rows_stages(xn_ref, wg_ref, g_ref)
    first, stride = 1, max(1, (levels + 1) // len(proj))
    for i, stage in enumerate(ssm):
        if proj and i >= first and (i - first) % stride == 0:
            proj.pop(0)()
        stage()
    for stage in proj:
        stage()


def _ssm_core_and_gates(u_t, w1, w3, pwr, pwi, s_meta, xn, w_gates, levels):
    b, n, s = u_t.shape
    gps, ns = GROUPS_PER_STEP, SSM_STATE
    rows = gps * SSM_GROUP_CH
    nchunks = s // CHUNK
    d = xn.shape[2]
    tm = s // (n // rows)
    tiles = s // tm
    wspec = lambda *tail: pl.BlockSpec((gps,) + tail, lambda gb, i: (gb,) + (0,) * len(tail))
    tile = lambda gb, i: ((gb * b + i) // tiles, (gb * b + i) % tiles, 0)
    return pl.pallas_call(
        functools.partial(_ssm_gates_kernel, nchunks=nchunks, levels=levels),
        grid=(n // rows, b),
        in_specs=[
            pl.BlockSpec((None, rows, s), lambda gb, i: (i, gb, 0)),
            wspec(2 * ns, TOEP), wspec(TOEP, TOEP + 2 * ns),
            wspec(levels, ns, LANES), wspec(levels, ns, LANES), wspec(4, ns, LANES),
            pl.BlockSpec((None, tm, d), tile),
            pl.BlockSpec(w_gates.shape, lambda gb, i: (0, 0)),
        ],
        out_specs=[
            pl.BlockSpec((None, rows, s), lambda gb, i: (i, gb, 0)),
            pl.BlockSpec((None, tm, w_gates.shape[1]), tile),
        ],
        out_shape=[jax.ShapeDtypeStruct((b, n, s), _F32),
                   jax.ShapeDtypeStruct((b, s, w_gates.shape[1]), _BF16)],
        scratch_shapes=[pltpu.VMEM((gps * ns, nchunks), _F32), pltpu.VMEM((gps * ns, nchunks), _F32)],
        compiler_params=pltpu.CompilerParams(
            dimension_semantics=("arbitrary", "arbitrary"), vmem_limit_bytes=VMEM_LIMIT),
        name="ssm_core_gates",
    )(u_t, w1, w3, pwr, pwi, s_meta, xn, w_gates)


def _attn_kernel(q_ref, k_ref, v_ref, kp_ref, vp_ref, km_ref, vm_ref, sink_ref, tri_ref, o_ref, acc_ref):
    first_step = pl.program_id(1) == 0
    wq = GQA_GROUP * WINDOW
    kj = lax.broadcasted_iota(jnp.int32, (WINDOW, wq), 0)
    qi = lax.broadcasted_iota(jnp.int32, (WINDOW, wq), 1) % WINDOW
    in_cur = kj <= qi
    units = [(blk, h) for blk in range(ATTN_BLOCKS) for h in range(N_KV_HEADS)]

    def kv_rows(ref, prev_ref, blk, h):
        rows = slice(h * HEAD_DIM, (h + 1) * HEAD_DIM)
        cur = ref[rows, blk * WINDOW:(blk + 1) * WINDOW]
        prev = prev_ref[rows, :] if blk == 0 else ref[rows, (blk - 1) * WINDOW:blk * WINDOW]
        return cur, prev

    scores = []
    for blk, h in units:
        q4 = jnp.concatenate(
            [q_ref[(h * GQA_GROUP + r) * HEAD_DIM:(h * GQA_GROUP + r + 1) * HEAD_DIM,
                   blk * WINDOW:(blk + 1) * WINDOW] for r in range(GQA_GROUP)], axis=1)
        k_cur, k_prev = kv_rows(k_ref, kp_ref, blk, h)
        k_meta = km_ref[h * HEAD_DIM:(h + 1) * HEAD_DIM, :]
        scores.append(tuple(lax.dot_general(k, q4, _TN, preferred_element_type=_F32)
                            for k in (k_cur, k_prev, k_meta)))
    for (blk, h), (s_cur, s_prev, s_meta) in zip(units, scores):
        if blk == 0:
            s_prev = jnp.where(first_step, -jnp.inf, s_prev)
        s_sel = jnp.where(in_cur, s_cur, s_prev)
        sink = sink_ref[h]
        m = jnp.maximum(jnp.maximum(jnp.max(s_sel, axis=0, keepdims=True),
                                    jnp.max(s_meta, axis=0, keepdims=True)), sink)
        e_sel = jnp.exp2(s_sel - m)
        e_meta = jnp.exp2(s_meta - m)
        den = (jnp.sum(e_sel, axis=0, keepdims=True) + jnp.sum(e_meta, axis=0, keepdims=True)
               + jnp.exp2(sink - m))
        e_bf = e_sel.astype(_BF16)
        p_cur = e_bf * tri_ref[...]
        p_all = jnp.concatenate([p_cur, e_bf - p_cur, e_meta.astype(_BF16)], axis=0)
        v_cur, v_prev = kv_rows(v_ref, vp_ref, blk, h)
        v_all = jnp.concatenate([v_cur, v_prev, vm_ref[h * HEAD_DIM:(h + 1) * HEAD_DIM, :]], axis=1)
        o = jnp.dot(v_all, p_all, preferred_element_type=_F32) * (1.0 / den)
        for r in range(GQA_GROUP):
            hq = h * GQA_GROUP + r
            acc_ref[hq * HEAD_DIM:(hq + 1) * HEAD_DIM, blk * WINDOW:(blk + 1) * WINDOW] = (
                o[:, r * WINDOW:(r + 1) * WINDOW])
        if h == N_KV_HEADS - 1:
            rows = slice(blk * WINDOW, (blk + 1) * WINDOW)
            o_ref[rows, :] = acc_ref[:, rows].T.astype(o_ref.dtype)


def _attention(qkv_t, qkv_meta_t, sink_rows):
    b, _, s = qkv_t.shape
    kj = lax.broadcasted_iota(jnp.int32, (WINDOW, GQA_GROUP * WINDOW), 0)
    qi = lax.broadcasted_iota(jnp.int32, (WINDOW, GQA_GROUP * WINDOW), 1) % WINDOW
    tri = (kj <= qi).astype(_BF16)
    tq = ATTN_BLOCKS * WINDOW
    kblk = Q_W // KV_W
    prev_blk = lambda n: jnp.maximum(n * ATTN_BLOCKS - 1, 0)
    return pl.pallas_call(
        _attn_kernel,
        grid=(b, s // tq),
        in_specs=[
            pl.BlockSpec((None, Q_W, tq), lambda i, n: (i, 0, n)),
            pl.BlockSpec((None, KV_W, tq), lambda i, n: (i, kblk, n)),
            pl.BlockSpec((None, KV_W, tq), lambda i, n: (i, kblk + 1, n)),
            pl.BlockSpec((None, KV_W, WINDOW), lambda i, n: (i, kblk, prev_blk(n))),
            pl.BlockSpec((None, KV_W, WINDOW), lambda i, n: (i, kblk + 1, prev_blk(n))),
            pl.BlockSpec((None, KV_W, N_META), lambda i, n: (0, kblk, 0)),
            pl.BlockSpec((None, KV_W, N_META), lambda i, n: (0, kblk + 1, 0)),
            pl.BlockSpec(sink_rows.shape, lambda i, n: (0, 0, 0)),
            pl.BlockSpec(tri.shape, lambda i, n: (0, 0)),
        ],
        out_specs=pl.BlockSpec((None, tq, Q_W), lambda i, n: (i, n, 0)),
        out_shape=jax.ShapeDtypeStruct((b, s, Q_W), _F32),
        scratch_shapes=[pltpu.VMEM((Q_W, tq), _F32)],
        compiler_params=pltpu.CompilerParams(
            dimension_semantics=("arbitrary", "arbitrary"), vmem_limit_bytes=VMEM_LIMIT),
        name="swa_attention",
    )(qkv_t, qkv_t, qkv_t, qkv_t, qkv_t, qkv_meta_t, qkv_meta_t, sink_rows, tri)


def _tail_kernel(x_hbm, attn_hbm, y_ref, g_ref, wglu_ref, abn_ref, sbn_ref, wout_ref,
                 wfi_ref, wfo_ref, o_hbm, xbuf, abuf, obuf, hn_ref, part_ref, sem_x, sem_a, sem_o,
                 *, ntiles, nj):
    d = D_MODEL
    step = pl.program_id(0)
    slot = step % 2

    def fetch(t, sl):
        bb, jj = t // nj, t % nj
        return (pltpu.make_async_copy(*_row_set_copy(x_hbm, xbuf, sem_x, bb, jj, sl)),
                pltpu.make_async_copy(*_row_set_copy(attn_hbm, abuf, sem_a, bb, jj, sl)))

    def put(t, sl):
        dst, src, sem = _row_set_copy(o_hbm, obuf, sem_o, t // nj, t % nj, sl)
        return pltpu.make_async_copy(src, dst, sem)

    @pl.when(step == 0)
    def _():
        for c in fetch(0, 0):
            c.start()
        hn_ref[...] = jnp.zeros_like(hn_ref)
        part_ref[...] = jnp.zeros_like(part_ref)

    @pl.when(step + 1 < ntiles)
    def _():
        for c in fetch(step + 1, 1 - slot):
            c.start()

    @pl.when(step < ntiles)
    def _():
        for c in fetch(step, slot):
            c.wait()

    @pl.when(step >= 3)
    def _():
        put(step - 3, 1 - slot).wait()

    nrows = xbuf.shape[1]
    pieces = [slice(r, r + TAIL_PIECE) for r in range(0, nrows, TAIL_PIECE)]
    (lo0, hi0), (lo1, hi1) = zip(FFN_EDGES[:-1], FFN_EDGES[1:])

    def ffn_chunk(hn, lo, hi):
        gate = jnp.dot(hn, wfi_ref[:, lo:hi], preferred_element_type=_F32)
        up = jnp.dot(hn, wfi_ref[:, D_FF + lo:D_FF + hi], preferred_element_type=_F32)
        act = (jax.nn.silu(gate) * up).astype(_BF16)
        return jnp.dot(act, wfo_ref[lo:hi, :], preferred_element_type=_F32)

    front = []
    for p in pieces:
        z = jax.nn.gelu(y_ref[:, p]).astype(_BF16)
        front.append(lax.dot_general(z, wglu_ref[...], _TN, preferred_element_type=_F32))
    for p in pieces:
        obuf[1 - slot, p, :] = part_ref[p, :] + ffn_chunk(hn_ref[p, :], lo1, hi1)
    hs = []
    for p, zz in zip(pieces, front):
        ssm = zz[:, :d] * jax.nn.sigmoid(zz[:, d:])
        merged = (jax.nn.sigmoid(g_ref[p, :d].astype(_F32)) * _rms_rows(abuf[slot, p, :], abn_ref[...])
                  + jax.nn.sigmoid(g_ref[p, d:].astype(_F32)) * _rms_rows(ssm, sbn_ref[...]))
        h = xbuf[slot, p, :] + jnp.dot(merged.astype(_BF16), wout_ref[...], preferred_element_type=_F32)
        hs.append((h, _unit_rms(h).astype(_BF16)))
    for p, (h, hn) in zip(pieces, hs):
        part_ref[p, :] = h + ffn_chunk(hn, lo0, hi0)
        hn_ref[p, :] = hn

    @pl.when(step >= 1)
    def _():
        put(step - 1, 1 - slot).start()

    @pl.when(step == ntiles)
    def _():
        put(step - 1, 1 - slot).wait()
        put(step - 2, slot).wait()


def _tail(x, attn, y_t, gates, w_glu, abn, sbn, w_out, w_fi, w_fo):
    b, s, d = x.shape
    nchunks = s // CHUNK
    ntiles = b * CHUNK
    x4 = x.reshape(b, nchunks, CHUNK, d)
    a4 = attn.reshape(b, nchunks, CHUNK, d)
    const = lambda a: pl.BlockSpec(a.shape, lambda t: (0,) * a.ndim, pipeline_mode=pl.Buffered(1))
    hbm = pl.BlockSpec(memory_space=pl.ANY)
    tile = lambda t: jnp.minimum(t, ntiles - 1)
    out = pl.pallas_call(
        functools.partial(_tail_kernel, ntiles=ntiles, nj=CHUNK),
        grid=(ntiles + 1,),
        in_specs=[
            hbm, hbm,
            pl.BlockSpec((None, d, nchunks), lambda t: (tile(t) // CHUNK, 0, tile(t) % CHUNK)),
            pl.BlockSpec((None, nchunks, gates.shape[2]), lambda t: (tile(t) // CHUNK, tile(t) % CHUNK, 0)),
            const(w_glu), const(abn), const(sbn), const(w_out),
            const(w_fi), const(w_fo),
        ],
        out_specs=hbm,
        out_shape=jax.ShapeDtypeStruct(x4.shape, x.dtype),
        scratch_shapes=[pltpu.VMEM((2, nchunks, d), x.dtype), pltpu.VMEM((2, nchunks, d), attn.dtype),
                        pltpu.VMEM((2, nchunks, d), x.dtype),
                        pltpu.VMEM((nchunks, d), _BF16), pltpu.VMEM((nchunks, d), _F32),
                        pltpu.SemaphoreType.DMA((2,)), pltpu.SemaphoreType.DMA((2,)),
                        pltpu.SemaphoreType.DMA((2,))],
        compiler_params=pltpu.CompilerParams(
            dimension_semantics=("arbitrary",), vmem_limit_bytes=VMEM_LIMIT),
        name="tail",
    )(x4, a4, y_t, gates, w_glu, abn, sbn, w_out, w_fi, w_fo)
    return out.reshape(b, s, d)


def kernel(x, meta_tokens, norm_mix, w_in, q_norm, k_norm, attn_sinks, lam_re, lam_im, log_dt,
           ssm_b_re, ssm_b_im, ssm_c_re, ssm_c_im, ssm_d, w_glu, attn_branch_norm, ssm_branch_norm,
           w_out, norm_ffn, w_ffn_in, w_ffn_out):
    seq = x.shape[1]
    levels = (seq // CHUNK).bit_length() - 1
    w = w_in[0] * norm_mix[0].astype(_F32)[:, None]
    w_qkv_t = w[:, :QKV_W].T.astype(_BF16)
    w_u_t = w[:, QKV_W:QKV_W + D_MODEL].T.astype(_BF16)
    w_g = w[:, QKV_W + D_MODEL:].astype(_BF16)
    scale = HEAD_DIM ** -0.5 * LOG2_E
    qk_gain = jnp.concatenate([jnp.tile(q_norm[0] * scale, N_Q_HEADS),
                               jnp.tile(k_norm[0], N_KV_HEADS)])[:, None].astype(_F32)
    no_gain = jnp.zeros((8, 1), _F32)
    sink_rows = jnp.repeat((attn_sinks[0] * LOG2_E).reshape(N_KV_HEADS, 1, GQA_GROUP), WINDOW,
                           axis=2).astype(_F32)
    dt = jnp.exp(log_dt[0].astype(_F32))
    lr, li = lam_re[0].astype(_F32), lam_im[0].astype(_F32)
    dtb = jnp.broadcast_to(dt[:, None], lr.shape)
    colp = jnp.stack([lr, li, dtb, jnp.zeros_like(lr)], axis=2)
    d_grp = ssm_d[0].astype(_F32).reshape(SSM_GROUPS, SSM_GROUP_CH, 1)

    meta = meta_tokens.astype(_F32)[None]
    qkv_meta_t = _proj_t(meta, w_qkv_t, qk_gain, norm_heads=N_Q_HEADS + N_KV_HEADS,
                         tm=N_META, name="proj_qkv_meta")
    u_meta_t = _proj_t(meta, w_u_t, no_gain, norm_heads=0, tm=N_META, name="proj_u_meta")
    a_meta = (u_meta_t[0].astype(_F32).reshape(SSM_GROUPS, SSM_GROUP_CH, CHUNK)
              .transpose(0, 2, 1).reshape(SSM_GROUPS, 1, TOEP))

    u_t, xn, qkv_t, w1, w3, pwr, pwi, s_meta = _projections_and_ssm_prep(
        x, w_u_t, w_qkv_t, qk_gain, N_Q_HEADS + N_KV_HEADS, colp,
        ssm_b_re[0].astype(_F32), ssm_b_im[0].astype(_F32),
        ssm_c_re[0].astype(_F32), ssm_c_im[0].astype(_F32), a_meta, d_grp, levels)
    y_t, gates = _ssm_core_and_gates(u_t, w1, w3, pwr, pwi, s_meta, xn, w_g, levels)
    attn = _attention(qkv_t, qkv_meta_t, sink_rows)
    return _tail(x, attn, y_t, gates, w_glu[0].astype(_BF16), attn_branch_norm,
                 ssm_branch_norm, w_out[0].astype(_BF16),
                 (w_ffn_in[0] * norm_ffn[0].astype(_F32)[:, None]).astype(_BF16),
                 w_ffn_out[0].astype(_BF16))
```
